```python
import math
import jax, jax.numpy as jnp
from jax import lax
import numpy as np

D_MODEL = 2048
BATCH = 4
SEQ = 4096
DEPTH = 4

GRID_W = 64
CTX_LEN = 256
HEAD_DIM = 128
NA_HEADS = 8
NA_ROWS = 8
NA_COLS = 16
GQA_Q_HEADS = 8
GQA_KV_HEADS = 2
Q_BLOCK = 128
ROPE_THETA = 10000.0
ROPE_AXIS_DIM = HEAD_DIM // 2
ATTN_IN_COLS = (3 * NA_HEADS + GQA_Q_HEADS + 2 * GQA_KV_HEADS) * HEAD_DIM
ATTN_MIX_WIDTH = (NA_HEADS + GQA_Q_HEADS) * HEAD_DIM
S5_WIDTH = D_MODEL
S5_GROUP = 16
S5_GROUPS = S5_WIDTH // S5_GROUP
S5_STATE = 64
S5_DT_MIN = 0.001
S5_DT_MAX = 0.1
MOE_GROUPS = 4
MOE_EXPERTS_PER_GROUP = 8
MOE_TOP_K = 2
MOE_FFN = D_MODEL // 4

N_EVEN = (DEPTH + 1) // 2
N_ODD = DEPTH // 2
RMS_EPS = 1e-6
NEG_INF = -1e30

kernel_name = "hybrid_natten_gqa_s5_hmoe_dit"


def rms_norm(x, g):
    xf = x.astype(jnp.float32)
    y = xf * lax.rsqrt(jnp.mean(xf * xf, axis=-1, keepdims=True) + RMS_EPS)
    return (y * g.astype(jnp.float32)).astype(x.dtype)


def modulate(h, shift, scale):
    return h * (1.0 + scale) + shift


def axial_rope_angles(S):
    t = jnp.arange(S)
    row = (t // GRID_W).astype(jnp.float32)
    col = (t % GRID_W).astype(jnp.float32)
    inv = ROPE_THETA ** (-jnp.arange(0, ROPE_AXIS_DIM, 2, dtype=jnp.float32) / ROPE_AXIS_DIM)
    return row[:, None] * inv[None], col[:, None] * inv[None]


def _rotate(xp, ang):
    x1, x2 = jnp.split(xp.astype(jnp.float32), 2, axis=-1)
    cos = jnp.cos(ang)[None, :, None, :]
    sin = jnp.sin(ang)[None, :, None, :]
    return jnp.concatenate([x1 * cos - x2 * sin, x1 * sin + x2 * cos], axis=-1)


def apply_axial_rope(x, ang_row, ang_col):
    y = jnp.concatenate([_rotate(x[..., :ROPE_AXIS_DIM], ang_row),
                         _rotate(x[..., ROPE_AXIS_DIM:], ang_col)], axis=-1)
    return y.astype(x.dtype)


def context_attention(q, k, v):
    Bn, L, H, dh = q.shape
    kvh = k.shape[2]
    qg = q.reshape(Bn, L, kvh, H // kvh, dh)
    s = jnp.einsum('bqkgd,blkd->bkgql', qg, k).astype(jnp.float32) * (dh ** -0.5)
    p = jax.nn.softmax(s, axis=-1).astype(v.dtype)
    return jnp.einsum('bkgql,blkd->bqkgd', p, v).reshape(Bn, L, H, dh)


def neighbourhood_attention(q, k, v, kc, vc, rpb):
    Bn, S, H, dh = q.shape
    R = S // GRID_W
    kr = min(NA_ROWS, R)
    kw = min(NA_COLS, GRID_W)
    ar = jnp.arange(R)
    aw = jnp.arange(GRID_W)
    row_start = jnp.clip(ar - kr // 2, 0, R - kr)
    off_r = row_start[:, None] + jnp.arange(kr)[None] - ar[:, None] + (NA_ROWS - 1)
    col_start = jnp.clip(aw - kw // 2, 0, GRID_W - kw)
    col_ok = (aw[None, :] >= col_start[:, None]) & (aw[None, :] < col_start[:, None] + kw)
    off_c = jnp.clip(aw[None, :] - aw[:, None] + (NA_COLS - 1), 0, 2 * NA_COLS - 2)
    kg = k.reshape(Bn, R, GRID_W, H, dh)
    vg = v.reshape(Bn, R, GRID_W, H, dh)
    qrows = q.reshape(Bn, R, GRID_W, H, dh).transpose(1, 0, 2, 3, 4)
    rpb32 = rpb.astype(jnp.float32)
    scale = dh ** -0.5
    n_loc = kr * GRID_W

    def one_row(args):
        q_row, rs, offr = args
        k_rows = lax.dynamic_slice_in_dim(kg, rs, kr, axis=1)
        v_rows = lax.dynamic_slice_in_dim(vg, rs, kr, axis=1)
        bias = rpb32[:, offr[None, :, None], off_c[:, None, :]]
        s_loc = jnp.einsum('bqhd,bikhd->bhqik', q_row, k_rows).astype(jnp.float32) * scale + bias[None]
        s_loc = jnp.where(col_ok[:, None, :], s_loc, NEG_INF).reshape(Bn, H, GRID_W, n_loc)
        s_ctx = jnp.einsum('bqhd,blhd->bhql', q_row, kc).astype(jnp.float32) * scale
        p = jax.nn.softmax(jnp.concatenate([s_loc, s_ctx], axis=-1), axis=-1).astype(v.dtype)
        p_loc = p[..., :n_loc].reshape(Bn, H, GRID_W, kr, GRID_W)
        return (jnp.einsum('bhqik,bikhd->bqhd', p_loc, v_rows)
                + jnp.einsum('bhql,blhd->bqhd', p[..., n_loc:], vc))

    o = lax.map(one_row, (qrows, row_start, off_r))
    return o.transpose(1, 0, 2, 3, 4).reshape(Bn, S, H, dh)


def gqa_blocked(q, k, v, kc, vc):
    Bn, S, H, dh = q.shape
    kvh = k.shape[2]
    nb = S // Q_BLOCK
    qb = q.reshape(Bn, nb, Q_BLOCK, kvh, H // kvh, dh).transpose(1, 0, 2, 3, 4, 5)
    scale = dh ** -0.5

    def one_block(qblk):
        s_lat = jnp.einsum('bqkgd,bskd->bkgqs', qblk, k)
        s_ctx = jnp.einsum('bqkgd,blkd->bkgql', qblk, kc)
        s = jnp.concatenate([s_lat, s_ctx], axis=-1).astype(jnp.float32) * scale
        p = jax.nn.softmax(s, axis=-1).astype(v.dtype)
        return (jnp.einsum('bkgqs,bskd->bqkgd', p[..., :S], v)
                + jnp.einsum('bkgql,blkd->bqkgd', p[..., S:], vc))

    o = lax.map(one_block, qb)
    return o.transpose(1, 0, 2, 3, 4, 5).reshape(Bn, S, H, dh)


def hybrid_attention(hx, hc, w_in, w_out, rpb, q_gain, k_gain, ang_row, ang_col, need_ctx):
    Bn, S, _ = hx.shape
    L = hc.shape[1]
    a = NA_HEADS * HEAD_DIM
    bq = GQA_Q_HEADS * HEAD_DIM
    bkv = GQA_KV_HEADS * HEAD_DIM
    cuts = [a, 2 * a, 3 * a, 3 * a + bq, 3 * a + bq + bkv]

    def heads(p, T):
        qa, ka, va, qb, kb, vb = jnp.split(p, cuts, axis=-1)
        r = lambda t, n: t.reshape(Bn, T, n, HEAD_DIM)
        return (r(qa, NA_HEADS), r(ka, NA_HEADS), r(va, NA_HEADS),
                rms_norm(r(qb, GQA_Q_HEADS), q_gain), rms_norm(r(kb, GQA_KV_HEADS), k_gain),
                r(vb, GQA_KV_HEADS))

    qa, ka, va, qb, kb, vb = heads(hx @ w_in, S)
    qac, kac, vac, qbc, kbc, vbc = heads(hc @ w_in, L)
    qb = apply_axial_rope(qb, ang_row, ang_col)
    kb = apply_axial_rope(kb, ang_row, ang_col)
    oa = neighbourhood_attention(qa, ka, va, kac, vac, rpb)
    ob = gqa_blocked(qb, kb, vb, kbc, vbc)
    yx = jnp.concatenate([oa.reshape(Bn, S, -1), ob.reshape(Bn, S, -1)], axis=-1) @ w_out
    yc = None
    if need_ctx:
        oac = context_attention(qac, kac, vac)
        obc = context_attention(qbc, kbc, vbc)
        yc = jnp.concatenate([oac.reshape(Bn, L, -1), obc.reshape(Bn, L, -1)], axis=-1) @ w_out
    return yx, yc


def zoh_discretise(lam_re, lam_im, log_dt, b_re, b_im):
    lr = jnp.minimum(lam_re.astype(jnp.float32), -1e-4)
    li = lam_im.astype(jnp.float32)
    dt = jnp.exp(log_dt.astype(jnp.float32))[:, None]
    mag = jnp.exp(lr * dt)
    ar = mag * jnp.cos(li * dt)
    ai = mag * jnp.sin(li * dt)
    den = lr * lr + li * li
    xr = ar - 1.0
    fr = (xr * lr + ai * li) / den
    fi = (ai * lr - xr * li) / den
    br = b_re.astype(jnp.float32)
    bi = b_im.astype(jnp.float32)
    bbr = fr[..., None] * br - fi[..., None] * bi
    bbi = fr[..., None] * bi + fi[..., None] * br
    return ar, ai, bbr, bbi


def _complex_affine_combine(e1, e2):
    a1r, a1i, b1r, b1i = e1
    a2r, a2i, b2r, b2i = e2
    return (a2r * a1r - a2i * a1i, a2r * a1i + a2i * a1r,
            a2r * b1r - a2i * b1i + b2r, a2r * b1i + a2i * b1r + b2i)


def diag_scan(ar, ai, br, bi, reverse, h0=None):
    T = br.shape[1]
    if h0 is not None:
        h0r, h0i = h0
        idx = T - 1 if reverse else 0
        br = br.at[:, idx].add(ar * h0r - ai * h0i)
        bi = bi.at[:, idx].add(ar * h0i + ai * h0r)
    shape = (1, T) + ar.shape
    elems = (jnp.broadcast_to(ar, shape), jnp.broadcast_to(ai, shape), br, bi)
    _, _, hr, hi = lax.associative_scan(_complex_affine_combine, elems, reverse=reverse, axis=1)
    return hr, hi


def s5_mixer(hx, hc, w_in, lam_re, lam_im, log_dt, b_re, b_im, c_re, c_im, d_skip, w_glu, need_ctx):
    Bn, S, _ = hx.shape
    L = hc.shape[1]
    ux = (hx @ w_in).astype(jnp.float32).reshape(Bn, S, S5_GROUPS, S5_GROUP)
    uc = (hc @ w_in).astype(jnp.float32).reshape(Bn, L, S5_GROUPS, S5_GROUP)
    dsk = d_skip.astype(jnp.float32).reshape(S5_GROUPS, S5_GROUP)
    yx = ux * dsk
    yc = uc * dsk
    drive = lambda u, w: jnp.einsum('btgh,gph->btgp', u, w)
    for dirn in range(2):
        rev = dirn == 1
        ar, ai, bbr, bbi = zoh_discretise(lam_re[dirn], lam_im[dirn], log_dt[dirn], b_re[dirn], b_im[dirn])
        cr = c_re[dirn].astype(jnp.float32)
        ci = c_im[dirn].astype(jnp.float32)
        readout = lambda hr, hi: (jnp.einsum('btgp,ghp->btgh', hr, cr)
                                  - jnp.einsum('btgp,ghp->btgh', hi, ci))
        hcr, hci = diag_scan(ar, ai, drive(uc, bbr), drive(uc, bbi), rev)
        end = 0 if rev else L - 1
        hxr, hxi = diag_scan(ar, ai, drive(ux, bbr), drive(ux, bbi), rev, (hcr[:, end], hci[:, end]))
        yx = yx + readout(hxr, hxi)
        if need_ctx:
            yc = yc + readout(hcr, hci)

    def glu_out(y, T):
        z = jax.nn.gelu(y).reshape(Bn, T, S5_WIDTH).astype(hx.dtype) @ w_glu
        za, zg = jnp.split(z, 2, axis=-1)
        return za * jax.nn.sigmoid(zg)

    return glu_out(yx, S), (glu_out(yc, L) if need_ctx else None)


def hier_moe(h, w_coarse, b_coarse, w_fine, b_fine, w_gate, w_up, w_down):
    lc = (h @ w_coarse).astype(jnp.float32) + b_coarse.astype(jnp.float32)
    pc = jax.nn.softmax(lc, axis=-1)
    g_sel = jnp.argmax(lc, axis=-1)
    p_sel = jnp.take_along_axis(pc, g_sel[:, None], axis=1)
    lf = jnp.einsum('td,gde->tge', h, w_fine).astype(jnp.float32) + b_fine.astype(jnp.float32)
    lf_sel = jnp.take_along_axis(lf, g_sel[:, None, None], axis=1)[:, 0]
    top_v, top_i = lax.top_k(lf_sel, MOE_TOP_K)
    w_top = jax.nn.softmax(top_v, axis=-1) * p_sel
    w_exp = jnp.sum(jax.nn.one_hot(top_i, MOE_EXPERTS_PER_GROUP, dtype=jnp.float32)
                    * w_top[..., None], axis=1)
    gates = (jax.nn.one_hot(g_sel, MOE_GROUPS, dtype=jnp.float32)[:, :, None]
             * w_exp[:, None, :]).astype(h.dtype)
    out = jnp.zeros_like(h)
    for g in range(MOE_GROUPS):
        hid = (jax.nn.silu(jnp.einsum('td,edf->tef', h, w_gate[g]))
               * jnp.einsum('td,edf->tef', h, w_up[g]) * gates[:, g, :, None])
        out = out + jnp.einsum('tef,efd->td', hid, w_down[g])
    return out


def setup_inputs(seed: int = 0) -> dict:
    key = jax.random.key(seed)
    ks = jax.random.split(key, 32)
    f32 = jnp.float32
    D = D_MODEL
    nrm = lambda k, shape, s: jax.random.normal(k, shape, f32) * s
    G, P, Hg = S5_GROUPS, S5_STATE, S5_GROUP
    NG, EG, F = MOE_GROUPS, MOE_EXPERTS_PER_GROUP, MOE_FFN
    lam_im0 = jnp.broadcast_to(jnp.pi * jnp.arange(P, dtype=f32), (N_ODD, 2, G, P))
    return {
        "x": nrm(ks[0], (BATCH, SEQ, D), 1.0),
        "c": nrm(ks[1], (BATCH, D), 1.0),
        "ctx": nrm(ks[2], (BATCH, CTX_LEN, D), 1.0),
        "c_ctx": nrm(ks[3], (D,), 1.0),
        "ada_w": nrm(ks[4], (DEPTH, D, 6 * D), 0.5 * D ** -0.5),
        "ada_b": nrm(ks[5], (DEPTH, 6 * D), 0.02),
        "norm1_g": 1.0 + nrm(ks[6], (DEPTH, D), 0.05),
        "norm2_g": 1.0 + nrm(ks[7], (DEPTH, D), 0.05),
        "final_g": 1.0 + nrm(ks[8], (D,), 0.05),
        "attn_w_in": nrm(ks[9], (N_EVEN, D, ATTN_IN_COLS), D ** -0.5),
        "attn_w_out": nrm(ks[10], (N_EVEN, ATTN_MIX_WIDTH, D), ATTN_MIX_WIDTH ** -0.5),
        "na_rpb": nrm(ks[11], (N_EVEN, NA_HEADS, 2 * NA_ROWS - 1, 2 * NA_COLS - 1), 0.1),
        "q_gain": 1.0 + nrm(ks[12], (N_EVEN, HEAD_DIM), 0.05),
        "k_gain": 1.0 + nrm(ks[13], (N_EVEN, HEAD_DIM), 0.05),
        "s5_w_in": nrm(ks[14], (N_ODD, D, S5_WIDTH), D ** -0.5),
        "s5_lam_re": -0.5 + nrm(ks[15], (N_ODD, 2, G, P), 0.01),
        "s5_lam_im": lam_im0 + nrm(ks[16], (N_ODD, 2, G, P), 0.01),
        "s5_log_dt": jax.random.uniform(ks[17], (N_ODD, 2, G), f32,
                                        minval=math.log(S5_DT_MIN), maxval=math.log(S5_DT_MAX)),
        "s5_b_re": nrm(ks[18], (N_ODD, 2, G, P, Hg), (2 * Hg) ** -0.5),
        "s5_b_im": nrm(ks[19], (N_ODD, 2, G, P, Hg), (2 * Hg) ** -0.5),
        "s5_c_re": nrm(ks[20], (N_ODD, 2, G, Hg, P), (2 * P) ** -0.5),
        "s5_c_im": nrm(ks[21], (N_ODD, 2, G, Hg, P), (2 * P) ** -0.5),
        "s5_d": nrm(ks[22], (N_ODD, S5_WIDTH), 0.3),
        "s5_w_glu": nrm(ks[23], (N_ODD, S5_WIDTH, 2 * D), S5_WIDTH ** -0.5),
        "moe_w_coarse": nrm(ks[24], (DEPTH, D, NG), D ** -0.5),
        "moe_b_coarse": nrm(ks[25], (DEPTH, NG), 0.01),
        "moe_w_fine": nrm(ks[26], (DEPTH, NG, D, EG), D ** -0.5),
        "moe_b_fine": nrm(ks[27], (DEPTH, NG, EG), 0.01),
        "moe_w_gate": nrm(ks[28], (DEPTH, NG, EG, D, F), D ** -0.5),
        "moe_w_up": nrm(ks[29], (DEPTH, NG, EG, D, F), D ** -0.5),
        "moe_w_down": nrm(ks[30], (DEPTH, NG, EG, F, D), F ** -0.5),
    }


def reference(x, c, ctx, c_ctx, ada_w, ada_b, norm1_g, norm2_g, final_g, attn_w_in, attn_w_out,
              na_rpb, q_gain, k_gain, s5_w_in, s5_lam_re, s5_lam_im, s5_log_dt, s5_b_re, s5_b_im,
              s5_c_re, s5_c_im, s5_d, s5_w_glu, moe_w_coarse, moe_b_coarse, moe_w_fine, moe_b_fine,
              moe_w_gate, moe_w_up, moe_w_down):
    Bn, S, D = x.shape
    ang_row, ang_col = axial_rope_angles(S)
    s_c = jax.nn.silu(c)
    s_cc = jax.nn.silu(c_ctx)
    xc = ctx
    for layer in range(DEPTH):
        last = layer == DEPTH - 1
        li = layer // 2
        mod = s_c @ ada_w[layer] + ada_b[layer]
        modc = s_cc @ ada_w[layer] + ada_b[layer]
        sh1, sc1, g1, sh2, sc2, g2 = jnp.split(mod[:, None, :], 6, axis=-1)
        csh1, csc1, cg1, csh2, csc2, cg2 = jnp.split(modc, 6, axis=-1)
        hx = modulate(rms_norm(x, norm1_g[layer]), sh1, sc1)
        hc = modulate(rms_norm(xc, norm1_g[layer]), csh1, csc1)
        if layer % 2 == 0:
            yx, yc = hybrid_attention(hx, hc, attn_w_in[li], attn_w_out[li], na_rpb[li], q_gain[li],
                                      k_gain[li], ang_row, ang_col, not last)
        else:
            yx, yc = s5_mixer(hx, hc, s5_w_in[li], s5_lam_re[li], s5_lam_im[li], s5_log_dt[li],
                              s5_b_re[li], s5_b_im[li], s5_c_re[li], s5_c_im[li], s5_d[li],
                              s5_w_glu[li], not last)
        x = x + g1 * yx
        hx = modulate(rms_norm(x, norm2_g[layer]), sh2, sc2)
        moe_args = (moe_w_coarse[layer], moe_b_coarse[layer], moe_w_fine[layer], moe_b_fine[layer],
                    moe_w_gate[layer], moe_w_up[layer], moe_w_down[layer])
        if last:
            x = x + g2 * hier_moe(hx.reshape(-1, D), *moe_args).reshape(Bn, S, D)
        else:
            xc = xc + cg1 * yc
            hc = modulate(rms_norm(xc, norm2_g[layer]), csh2, csc2)
            y = hier_moe(jnp.concatenate([hx.reshape(-1, D), hc.reshape(-1, D)], axis=0), *moe_args)
            x = x + g2 * y[:Bn * S].reshape(Bn, S, D)
            xc = xc + cg2 * y[Bn * S:].reshape(xc.shape)
    return rms_norm(x, final_g)
```

```python
import functools
import math

import jax
import jax.numpy as jnp
from jax import lax
from jax.experimental import pallas as pl
from jax.experimental.pallas import tpu as pltpu

F32 = jnp.float32
BF16 = jnp.bfloat16

D_MODEL = 2048
BATCH = 4
SEQ = 4096
DEPTH = 4
GRID_W = 64
CTX_LEN = 256
HEAD_DIM = 128
NA_HEADS = 8
NA_ROWS = 8
NA_COLS = 16
GQA_Q_HEADS = 8
GQA_KV_HEADS = 2
ROPE_THETA = 10000.0
ROPE_AXIS_DIM = HEAD_DIM // 2
ATTN_IN_COLS = (3 * NA_HEADS + GQA_Q_HEADS + 2 * GQA_KV_HEADS) * HEAD_DIM
S5_GROUP = 16
S5_GROUPS = D_MODEL // S5_GROUP
S5_STATE = 64
MOE_GROUPS = 4
MOE_EXPERTS_PER_GROUP = 8
MOE_EXPERTS = MOE_GROUPS * MOE_EXPERTS_PER_GROUP
MOE_FFN = D_MODEL // 4
RMS_EPS = 1e-6
NEG_INF = -1e30

T_LAT = BATCH * SEQ
T_CTX = BATCH * CTX_LEN
T_ALL = T_LAT + T_CTX
MOD_ROWS = 8
ATT_SCALE = HEAD_DIM ** -0.5

S5_CHUNK = 16
S5_SEQ = CTX_LEN + SEQ
S5_NCHUNK = S5_SEQ // S5_CHUNK
S5_ROWS = S5_NCHUNK * 2 * BATCH
S5_CW = S5_CHUNK * S5_GROUP
S5_GB = 4

MOE_TILE = 256
MOE_SLOTS = T_ALL * 2
MOE_NT = MOE_SLOTS // MOE_TILE + MOE_EXPERTS

VMEM_LIMIT = 56 * 1024 * 1024


def _cp(*sem):
    return pltpu.CompilerParams(dimension_semantics=sem, vmem_limit_bytes=VMEM_LIMIT)


def _mod_row(i, tm):
    return jnp.minimum((i * tm) // SEQ, BATCH)


def _ada_kernel(c_ref, w_ref, b_ref, o_ref):
    s = jax.nn.silu(c_ref[...])
    o_ref[...] = jnp.dot(s, w_ref[...], precision=lax.Precision.HIGHEST,
                         preferred_element_type=F32) + b_ref[...]


def ada_mod(c8, ada_w, ada_b):
    tn = 1024
    n = ada_w.shape[-1]
    return pl.pallas_call(
        _ada_kernel,
        grid=(DEPTH, n // tn),
        in_specs=[pl.BlockSpec((MOD_ROWS, D_MODEL), lambda l, j: (0, 0)),
                  pl.BlockSpec((None, D_MODEL, tn), lambda l, j: (l, 0, j)),
                  pl.BlockSpec((None, 1, tn), lambda l, j: (l, 0, j))],
        out_specs=pl.BlockSpec((None, MOD_ROWS, tn), lambda l, j: (l, 0, j)),
        out_shape=jax.ShapeDtypeStruct((DEPTH, MOD_ROWS, n), F32),
        compiler_params=_cp("arbitrary", "arbitrary"),
        name="ada_mod",
    )(c8, ada_w, ada_b.reshape(DEPTH, 1, n))


def _norm_mod(x, g, sh, sc):
    y = x * lax.rsqrt(jnp.mean(x * x, axis=-1, keepdims=True) + RMS_EPS) * g
    return y * (1.0 + sc) + sh


def _mod_spec(layer, col, tm):
    return pl.BlockSpec((None, 1, D_MODEL),
                        lambda i, j: (layer * MOD_ROWS + _mod_row(i, tm), 0, col))


def _gate_spec(layer, col, tm, tn):
    return pl.BlockSpec((None, 1, tn),
                        lambda i, j: (layer * MOD_ROWS + _mod_row(i, tm), 0, col * (D_MODEL // tn) + j))


def _nm_mm_kernel(x_ref, g_ref, sh_ref, sc_ref, w_ref, o_ref, h_ref):
    @pl.when(pl.program_id(1) == 0)
    def _():
        h_ref[...] = _norm_mod(x_ref[...], g_ref[...], sh_ref[...], sc_ref[...]).astype(BF16)

    o_ref[...] = jnp.dot(h_ref[...], w_ref[...].astype(BF16),
                         preferred_element_type=F32).astype(o_ref.dtype)


def norm_mod_matmul(x, g_all, mod, w_all, layer, wi, sh_col, sc_col, tm=1024, tn=512):
    t = x.shape[0]
    n = w_all.shape[-1]
    return pl.pallas_call(
        _nm_mm_kernel,
        grid=(t // tm, n // tn),
        in_specs=[pl.BlockSpec((tm, D_MODEL), lambda i, j: (i, 0)),
                  pl.BlockSpec((None, 1, D_MODEL), lambda i, j: (layer, 0, 0)),
                  _mod_spec(layer, sh_col, tm),
                  _mod_spec(layer, sc_col, tm),
                  pl.BlockSpec((None, D_MODEL, tn), lambda i, j: (wi, 0, j))],
        out_specs=pl.BlockSpec((tm, tn), lambda i, j: (i, j)),
        out_shape=jax.ShapeDtypeStruct((t, n), F32),
        scratch_shapes=[pltpu.VMEM((tm, D_MODEL), BF16)],
        compiler_params=_cp("arbitrary", "arbitrary"),
        name="norm_mod_matmul",
    )(x, g_all.reshape(DEPTH, 1, D_MODEL), mod, mod, w_all)


def _proj_res_kernel(a_ref, b_ref, wa_ref, wb_ref, x_ref, gate_ref, o_ref):
    y = jnp.dot(a_ref[...], wa_ref[...].astype(BF16), preferred_element_type=F32)
    y = y + jnp.dot(b_ref[...], wb_ref[...].astype(BF16), preferred_element_type=F32)
    o_ref[...] = x_ref[...] + gate_ref[...] * y


def proj_residual(oa, ob, w_all, wi, x, mod, layer, gate_col, tm=1024, tn=512):
    t, ka = oa.shape
    kb = ob.shape[1]
    assert ka == kb
    return pl.pallas_call(
        _proj_res_kernel,
        grid=(t // tm, D_MODEL // tn),
        in_specs=[pl.BlockSpec((tm, ka), lambda i, j: (i, 0)),
                  pl.BlockSpec((tm, kb), lambda i, j: (i, 0)),
                  pl.BlockSpec((None, ka, tn), lambda i, j: (wi, 0, j)),
                  pl.BlockSpec((None, kb, tn), lambda i, j: (wi, 1, j)),
                  pl.BlockSpec((tm, tn), lambda i, j: (i, j)),
                  _gate_spec(layer, gate_col, tm, tn)],
        out_specs=pl.BlockSpec((tm, tn), lambda i, j: (i, j)),
        out_shape=jax.ShapeDtypeStruct((t, D_MODEL), F32),
        compiler_params=_cp("arbitrary", "arbitrary"),
        name="proj_residual",
    )(oa, ob, w_all, w_all, x, mod)


def _gqa_prep_kernel(p_ref, gain_ref, cos_ref, sin_ref, o_ref):
    h = pl.program_id(1)
    nqk = GQA_Q_HEADS + GQA_KV_HEADS

    @pl.when(h < nqk)
    def _():
        x = p_ref[...]
        scale = jnp.where(h < GQA_Q_HEADS, ATT_SCALE, 1.0).astype(F32)
        y = x * lax.rsqrt(jnp.mean(x * x, axis=-1, keepdims=True) + RMS_EPS) * (gain_ref[...] * scale)
        lane = lax.broadcasted_iota(jnp.int32, y.shape, 1)
        half = ROPE_AXIS_DIM // 2
        partner = jnp.where((lane % ROPE_AXIS_DIM) < half,
                            pltpu.roll(y, HEAD_DIM - half, 1), pltpu.roll(y, half, 1))
        o_ref[...] = (y * cos_ref[...] + partner * sin_ref[...]).astype(BF16)

    @pl.when(h >= nqk)
    def _():
        o_ref[...] = p_ref[...].astype(BF16)


def gqa_prep(p, gains, cos_t, sin_t, tm=512):
    t = p.shape[0]
    nh = GQA_Q_HEADS + 2 * GQA_KV_HEADS
    col0 = 3 * NA_HEADS
    lat_tiles = T_LAT // tm
    per_seq = SEQ // tm

    def tab_idx(i, h):
        return (jnp.where(i < lat_tiles, i % per_seq, per_seq), 0)

    return pl.pallas_call(
        _gqa_prep_kernel,
        grid=(t // tm, nh),
        in_specs=[pl.BlockSpec((tm, HEAD_DIM), lambda i, h: (i, col0 + h)),
                  pl.BlockSpec((None, 1, HEAD_DIM),
                               lambda i, h: (jnp.where(h < GQA_Q_HEADS, 0, 1), 0, 0)),
                  pl.BlockSpec((tm, HEAD_DIM), tab_idx),
                  pl.BlockSpec((tm, HEAD_DIM), tab_idx)],
        out_specs=pl.BlockSpec((tm, HEAD_DIM), lambda i, h: (i, h)),
        out_shape=jax.ShapeDtypeStruct((t, nh * HEAD_DIM), BF16),
        compiler_params=_cp("arbitrary", "arbitrary"),
        name="gqa_prep",
    )(p, gains, cos_t, sin_t)


def rope_tables(tm=512):
    t = jnp.arange(SEQ)
    row = (t // GRID_W).astype(F32)
    col = (t % GRID_W).astype(F32)
    inv = ROPE_THETA ** (-jnp.arange(0, ROPE_AXIS_DIM, 2, dtype=F32) / ROPE_AXIS_DIM)
    ar = row[:, None] * inv[None]
    ac = col[:, None] * inv[None]
    cos_t = jnp.concatenate([jnp.cos(ar), jnp.cos(ar), jnp.cos(ac), jnp.cos(ac)], axis=-1)
    sin_t = jnp.concatenate([-jnp.sin(ar), jnp.sin(ar), -jnp.sin(ac), jnp.sin(ac)], axis=-1)
    cos_t = jnp.concatenate([cos_t, jnp.ones((tm, HEAD_DIM), F32)], axis=0)
    sin_t = jnp.concatenate([sin_t, jnp.zeros((tm, HEAD_DIM), F32)], axis=0)
    return cos_t, sin_t


def _attn_kernel(*refs, nseg, group, scale):
    q_ref = refs[0]
    k_refs = refs[1:1 + nseg]
    v_refs = refs[1 + nseg:1 + 2 * nseg]
    o_ref = refs[1 + 2 * nseg]
    tq = q_ref.shape[0]
    q = jnp.concatenate([q_ref[:, g * HEAD_DIM:(g + 1) * HEAD_DIM] for g in range(group)], axis=0)
    if scale != 1.0:
        q = q.astype(F32) * scale
    q = q.astype(BF16)
    dn = (((1,), (1,)), ((), ()))
    s = [lax.dot_general(q, k[...].astype(BF16), dn, preferred_element_type=F32) for k in k_refs]
    m = functools.reduce(jnp.maximum, [jnp.max(x, axis=-1, keepdims=True) for x in s])
    l = 0.0
    acc = 0.0
    for x, v in zip(s, v_refs):
        p = jnp.exp(x - m)
        l = l + jnp.sum(p, axis=-1, keepdims=True)
        acc = acc + jnp.dot(p.astype(BF16), v[...].astype(BF16), preferred_element_type=F32)
    o = acc / l
    for g in range(group):
        o_ref[:, g * HEAD_DIM:(g + 1) * HEAD_DIM] = o[g * tq:(g + 1) * tq].astype(o_ref.dtype)


def attention(q_arr, q_idx, kv_arrs, k_idx, v_idx, kv_rows, grid, tq, group, scale, out_rows, out_cols, o_idx, name):
    nseg = len(kv_rows)
    in_specs = [pl.BlockSpec((tq, group * HEAD_DIM), q_idx)]
    in_specs += [pl.BlockSpec((kv_rows[i], HEAD_DIM), k_idx[i]) for i in range(nseg)]
    in_specs += [pl.BlockSpec((kv_rows[i], HEAD_DIM), v_idx[i]) for i in range(nseg)]
    return pl.pallas_call(
        functools.partial(_attn_kernel, nseg=nseg, group=group, scale=scale),
        grid=grid,
        in_specs=in_specs,
        out_specs=pl.BlockSpec((tq, group * HEAD_DIM), o_idx),
        out_shape=jax.ShapeDtypeStruct((out_rows, out_cols), BF16),
        compiler_params=_cp(*(["arbitrary"] * len(grid))),
        name=name,
    )(q_arr, *kv_arrs, *kv_arrs)


NA_QROWS = 8
NA_KROWS = 16
NA_TQ = NA_QROWS * GRID_W
NA_TK = NA_KROWS * GRID_W
GRID_H = SEQ // GRID_W


def _na_kernel(q_ref, k_ref, v_ref, kc_ref, vc_ref, bias_ref, o_ref):
    rb = pl.program_id(2)
    w0 = jnp.clip(NA_QROWS * rb - (NA_KROWS - NA_QROWS) // 2, 0, GRID_H - NA_KROWS) * GRID_W
    w0 = pl.multiple_of(w0, 4 * GRID_W)
    q = (q_ref[...] * ATT_SCALE).astype(BF16)
    kw = k_ref[pl.ds(w0, NA_TK), :].astype(BF16)
    vw = v_ref[pl.ds(w0, NA_TK), :].astype(BF16)
    dn = (((1,), (1,)), ((), ()))
    s_loc = lax.dot_general(q, kw, dn, preferred_element_type=F32) + bias_ref[...]
    s_ctx = lax.dot_general(q, kc_ref[...].astype(BF16), dn, preferred_element_type=F32)
    m = jnp.maximum(jnp.max(s_loc, axis=-1, keepdims=True), jnp.max(s_ctx, axis=-1, keepdims=True))
    p_loc = jnp.exp(s_loc - m)
    p_ctx = jnp.exp(s_ctx - m)
    l = jnp.sum(p_loc, axis=-1, keepdims=True) + jnp.sum(p_ctx, axis=-1, keepdims=True)
    acc = jnp.dot(p_loc.astype(BF16), vw, preferred_element_type=F32)
    acc = acc + jnp.dot(p_ctx.astype(BF16), vc_ref[...].astype(BF16), preferred_element_type=F32)
    o_ref[...] = (acc / l).astype(o_ref.dtype)


def na_bias_table(rpb):
    aw = jnp.arange(GRID_W)
    col_start = jnp.clip(aw - NA_COLS // 2, 0, GRID_W - NA_COLS)
    col_ok = (aw[None, :] >= col_start[:, None]) & (aw[None, :] < col_start[:, None] + NA_COLS)
    off_c = jnp.clip(aw[None, :] - aw[:, None] + (NA_COLS - 1), 0, 2 * NA_COLS - 2)
    nblk = GRID_H // NA_QROWS
    out = []
    for rb in (0, nblk // 2, nblk - 1):
        qr = NA_QROWS * rb + jnp.arange(NA_QROWS)
        w0 = min(max(NA_QROWS * rb - (NA_KROWS - NA_QROWS) // 2, 0), GRID_H - NA_KROWS)
        kr = w0 + jnp.arange(NA_KROWS)
        rs = jnp.clip(qr - NA_ROWS // 2, 0, GRID_H - NA_ROWS)
        row_ok = (kr[None, :] >= rs[:, None]) & (kr[None, :] < rs[:, None] + NA_ROWS)
        off_r = jnp.clip(kr[None, :] - qr[:, None] + (NA_ROWS - 1), 0, 2 * NA_ROWS - 2)
        b = rpb.astype(F32)[:, off_r[:, None, :, None], off_c[None, :, None, :]]
        ok = row_ok[:, None, :, None] & col_ok[None, :, None, :]
        out.append(jnp.where(ok[None], b, NEG_INF).reshape(rpb.shape[0], NA_TQ, NA_TK))
    return jnp.stack(out, axis=1)


def na_attention(p, bias):
    nblk = GRID_H // NA_QROWS
    ctx_blk0 = T_LAT // CTX_LEN

    def pat(rb):
        return jnp.where(rb == 0, 0, jnp.where(rb == nblk - 1, 2, 1))

    return pl.pallas_call(
        _na_kernel,
        grid=(NA_HEADS, BATCH, nblk),
        in_specs=[pl.BlockSpec((NA_TQ, HEAD_DIM), lambda h, b, r: (b * nblk + r, h)),
                  pl.BlockSpec((SEQ, HEAD_DIM), lambda h, b, r: (b, NA_HEADS + h)),
                  pl.BlockSpec((SEQ, HEAD_DIM), lambda h, b, r: (b, 2 * NA_HEADS + h)),
                  pl.BlockSpec((CTX_LEN, HEAD_DIM), lambda h, b, r: (ctx_blk0 + b, NA_HEADS + h)),
                  pl.BlockSpec((CTX_LEN, HEAD_DIM), lambda h, b, r: (ctx_blk0 + b, 2 * NA_HEADS + h)),
                  pl.BlockSpec((None, None, NA_TQ, NA_TK), lambda h, b, r: (h, pat(r), 0, 0))],
        out_specs=pl.BlockSpec((NA_TQ, HEAD_DIM), lambda h, b, r: (b * nblk + r, h)),
        out_shape=jax.ShapeDtypeStruct((T_LAT, NA_HEADS * HEAD_DIM), BF16),
        compiler_params=_cp("arbitrary", "arbitrary", "arbitrary"),
        name="na_attention",
    )(p, p, p, p, p, bias)


def s5_weights(lam_re, lam_im, log_dt, b_re, b_im, c_re, c_im):
    hi = lax.Precision.HIGHEST
    L = S5_CHUNK
    lr = jnp.minimum(lam_re.astype(F32), -1e-4)
    li = lam_im.astype(F32)
    dt = jnp.exp(log_dt.astype(F32))[..., None]
    mag = jnp.exp(lr * dt)
    ar = mag * jnp.cos(li * dt)
    ai = mag * jnp.sin(li * dt)
    den = lr * lr + li * li
    xr = ar - 1.0
    fr = (xr * lr + ai * li) / den
    fi = (ai * lr - xr * li) / den
    br = b_re.astype(F32)
    bi = b_im.astype(F32)
    bbr = fr[..., None] * br - fi[..., None] * bi
    bbi = fr[..., None] * bi + fi[..., None] * br
    k = jnp.arange(L + 1, dtype=F32)[:, None, None, None]
    pm = jnp.exp(lr * dt * k)
    pr = pm * jnp.cos(li * dt * k)
    pi = pm * jnp.sin(li * dt * k)
    e_r = pr[..., None] * bbr - pi[..., None] * bbi
    e_i = pr[..., None] * bbi + pi[..., None] * bbr
    cr = c_re.astype(F32)
    ci = c_im.astype(F32)
    kern = (jnp.einsum('dgop,kdgpi->kdgoi', cr, e_r[:L], precision=hi)
            - jnp.einsum('dgop,kdgpi->kdgoi', ci, e_i[:L], precision=hi))
    s_i = jnp.arange(L)
    lag = s_i[None, :] - s_i[:, None]
    toep = jnp.where((lag >= 0)[:, :, None, None, None, None], kern[jnp.clip(lag, 0, L - 1)], 0.0)
    ng = lr.shape[1]
    w_toep = toep.transpose(2, 3, 0, 5, 1, 4).reshape(2, ng, S5_CW, S5_CW)
    st_r = e_r[L - 1 - s_i].transpose(1, 2, 0, 4, 3).reshape(2, ng, S5_CW, S5_STATE)
    st_i = e_i[L - 1 - s_i].transpose(1, 2, 0, 4, 3).reshape(2, ng, S5_CW, S5_STATE)
    w_state = jnp.concatenate([st_r, st_i], axis=-1)
    w_state_sw = jnp.concatenate([st_i, st_r], axis=-1)
    qr = pr[1:, :, :, None, :]
    qi = pi[1:, :, :, None, :]
    d_r = cr[None] * qr - ci[None] * qi
    d_i = cr[None] * qi + ci[None] * qr
    wo_r = d_r.transpose(1, 2, 4, 0, 3).reshape(2, ng, S5_STATE, S5_CW)
    wo_i = (-d_i).transpose(1, 2, 4, 0, 3).reshape(2, ng, S5_STATE, S5_CW)
    w_out = jnp.concatenate([wo_r, wo_i], axis=2)
    cat_dir = lambda w: jnp.concatenate([w[0], w[1]], axis=-1)
    wst = jnp.concatenate([cat_dir(w_state), cat_dir(w_state_sw)], axis=-1).astype(BF16)
    wt = cat_dir(w_toep).astype(BF16)
    wo = cat_dir(w_out).astype(BF16)
    a1 = jnp.concatenate([pr[L], pr[L]], axis=-1)
    a2 = jnp.concatenate([-pi[L], pi[L]], axis=-1)
    per_row = lambda a: jnp.repeat(a.transpose(1, 0, 2), BATCH, axis=1)
    return wst, wt, wo, per_row(a1), per_row(a2)


def _s5_kernel(u_ref, wst_ref, wt_ref, wo_ref, a1_ref, a2_ref, y_ref, s_ref, ssw_ref, hp_ref):
    gb, rows, _ = u_ref.shape
    nc = rows // 8
    sw = 2 * S5_STATE
    fwd_s = (lax.broadcasted_iota(jnp.int32, (rows, sw), 0) % 8) < BATCH
    fwd_y = (lax.broadcasted_iota(jnp.int32, (rows, S5_CW), 0) % 8) < BATCH
    for j in range(gb):
        s4 = jnp.dot(u_ref[j], wst_ref[j], preferred_element_type=F32)
        s_ref[:, j] = jnp.where(fwd_s, s4[:, 0:sw], s4[:, sw:2 * sw]).reshape(nc, 8, sw)
        ssw_ref[:, j] = jnp.where(fwd_s, s4[:, 2 * sw:3 * sw], s4[:, 3 * sw:4 * sw]).reshape(nc, 8, sw)
    a1 = a1_ref[...]
    a2 = a2_ref[...]

    def step(c, carry):
        h, hs = carry
        hp_ref[c] = h
        hn = a1 * h + a2 * hs + s_ref[c]
        hsn = a1 * hs - a2 * h + ssw_ref[c]
        return hn, hsn

    z = jnp.zeros((gb, 8, sw), F32)
    lax.fori_loop(0, nc, step, (z, z), unroll=4)
    for j in range(gb):
        hp = hp_ref[:, j].reshape(rows, sw).astype(BF16)
        y4 = (jnp.dot(u_ref[j], wt_ref[j], preferred_element_type=F32)
              + jnp.dot(hp, wo_ref[j], preferred_element_type=F32))
        y_ref[j] = jnp.where(fwd_y, y4[:, :S5_CW], y4[:, S5_CW:])


def s5_core(u_chunks, wst, wt, wo, a1, a2):
    ng, rows, _ = u_chunks.shape
    gb = S5_GB
    nc = rows // 8
    sw = 2 * S5_STATE
    blk = lambda *shape: pl.BlockSpec((gb,) + shape, lambda i: (i,) + (0,) * len(shape))
    return pl.pallas_call(
        _s5_kernel,
        grid=(ng // gb,),
        in_specs=[blk(rows, S5_CW), blk(S5_CW, 4 * sw), blk(S5_CW, 2 * S5_CW), blk(sw, 2 * S5_CW),
                  blk(8, sw), blk(8, sw)],
        out_specs=blk(rows, S5_CW),
        out_shape=jax.ShapeDtypeStruct((ng, rows, S5_CW), F32),
        scratch_shapes=[pltpu.VMEM((nc, gb, 8, sw), F32), pltpu.VMEM((nc, gb, 8, sw), F32),
                        pltpu.VMEM((nc, gb, 8, sw), F32)],
        compiler_params=_cp("arbitrary"),
        name="s5_core",
    )(u_chunks, wst, wt, wo, a1, a2)


def s5_to_chunks(u):
    ux = u[:T_LAT].reshape(BATCH, SEQ, D_MODEL)
    uc = u[T_LAT:].reshape(BATCH, CTX_LEN, D_MODEL)
    fwd = jnp.concatenate([uc, ux], axis=1)
    rev = jnp.concatenate([uc[:, ::-1], ux[:, ::-1]], axis=1)
    seq = jnp.concatenate([fwd, rev], axis=0).astype(BF16)
    seq = seq.reshape(2 * BATCH, S5_NCHUNK, S5_CHUNK, S5_GROUPS, S5_GROUP)
    return seq.transpose(3, 1, 0, 2, 4).reshape(S5_GROUPS, S5_ROWS, S5_CW)


def s5_from_chunks(y):
    y = y.reshape(S5_GROUPS, S5_NCHUNK, 2 * BATCH, S5_CHUNK, S5_GROUP)
    y = y.transpose(2, 1, 3, 0, 4).reshape(2 * BATCH, S5_SEQ, D_MODEL)
    stream = lambda s: jnp.concatenate([s[:, CTX_LEN:].reshape(T_LAT, D_MODEL),
                                        s[:, :CTX_LEN].reshape(T_CTX, D_MODEL)], axis=0)
    yf = y[:BATCH]
    yr = y[BATCH:]
    yr = jnp.concatenate([yr[:, :CTX_LEN][:, ::-1], yr[:, CTX_LEN:][:, ::-1]], axis=1)
    return stream(yf), stream(yr)


def _glu_kernel(u_ref, yf_ref, yr_ref, d_ref, wa_ref, wg_ref, x_ref, gate_ref, o_ref, h_ref):
    @pl.when(pl.program_id(1) == 0)
    def _():
        y = u_ref[...] * d_ref[...] + yf_ref[...] + yr_ref[...]
        h_ref[...] = jax.nn.gelu(y).astype(BF16)

    h = h_ref[...]
    za = jnp.dot(h, wa_ref[...].astype(BF16), preferred_element_type=F32)
    zg = jnp.dot(h, wg_ref[...].astype(BF16), preferred_element_type=F32)
    o_ref[...] = x_ref[...] + gate_ref[...] * (za * jax.nn.sigmoid(zg))


def glu_residual(u, yf, yr, d_all, w_all, wi, x, mod, layer, gate_col, tm=512, tn=512):
    t = u.shape[0]
    nj = D_MODEL // tn
    row = pl.BlockSpec((tm, D_MODEL), lambda i, j: (i, 0))
    return pl.pallas_call(
        _glu_kernel,
        grid=(t // tm, nj),
        in_specs=[row, row, row,
                  pl.BlockSpec((None, 1, D_MODEL), lambda i, j: (wi, 0, 0)),
                  pl.BlockSpec((None, D_MODEL, tn), lambda i, j: (wi, 0, j)),
                  pl.BlockSpec((None, D_MODEL, tn), lambda i, j: (wi, 0, nj + j)),
                  pl.BlockSpec((tm, tn), lambda i, j: (i, j)),
                  _gate_spec(layer, gate_col, tm, tn)],
        out_specs=pl.BlockSpec((tm, tn), lambda i, j: (i, j)),
        out_shape=jax.ShapeDtypeStruct((t, D_MODEL), F32),
        scratch_shapes=[pltpu.VMEM((tm, D_MODEL), BF16)],
        compiler_params=_cp("arbitrary", "arbitrary"),
        name="glu_residual",
    )(u, yf, yr, d_all.reshape(-1, 1, D_MODEL), w_all, w_all, x, mod)


ROUTE_LANES = 128


def _route_kernel(x_ref, g_ref, sh_ref, sc_ref, wr_ref, br_ref, h_ref, r_ref):
    h = _norm_mod(x_ref[...], g_ref[...], sh_ref[...], sc_ref[...])
    h_ref[...] = h
    lg = jnp.dot(h, wr_ref[...], precision=lax.Precision.HIGHEST, preferred_element_type=F32) + br_ref[...]
    lane = lax.broadcasted_iota(jnp.int32, lg.shape, 1)
    ninf = -jnp.inf
    coarse = lane < MOE_GROUPS
    lc = jnp.where(coarse, lg, ninf)
    mc = jnp.max(lc, axis=-1, keepdims=True)
    g_sel = jnp.min(jnp.where(lc == mc, lane, ROUTE_LANES), axis=-1, keepdims=True)
    p_sel = 1.0 / jnp.sum(jnp.where(coarse, jnp.exp(lc - mc), 0.0), axis=-1, keepdims=True)
    lo = MOE_GROUPS + MOE_EXPERTS_PER_GROUP * g_sel
    lf = jnp.where((lane >= lo) & (lane < lo + MOE_EXPERTS_PER_GROUP), lg, ninf)
    v0 = jnp.max(lf, axis=-1, keepdims=True)
    i0 = jnp.min(jnp.where(lf == v0, lane, ROUTE_LANES), axis=-1, keepdims=True)
    lf2 = jnp.where(lane == i0, ninf, lf)
    v1 = jnp.max(lf2, axis=-1, keepdims=True)
    i1 = jnp.min(jnp.where(lf2 == v1, lane, ROUTE_LANES), axis=-1, keepdims=True)
    e1 = jnp.exp(v1 - v0)
    w0 = p_sel / (1.0 + e1)
    w1 = w0 * e1
    r = jnp.where(lane == 0, (i0 - MOE_GROUPS).astype(F32),
                  jnp.where(lane == 1, (i1 - MOE_GROUPS).astype(F32),
                            jnp.where(lane == 2, w0, jnp.where(lane == 3, w1, 0.0))))
    r_ref[...] = r


def moe_route(x, g_all, mod, layer, sh_col, sc_col, w_route, b_route, tm=512):
    t = x.shape[0]
    mspec = lambda col: pl.BlockSpec((None, 1, D_MODEL),
                                     lambda i: (layer * MOD_ROWS + _mod_row(i, tm), 0, col))
    return pl.pallas_call(
        _route_kernel,
        grid=(t // tm,),
        in_specs=[pl.BlockSpec((tm, D_MODEL), lambda i: (i, 0)),
                  pl.BlockSpec((None, 1, D_MODEL), lambda i: (layer, 0, 0)),
                  mspec(sh_col), mspec(sc_col),
                  pl.BlockSpec((D_MODEL, ROUTE_LANES), lambda i: (0, 0)),
                  pl.BlockSpec((1, ROUTE_LANES), lambda i: (0, 0))],
        out_specs=[pl.BlockSpec((tm, D_MODEL), lambda i: (i, 0)),
                   pl.BlockSpec((tm, ROUTE_LANES), lambda i: (i, 0))],
        out_shape=[jax.ShapeDtypeStruct((t, D_MODEL), F32),
                   jax.ShapeDtypeStruct((t, ROUTE_LANES), F32)],
        compiler_params=_cp("arbitrary"),
        name="moe_route",
    )(x, g_all.reshape(DEPTH, 1, D_MODEL), mod, mod, w_route, b_route)


def moe_dispatch(route):
    t = route.shape[0]
    e_flat = route[:, 0:2].astype(jnp.int32).reshape(-1)
    w_flat = route[:, 2:4].reshape(-1)
    onehot = (e_flat[:, None] == jnp.arange(MOE_EXPERTS, dtype=jnp.int32)[None, :]).astype(jnp.int32)
    csum = jnp.cumsum(onehot, axis=0)
    rank = jnp.take_along_axis(csum, e_flat[:, None], axis=1)[:, 0] - 1
    counts = csum[-1]
    padded = ((counts + MOE_TILE - 1) // MOE_TILE) * MOE_TILE
    pad_end = jnp.cumsum(padded)
    dest = (pad_end - padded)[e_flat] + rank
    n_used = (pad_end[-1] // MOE_TILE).astype(jnp.int32).reshape(1)
    tok = jnp.zeros((MOE_NT * MOE_TILE,), jnp.int32).at[dest].set(jnp.arange(2 * t, dtype=jnp.int32) // 2)
    gate = jnp.zeros((MOE_NT * MOE_TILE,), F32).at[dest].set(w_flat)
    tile_expert = jnp.searchsorted(pad_end, jnp.arange(MOE_NT, dtype=jnp.int32) * MOE_TILE, side='right')
    tile_expert = jnp.minimum(tile_expert, MOE_EXPERTS - 1).astype(jnp.int32)
    return tile_expert, n_used, tok.reshape(MOE_NT, 1, MOE_TILE), gate.reshape(-1, 1), dest.reshape(t, 2)


def _expert_kernel(te_ref, nu_ref, tok_ref, tokn_ref, gate_ref, h_hbm, wg_ref, wu_ref, wd_ref,
                   o_ref, buf, sem, wgb, wub, wdb):
    i = pl.program_id(0)
    nu = nu_ref[0]

    def row_copy(t, slot, r):
        return pltpu.make_async_copy(h_hbm.at[pl.ds(t, 1), :], buf.at[slot, pl.ds(r, 1), :], sem.at[slot])

    def issue(tref, slot):
        def body(r, c):
            row_copy(tref[0, 0, r], slot, r).start()
            return c
        lax.fori_loop(0, MOE_TILE, body, 0, unroll=8)

    @pl.when(i == 0)
    def _():
        issue(tok_ref, 0)

    @pl.when(i + 1 < nu)
    def _():
        issue(tokn_ref, (i + 1) % 2)

    @pl.when(i < nu)
    def _():
        slot = i % 2

        def wbody(r, c):
            row_copy(0, slot, r).wait()
            return c
        lax.fori_loop(0, MOE_TILE, wbody, 0, unroll=8)

        @pl.when((i == 0) | (te_ref[i] != te_ref[jnp.maximum(i - 1, 0)]))
        def _():
            wgb[...] = wg_ref[...].astype(BF16)
            wub[...] = wu_ref[...].astype(BF16)
            wdb[...] = wd_ref[...].astype(BF16)

        h = buf[slot].astype(BF16)
        g = jnp.dot(h, wgb[...], preferred_element_type=F32)
        u = jnp.dot(h, wub[...], preferred_element_type=F32)
        hid = (jax.nn.silu(g) * u * gate_ref[...]).astype(BF16)
        o_ref[...] = jnp.dot(hid, wdb[...], preferred_element_type=F32)

    @pl.when(i >= nu)
    def _():
        o_ref[...] = jnp.zeros_like(o_ref)


def moe_experts(h, tile_expert, n_used, tok, gate, w_gate, w_up, w_down, layer):
    def wspec(shape):
        return pl.BlockSpec((None, None, None) + shape,
                            lambda i, te, nu: (layer, te[i] // MOE_EXPERTS_PER_GROUP,
                                               te[i] % MOE_EXPERTS_PER_GROUP, 0, 0))

    smem_tile = lambda off: pl.BlockSpec(
        (1, 1, MOE_TILE), lambda i, te, nu: (jnp.minimum(i + off, MOE_NT - 1), 0, 0),
        memory_space=pltpu.SMEM)
    grid_spec = pltpu.PrefetchScalarGridSpec(
        num_scalar_prefetch=2,
        grid=(MOE_NT,),
        in_specs=[smem_tile(0), smem_tile(1),
                  pl.BlockSpec((MOE_TILE, 1), lambda i, te, nu: (i, 0)),
                  pl.BlockSpec(memory_space=pl.ANY),
                  wspec((D_MODEL, MOE_FFN)), wspec((D_MODEL, MOE_FFN)), wspec((MOE_FFN, D_MODEL))],
        out_specs=pl.BlockSpec((MOE_TILE, D_MODEL), lambda i, te, nu: (i, 0)),
        scratch_shapes=[pltpu.VMEM((2, MOE_TILE, D_MODEL), F32),
                        pltpu.SemaphoreType.DMA((2,)),
                        pltpu.VMEM((D_MODEL, MOE_FFN), BF16),
                        pltpu.VMEM((D_MODEL, MOE_FFN), BF16),
                        pltpu.VMEM((MOE_FFN, D_MODEL), BF16)])
    return pl.pallas_call(
        _expert_kernel,
        grid_spec=grid_spec,
        out_shape=jax.ShapeDtypeStruct((MOE_NT * MOE_TILE, D_MODEL), F32),
        compiler_params=_cp("arbitrary"),
        name="moe_experts",
    )(tile_expert, n_used, tok, tok, gate, h, w_gate, w_up, w_down)


def _combine_kernel(a_ref, b_ref, x_ref, gate_ref, o_ref):
    o_ref[...] = x_ref[...] + gate_ref[...] * (a_ref[...] + b_ref[...])


def moe_combine(ya, yb, x, mod, layer, gate_col, tm=512):
    t = x.shape[0]
    row = pl.BlockSpec((tm, D_MODEL), lambda i: (i, 0))
    return pl.pallas_call(
        _combine_kernel,
        grid=(t // tm,),
        in_specs=[row, row, row,
                  pl.BlockSpec((None, 1, D_MODEL),
                               lambda i: (layer * MOD_ROWS + _mod_row(i, tm), 0, gate_col))],
        out_specs=row,
        out_shape=jax.ShapeDtypeStruct((t, D_MODEL), F32),
        compiler_params=_cp("arbitrary"),
        name="moe_combine",
    )(ya, yb, x, mod)


def _final_norm_kernel(x_ref, g_ref, o_ref):
    x = x_ref[...]
    o_ref[...] = x * lax.rsqrt(jnp.mean(x * x, axis=-1, keepdims=True) + RMS_EPS) * g_ref[...]


def final_norm(x, g, rows, tm=512):
    row = pl.BlockSpec((tm, D_MODEL), lambda i: (i, 0))
    return pl.pallas_call(
        _final_norm_kernel,
        grid=(rows // tm,),
        in_specs=[row, pl.BlockSpec((1, D_MODEL), lambda i: (0, 0))],
        out_specs=row,
        out_shape=jax.ShapeDtypeStruct((rows, D_MODEL), F32),
        compiler_params=_cp("arbitrary"),
        name="final_norm",
    )(x, g.reshape(1, D_MODEL))


def attention_layer(xs, mod, layer, norm1_g, w_in, w_out, rpb, q_gain, k_gain, cos_t, sin_t):
    li = layer // 2
    p = norm_mod_matmul(xs, norm1_g, mod, w_in, layer, li, 0, 1)
    gains = jnp.stack([q_gain[li], k_gain[li]]).reshape(2, 1, HEAD_DIM)
    qkv = gqa_prep(p, gains, cos_t, sin_t)
    kcol, vcol = GQA_Q_HEADS, GQA_Q_HEADS + GQA_KV_HEADS
    grp = GQA_Q_HEADS // GQA_KV_HEADS
    ctx0 = T_LAT // CTX_LEN

    oa = na_attention(p, na_bias_table(rpb[li]))
    tq = 128
    nq = SEQ // tq
    ob = attention(
        qkv, lambda b, k, q: (b * nq + q, k),
        [qkv, qkv],
        [lambda b, k, q: (b, kcol + k), lambda b, k, q: (ctx0 + b, kcol + k)],
        [lambda b, k, q: (b, vcol + k), lambda b, k, q: (ctx0 + b, vcol + k)],
        [SEQ, CTX_LEN], (BATCH, GQA_KV_HEADS, nq), tq, grp, 1.0,
        T_LAT, GQA_Q_HEADS * HEAD_DIM, lambda b, k, q: (b * nq + q, k), "gqa_latent")
    oac = attention(
        p, lambda b, h: (ctx0 + b, h),
        [p],
        [lambda b, h: (ctx0 + b, NA_HEADS + h)],
        [lambda b, h: (ctx0 + b, 2 * NA_HEADS + h)],
        [CTX_LEN], (BATCH, NA_HEADS), CTX_LEN, 1, ATT_SCALE,
        T_CTX, NA_HEADS * HEAD_DIM, lambda b, h: (b, h), "na_context")
    obc = attention(
        qkv, lambda b, k: (ctx0 + b, k),
        [qkv],
        [lambda b, k: (ctx0 + b, kcol + k)],
        [lambda b, k: (ctx0 + b, vcol + k)],
        [CTX_LEN], (BATCH, GQA_KV_HEADS), CTX_LEN, grp, 1.0,
        T_CTX, GQA_Q_HEADS * HEAD_DIM, lambda b, k: (b, k), "gqa_context")
    oa_all = jnp.concatenate([oa, oac], axis=0)
    ob_all = jnp.concatenate([ob, obc], axis=0)
    return proj_residual(oa_all, ob_all, w_out, li, xs, mod, layer, 2)


def s5_layer(xs, mod, layer, norm1_g, w_in, lam_re, lam_im, log_dt, b_re, b_im, c_re, c_im, d_skip, w_glu):
    li = layer // 2
    u = norm_mod_matmul(xs, norm1_g, mod, w_in, layer, li, 0, 1)
    wst, wt, wo, a1, a2 = s5_weights(lam_re[li], lam_im[li], log_dt[li], b_re[li], b_im[li],
                                     c_re[li], c_im[li])
    y = s5_core(s5_to_chunks(u), wst, wt, wo, a1, a2)
    yf, yr = s5_from_chunks(y)
    return glu_residual(u, yf, yr, d_skip, w_glu, li, xs, mod, layer, 2)


def moe_layer(xs, mod, layer, norm2_g, w_coarse, b_coarse, w_fine, b_fine, w_gate, w_up, w_down):
    wf = w_fine[layer].transpose(1, 0, 2).reshape(D_MODEL, MOE_EXPERTS)
    w_route = jnp.concatenate([w_coarse[layer], wf], axis=1).astype(F32)
    w_route = jnp.pad(w_route, ((0, 0), (0, ROUTE_LANES - w_route.shape[1])))
    b_route = jnp.concatenate([b_coarse[layer], b_fine[layer].reshape(-1)]).astype(F32)
    b_route = jnp.pad(b_route, (0, ROUTE_LANES - b_route.shape[0])).reshape(1, ROUTE_LANES)
    h, route = moe_route(xs, norm2_g, mod, layer, 3, 4, w_route, b_route)
    tile_expert, n_used, tok, gate, dest = moe_dispatch(route)
    ys = moe_experts(h, tile_expert, n_used, tok, gate, w_gate, w_up, w_down, layer)
    ya = jnp.take(ys, dest[:, 0], axis=0)
    yb = jnp.take(ys, dest[:, 1], axis=0)
    return moe_combine(ya, yb, xs, mod, layer, 5)


def kernel(x, c, ctx, c_ctx, ada_w, ada_b, norm1_g, norm2_g, final_g, attn_w_in, attn_w_out, na_rpb, q_gain, k_gain, s5_w_in, s5_lam_re, s5_lam_im, s5_log_dt, s5_b_re, s5_b_im, s5_c_re, s5_c_im, s5_d, s5_w_glu, moe_w_coarse, moe_b_coarse, moe_w_fine, moe_b_fine, moe_w_gate, moe_w_up, moe_w_down):
    xs = jnp.concatenate([x.reshape(T_LAT, D_MODEL), ctx.reshape(T_CTX, D_MODEL)], axis=0)
    c8 = jnp.concatenate([c, c_ctx[None, :], jnp.zeros((MOD_ROWS - BATCH - 1, D_MODEL), F32)], axis=0)
    mod = ada_mod(c8, ada_w, ada_b).reshape(DEPTH * MOD_ROWS, 1, 6 * D_MODEL)
    cos_t, sin_t = rope_tables()
    for layer in range(DEPTH):
        if layer % 2 == 0:
            xs = attention_layer(xs, mod, layer, norm1_g, attn_w_in, attn_w_out, na_rpb, q_gain, k_gain,
                                 cos_t, sin_t)
        else:
            xs = s5_layer(xs, mod, layer, norm1_g, s5_w_in, s5_lam_re, s5_lam_im, s5_log_dt,
                          s5_b_re, s5_b_im, s5_c_re, s5_c_im, s5_d, s5_w_glu)
        xs = moe_layer(xs, mod, layer, norm2_g, moe_w_coarse, moe_b_coarse, moe_w_fine, moe_b_fine,
                       moe_w_gate, moe_w_up, moe_w_down)
    return final_norm(xs, final_g, T_LAT).reshape(BATCH, SEQ, D_MODEL)
```

```python
import functools
import math

import jax
import jax.numpy as jnp
from jax import lax
from jax.experimental import pallas as pl
from jax.experimental.pallas import tpu as pltpu

F32 = jnp.float32
BF16 = jnp.bfloat16

D_MODEL = 2048
BATCH = 4
SEQ = 4096
DEPTH = 4
GRID_W = 64
CTX_LEN = 256
HEAD_DIM = 128
NA_HEADS = 8
NA_ROWS = 8
NA_COLS = 16
GQA_Q_HEADS = 8
GQA_KV_HEADS = 2
ROPE_THETA = 10000.0
ROPE_AXIS_DIM = HEAD_DIM // 2
ATTN_IN_COLS = (3 * NA_HEADS + GQA_Q_HEADS + 2 * GQA_KV_HEADS) * HEAD_DIM
S5_GROUP = 16
S5_GROUPS = D_MODEL // S5_GROUP
S5_STATE = 64
MOE_GROUPS = 4
MOE_EXPERTS_PER_GROUP = 8
MOE_EXPERTS = MOE_GROUPS * MOE_EXPERTS_PER_GROUP
MOE_FFN = D_MODEL // 4
RMS_EPS = 1e-6
NEG_INF = -1e30

T_LAT = BATCH * SEQ
T_CTX = BATCH * CTX_LEN
T_ALL = T_LAT + T_CTX
MOD_ROWS = 8
ATT_SCALE = HEAD_DIM ** -0.5

S5_CHUNK = 16
S5_SEQ = CTX_LEN + SEQ
S5_NCHUNK = S5_SEQ // S5_CHUNK
S5_CW = S5_CHUNK * S5_GROUP
S5_GB = 4

MOE_TILE = 256
MOE_SLOTS = T_ALL * 2
MOE_NT = MOE_SLOTS // MOE_TILE + MOE_EXPERTS

VMEM_LIMIT = 56 * 1024 * 1024


def _cp(*sem):
    return pltpu.CompilerParams(dimension_semantics=sem, vmem_limit_bytes=VMEM_LIMIT)


def _mod_row(i, tm):
    return jnp.minimum((i * tm) // SEQ, BATCH)


def _ada_kernel(c_ref, w_ref, b_ref, o_ref):
    s = jax.nn.silu(c_ref[...])
    o_ref[...] = jnp.dot(s, w_ref[...], precision=lax.Precision.HIGHEST,
                         preferred_element_type=F32) + b_ref[...]


def ada_mod(c8, ada_w, ada_b):
    tn = 1024
    n = ada_w.shape[-1]
    return pl.pallas_call(
        _ada_kernel,
        grid=(DEPTH, n // tn),
        in_specs=[pl.BlockSpec((MOD_ROWS, D_MODEL), lambda l, j: (0, 0)),
                  pl.BlockSpec((None, D_MODEL, tn), lambda l, j: (l, 0, j)),
                  pl.BlockSpec((None, 1, tn), lambda l, j: (l, 0, j))],
        out_specs=pl.BlockSpec((None, MOD_ROWS, tn), lambda l, j: (l, 0, j)),
        out_shape=jax.ShapeDtypeStruct((DEPTH, MOD_ROWS, n), F32),
        compiler_params=_cp("arbitrary", "arbitrary"),
        name="ada_mod",
    )(c8, ada_w, ada_b.reshape(DEPTH, 1, n))


def _norm_mod(x, g, sh, sc):
    y = x * lax.rsqrt(jnp.mean(x * x, axis=-1, keepdims=True) + RMS_EPS) * g
    return y * (1.0 + sc) + sh


def _mod_spec(layer, col, tm):
    return pl.BlockSpec((None, 1, D_MODEL),
                        lambda i, j: (layer * MOD_ROWS + _mod_row(i, tm), 0, col))


def _gate_spec(layer, col, tm, tn):
    return pl.BlockSpec((None, 1, tn),
                        lambda i, j: (layer * MOD_ROWS + _mod_row(i, tm), 0, col * (D_MODEL // tn) + j))


def _nm_mm_kernel(x_ref, g_ref, sh_ref, sc_ref, w_ref, o_ref, h_ref):
    @pl.when(pl.program_id(1) == 0)
    def _():
        h_ref[...] = _norm_mod(x_ref[...], g_ref[...], sh_ref[...], sc_ref[...]).astype(BF16)

    o_ref[...] = jnp.dot(h_ref[...], w_ref[...].astype(BF16),
                         preferred_element_type=F32).astype(o_ref.dtype)


def norm_mod_matmul(x, g_all, mod, w_all, layer, wi, sh_col, sc_col, tm=1024, tn=512):
    t = x.shape[0]
    n = w_all.shape[-1]
    return pl.pallas_call(
        _nm_mm_kernel,
        grid=(t // tm, n // tn),
        in_specs=[pl.BlockSpec((tm, D_MODEL), lambda i, j: (i, 0)),
                  pl.BlockSpec((None, 1, D_MODEL), lambda i, j: (layer, 0, 0)),
                  _mod_spec(layer, sh_col, tm),
                  _mod_spec(layer, sc_col, tm),
                  pl.BlockSpec((None, D_MODEL, tn), lambda i, j: (wi, 0, j))],
        out_specs=pl.BlockSpec((tm, tn), lambda i, j: (i, j)),
        out_shape=jax.ShapeDtypeStruct((t, n), F32),
        scratch_shapes=[pltpu.VMEM((tm, D_MODEL), BF16)],
        compiler_params=_cp("arbitrary", "arbitrary"),
        name="norm_mod_matmul",
    )(x, g_all.reshape(DEPTH, 1, D_MODEL), mod, mod, w_all)


def _proj_res_kernel(a_ref, b_ref, wa_ref, wb_ref, x_ref, gate_ref, o_ref):
    y = jnp.dot(a_ref[...], wa_ref[...].astype(BF16), preferred_element_type=F32)
    y = y + jnp.dot(b_ref[...], wb_ref[...].astype(BF16), preferred_element_type=F32)
    o_ref[...] = x_ref[...] + gate_ref[...] * y


def proj_residual(oa, ob, w_all, wi, x, mod, layer, gate_col, tm=1024, tn=512):
    t, ka = oa.shape
    kb = ob.shape[1]
    assert ka == kb
    return pl.pallas_call(
        _proj_res_kernel,
        grid=(t // tm, D_MODEL // tn),
        in_specs=[pl.BlockSpec((tm, ka), lambda i, j: (i, 0)),
                  pl.BlockSpec((tm, kb), lambda i, j: (i, 0)),
                  pl.BlockSpec((None, ka, tn), lambda i, j: (wi, 0, j)),
                  pl.BlockSpec((None, kb, tn), lambda i, j: (wi, 1, j)),
                  pl.BlockSpec((tm, tn), lambda i, j: (i, j)),
                  _gate_spec(layer, gate_col, tm, tn)],
        out_specs=pl.BlockSpec((tm, tn), lambda i, j: (i, j)),
        out_shape=jax.ShapeDtypeStruct((t, D_MODEL), F32),
        compiler_params=_cp("arbitrary", "arbitrary"),
        name="proj_residual",
    )(oa, ob, w_all, w_all, x, mod)


def _gqa_prep_kernel(p_ref, gain_ref, cos_ref, sin_ref, o_ref):
    h = pl.program_id(1)
    nqk = GQA_Q_HEADS + GQA_KV_HEADS

    @pl.when(h < nqk)
    def _():
        x = p_ref[...]
        scale = jnp.where(h < GQA_Q_HEADS, ATT_SCALE, 1.0).astype(F32)
        y = x * lax.rsqrt(jnp.mean(x * x, axis=-1, keepdims=True) + RMS_EPS) * (gain_ref[...] * scale)
        lane = lax.broadcasted_iota(jnp.int32, y.shape, 1)
        half = ROPE_AXIS_DIM // 2
        partner = jnp.where((lane % ROPE_AXIS_DIM) < half,
                            pltpu.roll(y, HEAD_DIM - half, 1), pltpu.roll(y, half, 1))
        o_ref[...] = (y * cos_ref[...] + partner * sin_ref[...]).astype(BF16)

    @pl.when(h >= nqk)
    def _():
        o_ref[...] = p_ref[...].astype(BF16)


def gqa_prep(p, gains, cos_t, sin_t, tm=512):
    t = p.shape[0]
    nh = GQA_Q_HEADS + 2 * GQA_KV_HEADS
    col0 = 3 * NA_HEADS
    lat_tiles = T_LAT // tm
    per_seq = SEQ // tm

    def tab_idx(i, h):
        return (jnp.where(i < lat_tiles, i % per_seq, per_seq), 0)

    return pl.pallas_call(
        _gqa_prep_kernel,
        grid=(t // tm, nh),
        in_specs=[pl.BlockSpec((tm, HEAD_DIM), lambda i, h: (i, col0 + h)),
                  pl.BlockSpec((None, 1, HEAD_DIM),
                               lambda i, h: (jnp.where(h < GQA_Q_HEADS, 0, 1), 0, 0)),
                  pl.BlockSpec((tm, HEAD_DIM), tab_idx),
                  pl.BlockSpec((tm, HEAD_DIM), tab_idx)],
        out_specs=pl.BlockSpec((tm, HEAD_DIM), lambda i, h: (i, h)),
        out_shape=jax.ShapeDtypeStruct((t, nh * HEAD_DIM), BF16),
        compiler_params=_cp("arbitrary", "arbitrary"),
        name="gqa_prep",
    )(p, gains, cos_t, sin_t)


def rope_tables(tm=512):
    t = jnp.arange(SEQ)
    row = (t // GRID_W).astype(F32)
    col = (t % GRID_W).astype(F32)
    inv = ROPE_THETA ** (-jnp.arange(0, ROPE_AXIS_DIM, 2, dtype=F32) / ROPE_AXIS_DIM)
    ar = row[:, None] * inv[None]
    ac = col[:, None] * inv[None]
    cos_t = jnp.concatenate([jnp.cos(ar), jnp.cos(ar), jnp.cos(ac), jnp.cos(ac)], axis=-1)
    sin_t = jnp.concatenate([-jnp.sin(ar), jnp.sin(ar), -jnp.sin(ac), jnp.sin(ac)], axis=-1)
    cos_t = jnp.concatenate([cos_t, jnp.ones((tm, HEAD_DIM), F32)], axis=0)
    sin_t = jnp.concatenate([sin_t, jnp.zeros((tm, HEAD_DIM), F32)], axis=0)
    return cos_t, sin_t


def _attn_kernel(*refs, nseg, group, scale):
    q_ref = refs[0]
    k_refs = refs[1:1 + nseg]
    v_refs = refs[1 + nseg:1 + 2 * nseg]
    o_ref = refs[1 + 2 * nseg]
    tq = q_ref.shape[0]
    q = jnp.concatenate([q_ref[:, g * HEAD_DIM:(g + 1) * HEAD_DIM] for g in range(group)], axis=0)
    if scale != 1.0:
        q = q.astype(F32) * scale
    q = q.astype(BF16)
    dn = (((1,), (1,)), ((), ()))
    s = [lax.dot_general(q, k[...].astype(BF16), dn, preferred_element_type=F32) for k in k_refs]
    m = functools.reduce(jnp.maximum, [jnp.max(x, axis=-1, keepdims=True) for x in s])
    l = 0.0
    acc = 0.0
    for x, v in zip(s, v_refs):
        p = jnp.exp(x - m)
        l = l + jnp.sum(p, axis=-1, keepdims=True)
        acc = acc + jnp.dot(p.astype(BF16), v[...].astype(BF16), preferred_element_type=F32)
    o = acc / l
    for g in range(group):
        o_ref[:, g * HEAD_DIM:(g + 1) * HEAD_DIM] = o[g * tq:(g + 1) * tq].astype(o_ref.dtype)


def attention(q_arr, q_idx, kv_arrs, k_idx, v_idx, kv_rows, grid, tq, group, scale, out_rows, out_cols, o_idx, name):
    nseg = len(kv_rows)
    in_specs = [pl.BlockSpec((tq, group * HEAD_DIM), q_idx)]
    in_specs += [pl.BlockSpec((kv_rows[i], HEAD_DIM), k_idx[i]) for i in range(nseg)]
    in_specs += [pl.BlockSpec((kv_rows[i], HEAD_DIM), v_idx[i]) for i in range(nseg)]
    return pl.pallas_call(
        functools.partial(_attn_kernel, nseg=nseg, group=group, scale=scale),
        grid=grid,
        in_specs=in_specs,
        out_specs=pl.BlockSpec((tq, group * HEAD_DIM), o_idx),
        out_shape=jax.ShapeDtypeStruct((out_rows, out_cols), BF16),
        compiler_params=_cp(*(["arbitrary"] * len(grid))),
        name=name,
    )(q_arr, *kv_arrs, *kv_arrs)


NA_QROWS = 8
NA_KROWS = 16
NA_TQ = NA_QROWS * GRID_W
NA_TK = NA_KROWS * GRID_W
GRID_H = SEQ // GRID_W


def _na_kernel(q_ref, k_ref, v_ref, kc_ref, vc_ref, bias_ref, o_ref):
    rb = pl.program_id(2)
    w0 = jnp.clip(NA_QROWS * rb - (NA_KROWS - NA_QROWS) // 2, 0, GRID_H - NA_KROWS) * GRID_W
    w0 = pl.multiple_of(w0, 4 * GRID_W)
    q = (q_ref[...] * ATT_SCALE).astype(BF16)
    kw = k_ref[pl.ds(w0, NA_TK), :].astype(BF16)
    vw = v_ref[pl.ds(w0, NA_TK), :].astype(BF16)
    dn = (((1,), (1,)), ((), ()))
    s_loc = lax.dot_general(q, kw, dn, preferred_element_type=F32) + bias_ref[...]
    s_ctx = lax.dot_general(q, kc_ref[...].astype(BF16), dn, preferred_element_type=F32)
    m = jnp.maximum(jnp.max(s_loc, axis=-1, keepdims=True), jnp.max(s_ctx, axis=-1, keepdims=True))
    p_loc = jnp.exp(s_loc - m)
    p_ctx = jnp.exp(s_ctx - m)
    l = jnp.sum(p_loc, axis=-1, keepdims=True) + jnp.sum(p_ctx, axis=-1, keepdims=True)
    acc = jnp.dot(p_loc.astype(BF16), vw, preferred_element_type=F32)
    acc = acc + jnp.dot(p_ctx.astype(BF16), vc_ref[...].astype(BF16), preferred_element_type=F32)
    o_ref[...] = (acc / l).astype(o_ref.dtype)


def na_bias_table(rpb):
    aw = jnp.arange(GRID_W)
    col_start = jnp.clip(aw - NA_COLS // 2, 0, GRID_W - NA_COLS)
    col_ok = (aw[None, :] >= col_start[:, None]) & (aw[None, :] < col_start[:, None] + NA_COLS)
    off_c = jnp.clip(aw[None, :] - aw[:, None] + (NA_COLS - 1), 0, 2 * NA_COLS - 2)
    nh, nr, nc = rpb.shape
    pick = (off_c.reshape(1, -1) == jnp.arange(nc)[:, None]).astype(F32)
    tiles = jnp.dot(rpb.astype(F32).reshape(nh * nr, nc), pick, precision=lax.Precision.HIGHEST)
    tiles = jnp.where(col_ok.reshape(1, -1), tiles, NEG_INF).reshape(nh, nr, GRID_W, GRID_W)
    blocked = jnp.full((nh, 1, GRID_W, GRID_W), NEG_INF, F32)
    tiles = jnp.concatenate([tiles, blocked], axis=1)
    nblk = GRID_H // NA_QROWS
    out = []
    for rb in (0, nblk // 2, nblk - 1):
        qr = NA_QROWS * rb + jnp.arange(NA_QROWS)
        w0 = min(max(NA_QROWS * rb - (NA_KROWS - NA_QROWS) // 2, 0), GRID_H - NA_KROWS)
        kr = w0 + jnp.arange(NA_KROWS)
        rs = jnp.clip(qr - NA_ROWS // 2, 0, GRID_H - NA_ROWS)
        row_ok = (kr[None, :] >= rs[:, None]) & (kr[None, :] < rs[:, None] + NA_ROWS)
        off_r = jnp.where(row_ok, kr[None, :] - qr[:, None] + (NA_ROWS - 1), nr)
        b = jnp.take(tiles, off_r.reshape(-1), axis=1)
        b = b.reshape(nh, NA_QROWS, NA_KROWS, GRID_W, GRID_W).transpose(0, 1, 3, 2, 4)
        out.append(b.reshape(nh, NA_TQ, NA_TK))
    return jnp.stack(out, axis=1)


def na_attention(p, bias):
    nblk = GRID_H // NA_QROWS
    ctx_blk0 = T_LAT // CTX_LEN

    def pat(rb):
        return jnp.where(rb == 0, 0, jnp.where(rb == nblk - 1, 2, 1))

    return pl.pallas_call(
        _na_kernel,
        grid=(NA_HEADS, BATCH, nblk),
        in_specs=[pl.BlockSpec((NA_TQ, HEAD_DIM), lambda h, b, r: (b * nblk + r, h)),
                  pl.BlockSpec((SEQ, HEAD_DIM), lambda h, b, r: (b, NA_HEADS + h)),
                  pl.BlockSpec((SEQ, HEAD_DIM), lambda h, b, r: (b, 2 * NA_HEADS + h)),
                  pl.BlockSpec((CTX_LEN, HEAD_DIM), lambda h, b, r: (ctx_blk0 + b, NA_HEADS + h)),
                  pl.BlockSpec((CTX_LEN, HEAD_DIM), lambda h, b, r: (ctx_blk0 + b, 2 * NA_HEADS + h)),
                  pl.BlockSpec((None, None, NA_TQ, NA_TK), lambda h, b, r: (h, pat(r), 0, 0))],
        out_specs=pl.BlockSpec((NA_TQ, HEAD_DIM), lambda h, b, r: (b * nblk + r, h)),
        out_shape=jax.ShapeDtypeStruct((T_LAT, NA_HEADS * HEAD_DIM), BF16),
        compiler_params=_cp("arbitrary", "arbitrary", "arbitrary"),
        name="na_attention",
    )(p, p, p, p, p, bias)


def s5_weights(lam_re, lam_im, log_dt, b_re, b_im, c_re, c_im):
    hi = lax.Precision.HIGHEST
    L = S5_CHUNK
    lr = jnp.minimum(lam_re.astype(F32), -1e-4)
    li = lam_im.astype(F32)
    dt = jnp.exp(log_dt.astype(F32))[..., None]
    mag = jnp.exp(lr * dt)
    ar = mag * jnp.cos(li * dt)
    ai = mag * jnp.sin(li * dt)
    den = lr * lr + li * li
    xr = ar - 1.0
    fr = (xr * lr + ai * li) / den
    fi = (ai * lr - xr * li) / den
    br = b_re.astype(F32)
    bi = b_im.astype(F32)
    bbr = fr[..., None] * br - fi[..., None] * bi
    bbi = fr[..., None] * bi + fi[..., None] * br
    k = jnp.arange(L + 1, dtype=F32)[:, None, None, None]
    pm = jnp.exp(lr * dt * k)
    pr = pm * jnp.cos(li * dt * k)
    pi = pm * jnp.sin(li * dt * k)
    e_r = pr[..., None] * bbr - pi[..., None] * bbi
    e_i = pr[..., None] * bbi + pi[..., None] * bbr
    cr = c_re.astype(F32)
    ci = c_im.astype(F32)
    kern = (jnp.einsum('dgop,kdgpi->kdgoi', cr, e_r[:L], precision=hi)
            - jnp.einsum('dgop,kdgpi->kdgoi', ci, e_i[:L], precision=hi))
    s_i = jnp.arange(L)
    lag = s_i[None, :] - s_i[:, None]
    toep = jnp.where((lag >= 0)[:, :, None, None, None, None], kern[jnp.clip(lag, 0, L - 1)], 0.0)
    ng = lr.shape[1]
    w_toep = toep.transpose(2, 3, 0, 5, 1, 4).reshape(2, ng, S5_CW, S5_CW)
    st_r = e_r[L - 1 - s_i].transpose(1, 2, 0, 4, 3).reshape(2, ng, S5_CW, S5_STATE)
    st_i = e_i[L - 1 - s_i].transpose(1, 2, 0, 4, 3).reshape(2, ng, S5_CW, S5_STATE)
    w_state = jnp.concatenate([st_r, st_i], axis=-1)
    w_state_sw = jnp.concatenate([st_i, st_r], axis=-1)
    qr = pr[1:, :, :, None, :]
    qi = pi[1:, :, :, None, :]
    d_r = cr[None] * qr - ci[None] * qi
    d_i = cr[None] * qi + ci[None] * qr
    wo_r = d_r.transpose(1, 2, 4, 0, 3).reshape(2, ng, S5_STATE, S5_CW)
    wo_i = (-d_i).transpose(1, 2, 4, 0, 3).reshape(2, ng, S5_STATE, S5_CW)
    w_out = jnp.concatenate([wo_r, wo_i], axis=2)
    flip_rows = lambda w: w.reshape(ng, L, S5_GROUP, w.shape[-1])[:, ::-1].reshape(w.shape)
    flip_cols = lambda w: w.reshape(ng, w.shape[1], L, S5_GROUP)[:, :, ::-1].reshape(w.shape)
    wst = jnp.concatenate([w_state[0], flip_rows(w_state[1]),
                           w_state_sw[0], flip_rows(w_state_sw[1])], axis=-1).astype(BF16)
    toep_r = flip_cols(flip_rows(w_toep[1]))
    wy = jnp.concatenate([w_toep[0] + toep_r, w_out[0], flip_cols(w_out[1])], axis=1).astype(BF16)
    a1 = jnp.concatenate([pr[L], pr[L]], axis=-1)
    a2 = jnp.concatenate([-pi[L], pi[L]], axis=-1)
    per_row = lambda a: jnp.repeat(a.transpose(1, 0, 2), BATCH, axis=1)
    return wst, wy, per_row(a1), per_row(a2)


S5_CTX_CHUNKS = CTX_LEN // S5_CHUNK
S5_PACK = HEAD_DIM // S5_GROUP
S5_PACKS = S5_GROUPS // S5_PACK
S5_BROWS = S5_NCHUNK


def s5_lane_permutation():
    tl, g, hh = jnp.meshgrid(jnp.arange(S5_CHUNK), jnp.arange(S5_PACK), jnp.arange(S5_GROUP), indexing="ij")
    dst = (g * S5_CW + tl * S5_GROUP + hh).reshape(-1)
    return (dst[:, None] == jnp.arange(S5_PACK * S5_CW)[None, :]).astype(BF16)


def _s5_pack_kernel(lat_ref, ctx_ref, p_ref, o_ref):
    cols = []
    for tl in range(S5_CHUNK):
        xc = ctx_ref[pl.ds(tl, S5_CTX_CHUNKS, stride=S5_CHUNK), :]
        xl = lat_ref[pl.ds(tl, SEQ // S5_CHUNK, stride=S5_CHUNK), :]
        cols.append(jnp.concatenate([xc, xl], axis=0))
    xcat = jnp.concatenate(cols, axis=1).astype(BF16)
    o_ref[...] = jnp.dot(xcat, p_ref[...], preferred_element_type=F32).astype(BF16)


def s5_pack(u, perm):
    ctx0 = T_LAT // CTX_LEN
    w = S5_PACK * S5_CW
    return pl.pallas_call(
        _s5_pack_kernel,
        grid=(S5_PACKS, BATCH),
        in_specs=[pl.BlockSpec((SEQ, HEAD_DIM), lambda k, b: (b, k)),
                  pl.BlockSpec((CTX_LEN, HEAD_DIM), lambda k, b: (ctx0 + b, k)),
                  pl.BlockSpec((w, w), lambda k, b: (0, 0))],
        out_specs=pl.BlockSpec((None, S5_BROWS, w), lambda k, b: (k, b, 0)),
        out_shape=jax.ShapeDtypeStruct((S5_PACKS, BATCH * S5_BROWS, w), BF16),
        compiler_params=_cp("arbitrary", "arbitrary"),
        name="s5_pack",
    )(u, u, perm)


def _s5_kernel(u_ref, wst_ref, wy_ref, a1_ref, a2_ref, y_ref, s_ref, ssw_ref, hf_ref, hr_ref):
    gb = wst_ref.shape[0]
    nc = S5_NCHUNK
    sw = 2 * S5_STATE
    for j in range(gb):
        ug = u_ref[:, j * S5_CW:(j + 1) * S5_CW]
        s4 = jnp.dot(ug, wst_ref[j], preferred_element_type=F32)
        for b in range(BATCH):
            rows = s4[b * nc:(b + 1) * nc]
            s_ref[:, j, b, :] = rows[:, 0:sw]
            s_ref[:, j, BATCH + b, :] = rows[:, sw:2 * sw]
            ssw_ref[:, j, b, :] = rows[:, 2 * sw:3 * sw]
            ssw_ref[:, j, BATCH + b, :] = rows[:, 3 * sw:4 * sw]
    a1 = a1_ref[...]
    a2 = a2_ref[...]
    fwd = lax.broadcasted_iota(jnp.int32, (gb, 2 * BATCH, sw), 1) < BATCH

    def step(i, carry):
        h, hs = carry
        ri = jnp.where(i < S5_CTX_CHUNKS, S5_CTX_CHUNKS - 1 - i, nc - 1 + S5_CTX_CHUNKS - i)
        hf_ref[i] = h
        hr_ref[ri] = h
        s = jnp.where(fwd, s_ref[i], s_ref[ri])
        ssw = jnp.where(fwd, ssw_ref[i], ssw_ref[ri])
        return a1 * h + a2 * hs + s, a1 * hs - a2 * h + ssw

    z = jnp.zeros((gb, 2 * BATCH, sw), F32)
    lax.fori_loop(0, nc, step, (z, z), unroll=4)
    for j in range(gb):
        ug = u_ref[:, j * S5_CW:(j + 1) * S5_CW]
        hf = jnp.concatenate([hf_ref[:, j, b, :] for b in range(BATCH)], axis=0)
        hr = jnp.concatenate([hr_ref[:, j, BATCH + b, :] for b in range(BATCH)], axis=0)
        lhs = jnp.concatenate([ug, hf.astype(BF16), hr.astype(BF16)], axis=1)
        y_ref[:, j * S5_CW:(j + 1) * S5_CW] = jnp.dot(lhs, wy_ref[j], preferred_element_type=F32)


def s5_core(u_packed, wst, wy, a1, a2):
    gb = S5_GB
    per_pack = S5_PACK // gb
    rows = u_packed.shape[1]
    sw = 2 * S5_STATE
    blk = lambda *shape: pl.BlockSpec((gb,) + shape, lambda i: (i,) + (0,) * len(shape))
    io = pl.BlockSpec((None, rows, gb * S5_CW), lambda i: (i // per_pack, 0, i % per_pack))
    state = pltpu.VMEM((S5_NCHUNK, gb, 2 * BATCH, sw), F32)
    return pl.pallas_call(
        _s5_kernel,
        grid=(S5_GROUPS // gb,),
        in_specs=[io, blk(S5_CW, 4 * sw), blk(S5_CW + 2 * sw, S5_CW), blk(2 * BATCH, sw), blk(2 * BATCH, sw)],
        out_specs=io,
        out_shape=jax.ShapeDtypeStruct(u_packed.shape, F32),
        scratch_shapes=[state, state, state, state],
        compiler_params=_cp("arbitrary"),
        name="s5_core",
    )(u_packed, wst, wy, a1, a2)


def _s5_unpack_kernel(y_ref, pt_ref, lat_ref, ctx_ref):
    y = y_ref[...]
    hi = y.astype(BF16)
    lo = (y - hi.astype(F32)).astype(BF16)
    yn = (jnp.dot(hi, pt_ref[...], preferred_element_type=F32)
          + jnp.dot(lo, pt_ref[...], preferred_element_type=F32))
    for tl in range(S5_CHUNK):
        piece = yn[:, tl * HEAD_DIM:(tl + 1) * HEAD_DIM]
        ctx_ref[pl.ds(tl, S5_CTX_CHUNKS, stride=S5_CHUNK), :] = piece[:S5_CTX_CHUNKS]
        lat_ref[pl.ds(tl, SEQ // S5_CHUNK, stride=S5_CHUNK), :] = piece[S5_CTX_CHUNKS:]


def s5_unpack(y_packed, perm_t):
    w = S5_PACK * S5_CW
    return pl.pallas_call(
        _s5_unpack_kernel,
        grid=(S5_PACKS, BATCH),
        in_specs=[pl.BlockSpec((None, S5_BROWS, w), lambda k, b: (k, b, 0)),
                  pl.BlockSpec((w, w), lambda k, b: (0, 0))],
        out_specs=[pl.BlockSpec((SEQ, HEAD_DIM), lambda k, b: (b, k)),
                   pl.BlockSpec((CTX_LEN, HEAD_DIM), lambda k, b: (b, k))],
        out_shape=[jax.ShapeDtypeStruct((T_LAT, D_MODEL), F32), jax.ShapeDtypeStruct((T_CTX, D_MODEL), F32)],
        compiler_params=_cp("arbitrary", "arbitrary"),
        name="s5_unpack",
    )(y_packed, perm_t)


def _glu_kernel(u_ref, y_ref, d_ref, wa_ref, wg_ref, x_ref, gate_ref, o_ref, h_ref):
    @pl.when(pl.program_id(1) == 0)
    def _():
        y = u_ref[...] * d_ref[...] + y_ref[...]
        h_ref[...] = jax.nn.gelu(y).astype(BF16)

    h = h_ref[...]
    za = jnp.dot(h, wa_ref[...].astype(BF16), preferred_element_type=F32)
    zg = jnp.dot(h, wg_ref[...].astype(BF16), preferred_element_type=F32)
    o_ref[...] = x_ref[...] + gate_ref[...] * (za * jax.nn.sigmoid(zg))


def glu_residual(u, y, d_all, w_all, wi, x, mod, layer, gate_col, tm=512, tn=512):
    t = u.shape[0]
    nj = D_MODEL // tn
    row = pl.BlockSpec((tm, D_MODEL), lambda i, j: (i, 0))
    return pl.pallas_call(
        _glu_kernel,
        grid=(t // tm, nj),
        in_specs=[row, row,
                  pl.BlockSpec((None, 1, D_MODEL), lambda i, j: (wi, 0, 0)),
                  pl.BlockSpec((None, D_MODEL, tn), lambda i, j: (wi, 0, j)),
                  pl.BlockSpec((None, D_MODEL, tn), lambda i, j: (wi, 0, nj + j)),
                  pl.BlockSpec((tm, tn), lambda i, j: (i, j)),
                  _gate_spec(layer, gate_col, tm, tn)],
        out_specs=pl.BlockSpec((tm, tn), lambda i, j: (i, j)),
        out_shape=jax.ShapeDtypeStruct((t, D_MODEL), F32),
        scratch_shapes=[pltpu.VMEM((tm, D_MODEL), BF16)],
        compiler_params=_cp("arbitrary", "arbitrary"),
        name="glu_residual",
    )(u, y, d_all.reshape(-1, 1, D_MODEL), w_all, w_all, x, mod)


ROUTE_LANES = 128


def _route_kernel(x_ref, g_ref, sh_ref, sc_ref, wr_ref, br_ref, h_ref, r_ref):
    h = _norm_mod(x_ref[...], g_ref[...], sh_ref[...], sc_ref[...])
    h_ref[...] = h
    lg = jnp.dot(h, wr_ref[...], precision=lax.Precision.HIGHEST, preferred_element_type=F32) + br_ref[...]
    lane = lax.broadcasted_iota(jnp.int32, lg.shape, 1)
    ninf = -jnp.inf
    coarse = lane < MOE_GROUPS
    lc = jnp.where(coarse, lg, ninf)
    mc = jnp.max(lc, axis=-1, keepdims=True)
    g_sel = jnp.min(jnp.where(lc == mc, lane, ROUTE_LANES), axis=-1, keepdims=True)
    p_sel = 1.0 / jnp.sum(jnp.where(coarse, jnp.exp(lc - mc), 0.0), axis=-1, keepdims=True)
    lo = MOE_GROUPS + MOE_EXPERTS_PER_GROUP * g_sel
    lf = jnp.where((lane >= lo) & (lane < lo + MOE_EXPERTS_PER_GROUP), lg, ninf)
    v0 = jnp.max(lf, axis=-1, keepdims=True)
    i0 = jnp.min(jnp.where(lf == v0, lane, ROUTE_LANES), axis=-1, keepdims=True)
    lf2 = jnp.where(lane == i0, ninf, lf)
    v1 = jnp.max(lf2, axis=-1, keepdims=True)
    i1 = jnp.min(jnp.where(lf2 == v1, lane, ROUTE_LANES), axis=-1, keepdims=True)
    e1 = jnp.exp(v1 - v0)
    w0 = p_sel / (1.0 + e1)
    w1 = w0 * e1
    r = jnp.where(lane == 0, (i0 - MOE_GROUPS).astype(F32),
                  jnp.where(lane == 1, (i1 - MOE_GROUPS).astype(F32),
                            jnp.where(lane == 2, w0, jnp.where(lane == 3, w1, 0.0))))
    r_ref[...] = r


def moe_route(x, g_all, mod, layer, sh_col, sc_col, w_route, b_route, tm=512):
    t = x.shape[0]
    mspec = lambda col: pl.BlockSpec((None, 1, D_MODEL),
                                     lambda i: (layer * MOD_ROWS + _mod_row(i, tm), 0, col))
    return pl.pallas_call(
        _route_kernel,
        grid=(t // tm,),
        in_specs=[pl.BlockSpec((tm, D_MODEL), lambda i: (i, 0)),
                  pl.BlockSpec((None, 1, D_MODEL), lambda i: (layer, 0, 0)),
                  mspec(sh_col), mspec(sc_col),
                  pl.BlockSpec((D_MODEL, ROUTE_LANES), lambda i: (0, 0)),
                  pl.BlockSpec((1, ROUTE_LANES), lambda i: (0, 0))],
        out_specs=[pl.BlockSpec((tm, D_MODEL), lambda i: (i, 0)),
                   pl.BlockSpec((tm, ROUTE_LANES), lambda i: (i, 0))],
        out_shape=[jax.ShapeDtypeStruct((t, D_MODEL), F32),
                   jax.ShapeDtypeStruct((t, ROUTE_LANES), F32)],
        compiler_params=_cp("arbitrary"),
        name="moe_route",
    )(x, g_all.reshape(DEPTH, 1, D_MODEL), mod, mod, w_route, b_route)


def moe_dispatch(route):
    t = route.shape[0]
    e_flat = route[:, 0:2].astype(jnp.int32).reshape(-1)
    w_flat = route[:, 2:4].reshape(-1)
    onehot = (e_flat[:, None] == jnp.arange(MOE_EXPERTS, dtype=jnp.int32)[None, :]).astype(jnp.int32)
    csum = jnp.cumsum(onehot, axis=0)
    rank = jnp.take_along_axis(csum, e_flat[:, None], axis=1)[:, 0] - 1
    counts = csum[-1]
    padded = ((counts + MOE_TILE - 1) // MOE_TILE) * MOE_TILE
    pad_end = jnp.cumsum(padded)
    dest = (pad_end - padded)[e_flat] + rank
    n_used = (pad_end[-1] // MOE_TILE).astype(jnp.int32).reshape(1)
    tok = jnp.zeros((MOE_NT * MOE_TILE,), jnp.int32).at[dest].set(jnp.arange(2 * t, dtype=jnp.int32) // 2)
    gate = jnp.zeros((MOE_NT * MOE_TILE,), F32).at[dest].set(w_flat)
    tile_start = jnp.arange(MOE_NT, dtype=jnp.int32) * MOE_TILE
    tile_expert = jnp.sum((pad_end[None, :] <= tile_start[:, None]).astype(jnp.int32), axis=1)
    tile_expert = jnp.minimum(tile_expert, MOE_EXPERTS - 1).astype(jnp.int32)
    return tile_expert, n_used, tok.reshape(MOE_NT, 1, MOE_TILE), gate.reshape(-1, 1), dest.reshape(t, 2)


def _expert_kernel(te_ref, nu_ref, tok_ref, tokn_ref, gate_ref, h_hbm, wg_ref, wu_ref, wd_ref,
                   o_ref, buf, sem, wgb, wub, wdb):
    i = pl.program_id(0)
    nu = nu_ref[0]

    def row_copy(t, slot, r):
        return pltpu.make_async_copy(h_hbm.at[pl.ds(t, 1), :], buf.at[slot, pl.ds(r, 1), :], sem.at[slot])

    def issue(tref, slot):
        def body(r, c):
            row_copy(tref[0, 0, r], slot, r).start()
            return c
        lax.fori_loop(0, MOE_TILE, body, 0, unroll=8)

    @pl.when(i == 0)
    def _():
        issue(tok_ref, 0)

    @pl.when(i + 1 < nu)
    def _():
        issue(tokn_ref, (i + 1) % 2)

    @pl.when(i < nu)
    def _():
        slot = i % 2

        def wbody(r, c):
            row_copy(0, slot, r).wait()
            return c
        lax.fori_loop(0, MOE_TILE, wbody, 0, unroll=8)

        @pl.when((i == 0) | (te_ref[i] != te_ref[jnp.maximum(i - 1, 0)]))
        def _():
            wgb[...] = wg_ref[...].astype(BF16)
            wub[...] = wu_ref[...].astype(BF16)
            wdb[...] = wd_ref[...].astype(BF16)

        h = buf[slot].astype(BF16)
        g = jnp.dot(h, wgb[...], preferred_element_type=F32)
        u = jnp.dot(h, wub[...], preferred_element_type=F32)
        hid = (jax.nn.silu(g) * u * gate_ref[...]).astype(BF16)
        o_ref[...] = jnp.dot(hid, wdb[...], preferred_element_type=F32)

    @pl.when(i >= nu)
    def _():
        o_ref[...] = jnp.zeros_like(o_ref)


def moe_experts(h, tile_expert, n_used, tok, gate, w_gate, w_up, w_down, layer):
    def wspec(shape):
        return pl.BlockSpec((None, None, None) + shape,
                            lambda i, te, nu: (layer, te[i] // MOE_EXPERTS_PER_GROUP,
                                               te[i] % MOE_EXPERTS_PER_GROUP, 0, 0))

    smem_tile = lambda off: pl.BlockSpec(
        (1, 1, MOE_TILE), lambda i, te, nu: (jnp.minimum(i + off, MOE_NT - 1), 0, 0),
        memory_space=pltpu.SMEM)
    grid_spec = pltpu.PrefetchScalarGridSpec(
        num_scalar_prefetch=2,
        grid=(MOE_NT,),
        in_specs=[smem_tile(0), smem_tile(1),
                  pl.BlockSpec((MOE_TILE, 1), lambda i, te, nu: (i, 0)),
                  pl.BlockSpec(memory_space=pl.ANY),
                  wspec((D_MODEL, MOE_FFN)), wspec((D_MODEL, MOE_FFN)), wspec((MOE_FFN, D_MODEL))],
        out_specs=pl.BlockSpec((MOE_TILE, D_MODEL), lambda i, te, nu: (i, 0)),
        scratch_shapes=[pltpu.VMEM((2, MOE_TILE, D_MODEL), F32),
                        pltpu.SemaphoreType.DMA((2,)),
                        pltpu.VMEM((D_MODEL, MOE_FFN), BF16),
                        pltpu.VMEM((D_MODEL, MOE_FFN), BF16),
                        pltpu.VMEM((MOE_FFN, D_MODEL), BF16)])
    return pl.pallas_call(
        _expert_kernel,
        grid_spec=grid_spec,
        out_shape=jax.ShapeDtypeStruct((MOE_NT * MOE_TILE, D_MODEL), F32),
        compiler_params=_cp("arbitrary"),
        name="moe_experts",
    )(tile_expert, n_used, tok, tok, gate, h, w_gate, w_up, w_down)


def _combine_kernel(a_ref, b_ref, x_ref, gate_ref, o_ref):
    o_ref[...] = x_ref[...] + gate_ref[...] * (a_ref[...] + b_ref[...])


def moe_combine(ya, yb, x, mod, layer, gate_col, tm=512):
    t = x.shape[0]
    row = pl.BlockSpec((tm, D_MODEL), lambda i: (i, 0))
    return pl.pallas_call(
        _combine_kernel,
        grid=(t // tm,),
        in_specs=[row, row, row,
                  pl.BlockSpec((None, 1, D_MODEL),
                               lambda i: (layer * MOD_ROWS + _mod_row(i, tm), 0, gate_col))],
        out_specs=row,
        out_shape=jax.ShapeDtypeStruct((t, D_MODEL), F32),
        compiler_params=_cp("arbitrary"),
        name="moe_combine",
    )(ya, yb, x, mod)


def _final_norm_kernel(x_ref, g_ref, o_ref):
    x = x_ref[...]
    o_ref[...] = x * lax.rsqrt(jnp.mean(x * x, axis=-1, keepdims=True) + RMS_EPS) * g_ref[...]


def final_norm(x, g, rows, tm=512):
    row = pl.BlockSpec((tm, D_MODEL), lambda i: (i, 0))
    return pl.pallas_call(
        _final_norm_kernel,
        grid=(rows // tm,),
        in_specs=[row, pl.BlockSpec((1, D_MODEL), lambda i: (0, 0))],
        out_specs=row,
        out_shape=jax.ShapeDtypeStruct((rows, D_MODEL), F32),
        compiler_params=_cp("arbitrary"),
        name="final_norm",
    )(x, g.reshape(1, D_MODEL))


def attention_layer(xs, mod, layer, norm1_g, w_in, w_out, rpb, q_gain, k_gain, cos_t, sin_t):
    li = layer // 2
    p = norm_mod_matmul(xs, norm1_g, mod, w_in, layer, li, 0, 1)
    gains = jnp.stack([q_gain[li], k_gain[li]]).reshape(2, 1, HEAD_DIM)
    qkv = gqa_prep(p, gains, cos_t, sin_t)
    kcol, vcol = GQA_Q_HEADS, GQA_Q_HEADS + GQA_KV_HEADS
    grp = GQA_Q_HEADS // GQA_KV_HEADS
    ctx0 = T_LAT // CTX_LEN

    oa = na_attention(p, na_bias_table(rpb[li]))
    tq = 128
    nq = SEQ // tq
    ob = attention(
        qkv, lambda b, k, q: (b * nq + q, k),
        [qkv, qkv],
        [lambda b, k, q: (b, kcol + k), lambda b, k, q: (ctx0 + b, kcol + k)],
        [lambda b, k, q: (b, vcol + k), lambda b, k, q: (ctx0 + b, vcol + k)],
        [SEQ, CTX_LEN], (BATCH, GQA_KV_HEADS, nq), tq, grp, 1.0,
        T_LAT, GQA_Q_HEADS * HEAD_DIM, lambda b, k, q: (b * nq + q, k), "gqa_latent")
    oac = attention(
        p, lambda b, h: (ctx0 + b, h),
        [p],
        [lambda b, h: (ctx0 + b, NA_HEADS + h)],
        [lambda b, h: (ctx0 + b, 2 * NA_HEADS + h)],
        [CTX_LEN], (BATCH, NA_HEADS), CTX_LEN, 1, ATT_SCALE,
        T_CTX, NA_HEADS * HEAD_DIM, lambda b, h: (b, h), "na_context")
    obc = attention(
        qkv, lambda b, k: (ctx0 + b, k),
        [qkv],
        [lambda b, k: (ctx0 + b, kcol + k)],
        [lambda b, k: (ctx0 + b, vcol + k)],
        [CTX_LEN], (BATCH, GQA_KV_HEADS), CTX_LEN, grp, 1.0,
        T_CTX, GQA_Q_HEADS * HEAD_DIM, lambda b, k: (b, k), "gqa_context")
    oa_all = jnp.concatenate([oa, oac], axis=0)
    ob_all = jnp.concatenate([ob, obc], axis=0)
    return proj_residual(oa_all, ob_all, w_out, li, xs, mod, layer, 2)


def s5_layer(xs, mod, layer, norm1_g, w_in, lam_re, lam_im, log_dt, b_re, b_im, c_re, c_im, d_skip, w_glu):
    li = layer // 2
    u = norm_mod_matmul(xs, norm1_g, mod, w_in, layer, li, 0, 1)
    wst, wy, a1, a2 = s5_weights(lam_re[li], lam_im[li], log_dt[li], b_re[li], b_im[li], c_re[li], c_im[li])
    perm = s5_lane_permutation()
    y_lat, y_ctx = s5_unpack(s5_core(s5_pack(u, perm), wst, wy, a1, a2), perm.T)
    y = jnp.concatenate([y_lat, y_ctx], axis=0)
    return glu_residual(u, y, d_skip, w_glu, li, xs, mod, layer, 2)


def moe_layer(xs, mod, layer, norm2_g, w_coarse, b_coarse, w_fine, b_fine, w_gate, w_up, w_down):
    wf = w_fine[layer].transpose(1, 0, 2).reshape(D_MODEL, MOE_EXPERTS)
    w_route = jnp.concatenate([w_coarse[layer], wf], axis=1).astype(F32)
    w_route = jnp.pad(w_route, ((0, 0), (0, ROUTE_LANES - w_route.shape[1])))
    b_route = jnp.concatenate([b_coarse[layer], b_fine[layer].reshape(-1)]).astype(F32)
    b_route = jnp.pad(b_route, (0, ROUTE_LANES - b_route.shape[0])).reshape(1, ROUTE_LANES)
    h, route = moe_route(xs, norm2_g, mod, layer, 3, 4, w_route, b_route)
    tile_expert, n_used, tok, gate, dest = moe_dispatch(route)
    ys = moe_experts(h, tile_expert, n_used, tok, gate, w_gate, w_up, w_down, layer)
    ya = jnp.take(ys, dest[:, 0], axis=0)
    yb = jnp.take(ys, dest[:, 1], axis=0)
    return moe_combine(ya, yb, xs, mod, layer, 5)


def kernel(x, c, ctx, c_ctx, ada_w, ada_b, norm1_g, norm2_g, final_g, attn_w_in, attn_w_out, na_rpb, q_gain, k_gain, s5_w_in, s5_lam_re, s5_lam_im, s5_log_dt, s5_b_re, s5_b_im, s5_c_re, s5_c_im, s5_d, s5_w_glu, moe_w_coarse, moe_b_coarse, moe_w_fine, moe_b_fine, moe_w_gate, moe_w_up, moe_w_down):
    xs = jnp.concatenate([x.reshape(T_LAT, D_MODEL), ctx.reshape(T_CTX, D_MODEL)], axis=0)
    c8 = jnp.concatenate([c, c_ctx[None, :], jnp.zeros((MOD_ROWS - BATCH - 1, D_MODEL), F32)], axis=0)
    mod = ada_mod(c8, ada_w, ada_b).reshape(DEPTH * MOD_ROWS, 1, 6 * D_MODEL)
    cos_t, sin_t = rope_tables()
    for layer in range(DEPTH):
        if layer % 2 == 0:
            xs = attention_layer(xs, mod, layer, norm1_g, attn_w_in, attn_w_out, na_rpb, q_gain, k_gain,
                                 cos_t, sin_t)
        else:
            xs = s5_layer(xs, mod, layer, norm1_g, s5_w_in, s5_lam_re, s5_lam_im, s5_log_dt,
                          s5_b_re, s5_b_im, s5_c_re, s5_c_im, s5_d, s5_w_glu)
        xs = moe_layer(xs, mod, layer, norm2_g, moe_w_coarse, moe_b_coarse, moe_w_fine, moe_b_fine,
                       moe_w_gate, moe_w_up, moe_w_down)
    return final_norm(xs, final_g, T_LAT).reshape(BATCH, SEQ, D_MODEL)
```

```python
import functools
import math

import jax
import jax.numpy as jnp
from jax import lax
from jax.experimental import pallas as pl
from jax.experimental.pallas import tpu as pltpu

F32 = jnp.float32
BF16 = jnp.bfloat16

D_MODEL = 2048
BATCH = 4
SEQ = 4096
DEPTH = 4
GRID_W = 64
CTX_LEN = 256
HEAD_DIM = 128
NA_HEADS = 8
NA_ROWS = 8
NA_COLS = 16
GQA_Q_HEADS = 8
GQA_KV_HEADS = 2
ROPE_THETA = 10000.0
ROPE_AXIS_DIM = HEAD_DIM // 2
ATTN_IN_COLS = (3 * NA_HEADS + GQA_Q_HEADS + 2 * GQA_KV_HEADS) * HEAD_DIM
S5_GROUP = 16
S5_GROUPS = D_MODEL // S5_GROUP
S5_STATE = 64
MOE_GROUPS = 4
MOE_EXPERTS_PER_GROUP = 8
MOE_EXPERTS = MOE_GROUPS * MOE_EXPERTS_PER_GROUP
MOE_FFN = D_MODEL // 4
RMS_EPS = 1e-6
NEG_INF = -1e30

T_LAT = BATCH * SEQ
T_CTX = BATCH * CTX_LEN
T_ALL = T_LAT + T_CTX
MOD_ROWS = 8
ATT_SCALE = HEAD_DIM ** -0.5

S5_CHUNK = 16
S5_SEQ = CTX_LEN + SEQ
S5_NCHUNK = S5_SEQ // S5_CHUNK
S5_CW = S5_CHUNK * S5_GROUP
S5_GB = 4

MOE_TILE = 512
MOE_SLOTS = T_ALL * 2
MOE_NT = MOE_SLOTS // MOE_TILE + MOE_EXPERTS

VMEM_LIMIT = 56 * 1024 * 1024


def _cp(*sem):
    return pltpu.CompilerParams(dimension_semantics=sem, vmem_limit_bytes=VMEM_LIMIT)


def _mod_row(i, tm):
    return jnp.minimum((i * tm) // SEQ, BATCH)


def _ada_kernel(c_ref, w_ref, b_ref, o_ref):
    s = jax.nn.silu(c_ref[...])
    o_ref[...] = jnp.dot(s, w_ref[...], precision=lax.Precision.HIGHEST,
                         preferred_element_type=F32) + b_ref[...]


def ada_mod(c8, ada_w, ada_b):
    tn = 1024
    n = ada_w.shape[-1]
    return pl.pallas_call(
        _ada_kernel,
        grid=(DEPTH, n // tn),
        in_specs=[pl.BlockSpec((MOD_ROWS, D_MODEL), lambda l, j: (0, 0)),
                  pl.BlockSpec((None, D_MODEL, tn), lambda l, j: (l, 0, j)),
                  pl.BlockSpec((None, 1, tn), lambda l, j: (l, 0, j))],
        out_specs=pl.BlockSpec((None, MOD_ROWS, tn), lambda l, j: (l, 0, j)),
        out_shape=jax.ShapeDtypeStruct((DEPTH, MOD_ROWS, n), F32),
        compiler_params=_cp("arbitrary", "arbitrary"),
        name="ada_mod",
    )(c8, ada_w, ada_b.reshape(DEPTH, 1, n))


def _norm_mod(x, g, sh, sc):
    y = x * lax.rsqrt(jnp.mean(x * x, axis=-1, keepdims=True) + RMS_EPS) * g
    return y * (1.0 + sc) + sh


def _mod_spec(layer, col, tm):
    return pl.BlockSpec((None, 1, D_MODEL),
                        lambda i, j: (layer * MOD_ROWS + _mod_row(i, tm), 0, col))


def _gate_spec(layer, col, tm, tn):
    return pl.BlockSpec((None, 1, tn),
                        lambda i, j: (layer * MOD_ROWS + _mod_row(i, tm), 0, col * (D_MODEL // tn) + j))


def _nm_mm_kernel(x_ref, g_ref, sh_ref, sc_ref, w_ref, o_ref, h_ref):
    @pl.when(pl.program_id(1) == 0)
    def _():
        h_ref[...] = _norm_mod(x_ref[...], g_ref[...], sh_ref[...], sc_ref[...]).astype(BF16)

    o_ref[...] = jnp.dot(h_ref[...], w_ref[...].astype(BF16),
                         preferred_element_type=F32).astype(o_ref.dtype)


def norm_mod_matmul(x, g_all, mod, w_all, layer, wi, sh_col, sc_col, tm=1024, tn=512):
    t = x.shape[0]
    n = w_all.shape[-1]
    return pl.pallas_call(
        _nm_mm_kernel,
        grid=(t // tm, n // tn),
        in_specs=[pl.BlockSpec((tm, D_MODEL), lambda i, j: (i, 0)),
                  pl.BlockSpec((None, 1, D_MODEL), lambda i, j: (layer, 0, 0)),
                  _mod_spec(layer, sh_col, tm),
                  _mod_spec(layer, sc_col, tm),
                  pl.BlockSpec((None, D_MODEL, tn), lambda i, j: (wi, 0, j))],
        out_specs=pl.BlockSpec((tm, tn), lambda i, j: (i, j)),
        out_shape=jax.ShapeDtypeStruct((t, n), F32),
        scratch_shapes=[pltpu.VMEM((tm, D_MODEL), BF16)],
        compiler_params=_cp("arbitrary", "arbitrary"),
        name="norm_mod_matmul",
    )(x, g_all.reshape(DEPTH, 1, D_MODEL), mod, mod, w_all)


def _proj_res_kernel(a_ref, b_ref, wa_ref, wb_ref, x_ref, gate_ref, o_ref):
    y = jnp.dot(a_ref[...], wa_ref[...].astype(BF16), preferred_element_type=F32)
    y = y + jnp.dot(b_ref[...], wb_ref[...].astype(BF16), preferred_element_type=F32)
    o_ref[...] = x_ref[...] + gate_ref[...] * y


def proj_residual(oa, ob, w_all, wi, x, mod, layer, gate_col, tm=1024, tn=512):
    t, ka = oa.shape
    kb = ob.shape[1]
    assert ka == kb
    return pl.pallas_call(
        _proj_res_kernel,
        grid=(t // tm, D_MODEL // tn),
        in_specs=[pl.BlockSpec((tm, ka), lambda i, j: (i, 0)),
                  pl.BlockSpec((tm, kb), lambda i, j: (i, 0)),
                  pl.BlockSpec((None, ka, tn), lambda i, j: (wi, 0, j)),
                  pl.BlockSpec((None, kb, tn), lambda i, j: (wi, 1, j)),
                  pl.BlockSpec((tm, tn), lambda i, j: (i, j)),
                  _gate_spec(layer, gate_col, tm, tn)],
        out_specs=pl.BlockSpec((tm, tn), lambda i, j: (i, j)),
        out_shape=jax.ShapeDtypeStruct((t, D_MODEL), F32),
        compiler_params=_cp("arbitrary", "arbitrary"),
        name="proj_residual",
    )(oa, ob, w_all, w_all, x, mod)


def _gqa_prep_kernel(p_ref, gain_ref, cos_ref, sin_ref, o_ref):
    h = pl.program_id(1)
    nqk = GQA_Q_HEADS + GQA_KV_HEADS

    @pl.when(h < nqk)
    def _():
        x = p_ref[...]
        scale = jnp.where(h < GQA_Q_HEADS, ATT_SCALE * LOG2E, 1.0).astype(F32)
        y = x * lax.rsqrt(jnp.mean(x * x, axis=-1, keepdims=True) + RMS_EPS) * (gain_ref[...] * scale)
        lane = lax.broadcasted_iota(jnp.int32, y.shape, 1)
        half = ROPE_AXIS_DIM // 2
        partner = jnp.where((lane % ROPE_AXIS_DIM) < half,
                            pltpu.roll(y, HEAD_DIM - half, 1), pltpu.roll(y, half, 1))
        o_ref[...] = (y * cos_ref[...] + partner * sin_ref[...]).astype(BF16)

    @pl.when(h >= nqk)
    def _():
        o_ref[...] = p_ref[...].astype(BF16)


def gqa_prep(p, gains, cos_t, sin_t, tm=512):
    t = p.shape[0]
    nh = GQA_Q_HEADS + 2 * GQA_KV_HEADS
    col0 = 3 * NA_HEADS
    lat_tiles = T_LAT // tm
    per_seq = SEQ // tm

    def tab_idx(i, h):
        return (jnp.where(i < lat_tiles, i % per_seq, per_seq), 0)

    return pl.pallas_call(
        _gqa_prep_kernel,
        grid=(t // tm, nh),
        in_specs=[pl.BlockSpec((tm, HEAD_DIM), lambda i, h: (i, col0 + h)),
                  pl.BlockSpec((None, 1, HEAD_DIM),
                               lambda i, h: (jnp.where(h < GQA_Q_HEADS, 0, 1), 0, 0)),
                  pl.BlockSpec((tm, HEAD_DIM), tab_idx),
                  pl.BlockSpec((tm, HEAD_DIM), tab_idx)],
        out_specs=pl.BlockSpec((tm, HEAD_DIM), lambda i, h: (i, h)),
        out_shape=jax.ShapeDtypeStruct((t, nh * HEAD_DIM), BF16),
        compiler_params=_cp("arbitrary", "arbitrary"),
        name="gqa_prep",
    )(p, gains, cos_t, sin_t)


def rope_tables(tm=512):
    t = jnp.arange(SEQ)
    row = (t // GRID_W).astype(F32)
    col = (t % GRID_W).astype(F32)
    inv = ROPE_THETA ** (-jnp.arange(0, ROPE_AXIS_DIM, 2, dtype=F32) / ROPE_AXIS_DIM)
    ar = row[:, None] * inv[None]
    ac = col[:, None] * inv[None]
    cos_t = jnp.concatenate([jnp.cos(ar), jnp.cos(ar), jnp.cos(ac), jnp.cos(ac)], axis=-1)
    sin_t = jnp.concatenate([-jnp.sin(ar), jnp.sin(ar), -jnp.sin(ac), jnp.sin(ac)], axis=-1)
    cos_t = jnp.concatenate([cos_t, jnp.ones((tm, HEAD_DIM), F32)], axis=0)
    sin_t = jnp.concatenate([sin_t, jnp.zeros((tm, HEAD_DIM), F32)], axis=0)
    return cos_t, sin_t


LOG2E = 1.4426950408889634
ATT_KCHUNK = 512


def _softmax_chunk(q, k, v, bias, m, l, acc):
    s = lax.dot_general(q, k.astype(BF16), (((1,), (1,)), ((), ())), preferred_element_type=F32)
    if bias is not None:
        s = s + bias
    m_new = jnp.maximum(m, jnp.max(s, axis=-1, keepdims=True))
    alpha = jnp.exp2(m - m_new)
    p = jnp.exp2(s - m_new)
    l = alpha * l + jnp.sum(p, axis=-1, keepdims=True)
    acc = alpha * acc + jnp.dot(p.astype(BF16), v.astype(BF16), preferred_element_type=F32)
    return m_new, l, acc


def _attn_kernel(*refs, nseg, group, scale):
    q_ref = refs[0]
    k_refs = refs[1:1 + nseg]
    v_refs = refs[1 + nseg:1 + 2 * nseg]
    o_ref = refs[1 + 2 * nseg]
    tq = q_ref.shape[0]
    q = jnp.concatenate([q_ref[:, g * HEAD_DIM:(g + 1) * HEAD_DIM] for g in range(group)], axis=0)
    if scale != 1.0:
        q = q.astype(F32) * scale
    q = q.astype(BF16)
    m = jnp.full((q.shape[0], 1), -jnp.inf, F32)
    l = jnp.zeros((q.shape[0], 1), F32)
    acc = jnp.zeros((q.shape[0], HEAD_DIM), F32)
    for k_ref, v_ref in zip(k_refs, v_refs):
        nk = k_ref.shape[0]
        for c0 in range(0, nk, ATT_KCHUNK):
            cs = min(ATT_KCHUNK, nk - c0)
            m, l, acc = _softmax_chunk(q, k_ref[c0:c0 + cs, :], v_ref[c0:c0 + cs, :], None, m, l, acc)
    o = acc / l
    for g in range(group):
        o_ref[:, g * HEAD_DIM:(g + 1) * HEAD_DIM] = o[g * tq:(g + 1) * tq].astype(o_ref.dtype)


def attention(q_arr, q_idx, kv_arrs, k_idx, v_idx, kv_rows, grid, tq, group, scale, out_rows, out_cols, o_idx, name):
    nseg = len(kv_rows)
    in_specs = [pl.BlockSpec((tq, group * HEAD_DIM), q_idx)]
    in_specs += [pl.BlockSpec((kv_rows[i], HEAD_DIM), k_idx[i]) for i in range(nseg)]
    in_specs += [pl.BlockSpec((kv_rows[i], HEAD_DIM), v_idx[i]) for i in range(nseg)]
    return pl.pallas_call(
        functools.partial(_attn_kernel, nseg=nseg, group=group, scale=scale),
        grid=grid,
        in_specs=in_specs,
        out_specs=pl.BlockSpec((tq, group * HEAD_DIM), o_idx),
        out_shape=jax.ShapeDtypeStruct((out_rows, out_cols), BF16),
        compiler_params=_cp(*(["arbitrary"] * len(grid))),
        name=name,
    )(q_arr, *kv_arrs, *kv_arrs)


NA_QROWS = 8
NA_KROWS = 16
NA_TQ = NA_QROWS * GRID_W
NA_TK = NA_KROWS * GRID_W
GRID_H = SEQ // GRID_W


def _na_kernel(q_ref, k_ref, v_ref, kc_ref, vc_ref, bias_ref, o_ref):
    rb = pl.program_id(2)
    w0 = jnp.clip(NA_QROWS * rb - (NA_KROWS - NA_QROWS) // 2, 0, GRID_H - NA_KROWS) * GRID_W
    w0 = pl.multiple_of(w0, 4 * GRID_W)
    q = (q_ref[...] * (ATT_SCALE * LOG2E)).astype(BF16)
    m = jnp.full((NA_TQ, 1), -jnp.inf, F32)
    l = jnp.zeros((NA_TQ, 1), F32)
    acc = jnp.zeros((NA_TQ, HEAD_DIM), F32)
    m, l, acc = _softmax_chunk(q, kc_ref[...], vc_ref[...], None, m, l, acc)
    for c0 in range(0, NA_TK, ATT_KCHUNK):
        kw = k_ref[pl.ds(w0 + c0, ATT_KCHUNK), :]
        vw = v_ref[pl.ds(w0 + c0, ATT_KCHUNK), :]
        m, l, acc = _softmax_chunk(q, kw, vw, bias_ref[:, c0:c0 + ATT_KCHUNK], m, l, acc)
    o_ref[...] = (acc / l).astype(o_ref.dtype)


def na_bias_table(rpb):
    aw = jnp.arange(GRID_W)
    col_start = jnp.clip(aw - NA_COLS // 2, 0, GRID_W - NA_COLS)
    col_ok = (aw[None, :] >= col_start[:, None]) & (aw[None, :] < col_start[:, None] + NA_COLS)
    off_c = jnp.clip(aw[None, :] - aw[:, None] + (NA_COLS - 1), 0, 2 * NA_COLS - 2)
    nh, nr, nc = rpb.shape
    pick = (off_c.reshape(1, -1) == jnp.arange(nc)[:, None]).astype(F32)
    tiles = jnp.dot(rpb.astype(F32).reshape(nh * nr, nc), pick, precision=lax.Precision.HIGHEST)
    tiles = jnp.where(col_ok.reshape(1, -1), tiles * LOG2E, NEG_INF).reshape(nh, nr, GRID_W, GRID_W)
    blocked = jnp.full((nh, 1, GRID_W, GRID_W), NEG_INF, F32)
    tiles = jnp.concatenate([tiles, blocked], axis=1)
    nblk = GRID_H // NA_QROWS
    out = []
    for rb in (0, nblk // 2, nblk - 1):
        qr = NA_QROWS * rb + jnp.arange(NA_QROWS)
        w0 = min(max(NA_QROWS * rb - (NA_KROWS - NA_QROWS) // 2, 0), GRID_H - NA_KROWS)
        kr = w0 + jnp.arange(NA_KROWS)
        rs = jnp.clip(qr - NA_ROWS // 2, 0, GRID_H - NA_ROWS)
        row_ok = (kr[None, :] >= rs[:, None]) & (kr[None, :] < rs[:, None] + NA_ROWS)
        off_r = jnp.where(row_ok, kr[None, :] - qr[:, None] + (NA_ROWS - 1), nr)
        b = jnp.take(tiles, off_r.reshape(-1), axis=1)
        b = b.reshape(nh, NA_QROWS, NA_KROWS, GRID_W, GRID_W).transpose(0, 1, 3, 2, 4)
        out.append(b.reshape(nh, NA_TQ, NA_TK))
    return jnp.stack(out, axis=1)


def na_attention(p, bias):
    nblk = GRID_H // NA_QROWS
    ctx_blk0 = T_LAT // CTX_LEN

    def pat(rb):
        return jnp.where(rb == 0, 0, jnp.where(rb == nblk - 1, 2, 1))

    return pl.pallas_call(
        _na_kernel,
        grid=(NA_HEADS, BATCH, nblk),
        in_specs=[pl.BlockSpec((NA_TQ, HEAD_DIM), lambda h, b, r: (b * nblk + r, h)),
                  pl.BlockSpec((SEQ, HEAD_DIM), lambda h, b, r: (b, NA_HEADS + h)),
                  pl.BlockSpec((SEQ, HEAD_DIM), lambda h, b, r: (b, 2 * NA_HEADS + h)),
                  pl.BlockSpec((CTX_LEN, HEAD_DIM), lambda h, b, r: (ctx_blk0 + b, NA_HEADS + h)),
                  pl.BlockSpec((CTX_LEN, HEAD_DIM), lambda h, b, r: (ctx_blk0 + b, 2 * NA_HEADS + h)),
                  pl.BlockSpec((None, None, NA_TQ, NA_TK), lambda h, b, r: (h, pat(r), 0, 0))],
        out_specs=pl.BlockSpec((NA_TQ, HEAD_DIM), lambda h, b, r: (b * nblk + r, h)),
        out_shape=jax.ShapeDtypeStruct((T_LAT, NA_HEADS * HEAD_DIM), BF16),
        compiler_params=_cp("arbitrary", "arbitrary", "arbitrary"),
        name="na_attention",
    )(p, p, p, p, p, bias)


def s5_weights(lam_re, lam_im, log_dt, b_re, b_im, c_re, c_im):
    hi = lax.Precision.HIGHEST
    L = S5_CHUNK
    lr = jnp.minimum(lam_re.astype(F32), -1e-4)
    li = lam_im.astype(F32)
    dt = jnp.exp(log_dt.astype(F32))[..., None]
    mag = jnp.exp(lr * dt)
    ar = mag * jnp.cos(li * dt)
    ai = mag * jnp.sin(li * dt)
    den = lr * lr + li * li
    xr = ar - 1.0
    fr = (xr * lr + ai * li) / den
    fi = (ai * lr - xr * li) / den
    br = b_re.astype(F32)
    bi = b_im.astype(F32)
    bbr = fr[..., None] * br - fi[..., None] * bi
    bbi = fr[..., None] * bi + fi[..., None] * br
    k = jnp.arange(L + 1, dtype=F32)[:, None, None, None]
    pm = jnp.exp(lr * dt * k)
    pr = pm * jnp.cos(li * dt * k)
    pi = pm * jnp.sin(li * dt * k)
    e_r = pr[..., None] * bbr - pi[..., None] * bbi
    e_i = pr[..., None] * bbi + pi[..., None] * bbr
    cr = c_re.astype(F32)
    ci = c_im.astype(F32)
    kern = (jnp.einsum('dgop,kdgpi->kdgoi', cr, e_r[:L], precision=hi)
            - jnp.einsum('dgop,kdgpi->kdgoi', ci, e_i[:L], precision=hi))
    s_i = jnp.arange(L)
    lag = s_i[None, :] - s_i[:, None]
    toep = jnp.where((lag >= 0)[:, :, None, None, None, None], kern[jnp.clip(lag, 0, L - 1)], 0.0)
    ng = lr.shape[1]
    w_toep = toep.transpose(2, 3, 0, 5, 1, 4).reshape(2, ng, S5_CW, S5_CW)
    st_r = e_r[L - 1 - s_i].transpose(1, 2, 0, 4, 3).reshape(2, ng, S5_CW, S5_STATE)
    st_i = e_i[L - 1 - s_i].transpose(1, 2, 0, 4, 3).reshape(2, ng, S5_CW, S5_STATE)
    w_state = jnp.concatenate([st_r, st_i], axis=-1)
    w_state_sw = jnp.concatenate([st_i, st_r], axis=-1)
    qr = pr[1:, :, :, None, :]
    qi = pi[1:, :, :, None, :]
    d_r = cr[None] * qr - ci[None] * qi
    d_i = cr[None] * qi + ci[None] * qr
    wo_r = d_r.transpose(1, 2, 4, 0, 3).reshape(2, ng, S5_STATE, S5_CW)
    wo_i = (-d_i).transpose(1, 2, 4, 0, 3).reshape(2, ng, S5_STATE, S5_CW)
    w_out = jnp.concatenate([wo_r, wo_i], axis=2)
    flip_rows = lambda w: w.reshape(ng, L, S5_GROUP, w.shape[-1])[:, ::-1].reshape(w.shape)
    flip_cols = lambda w: w.reshape(ng, w.shape[1], L, S5_GROUP)[:, :, ::-1].reshape(w.shape)
    wst = jnp.concatenate([w_state[0], flip_rows(w_state[1]),
                           w_state_sw[0], flip_rows(w_state_sw[1])], axis=-1).astype(BF16)
    toep_r = flip_cols(flip_rows(w_toep[1]))
    wy = jnp.concatenate([w_toep[0] + toep_r, w_out[0], flip_cols(w_out[1])], axis=1).astype(BF16)
    a1 = jnp.concatenate([pr[L], pr[L]], axis=-1)
    a2 = jnp.concatenate([-pi[L], pi[L]], axis=-1)
    per_row = lambda a: jnp.repeat(a.transpose(1, 0, 2), BATCH, axis=1)
    return wst, wy, per_row(a1), per_row(a2)


S5_CTX_CHUNKS = CTX_LEN // S5_CHUNK
S5_PACK = HEAD_DIM // S5_GROUP
S5_PACKS = S5_GROUPS // S5_PACK
S5_BROWS = S5_NCHUNK


def s5_lane_permutation():
    tl, g, hh = jnp.meshgrid(jnp.arange(S5_CHUNK), jnp.arange(S5_PACK), jnp.arange(S5_GROUP), indexing="ij")
    dst = (g * S5_CW + tl * S5_GROUP + hh).reshape(-1)
    return (dst[:, None] == jnp.arange(S5_PACK * S5_CW)[None, :]).astype(BF16)


def _s5_pack_kernel(lat_ref, ctx_ref, p_ref, o_ref):
    cols = []
    for tl in range(S5_CHUNK):
        xc = ctx_ref[pl.ds(tl, S5_CTX_CHUNKS, stride=S5_CHUNK), :]
        xl = lat_ref[pl.ds(tl, SEQ // S5_CHUNK, stride=S5_CHUNK), :]
        cols.append(jnp.concatenate([xc, xl], axis=0))
    xcat = jnp.concatenate(cols, axis=1).astype(BF16)
    o_ref[...] = jnp.dot(xcat, p_ref[...], preferred_element_type=F32).astype(BF16)


def s5_pack(u, perm):
    ctx0 = T_LAT // CTX_LEN
    w = S5_PACK * S5_CW
    return pl.pallas_call(
        _s5_pack_kernel,
        grid=(S5_PACKS, BATCH),
        in_specs=[pl.BlockSpec((SEQ, HEAD_DIM), lambda k, b: (b, k)),
                  pl.BlockSpec((CTX_LEN, HEAD_DIM), lambda k, b: (ctx0 + b, k)),
                  pl.BlockSpec((w, w), lambda k, b: (0, 0))],
        out_specs=pl.BlockSpec((None, S5_BROWS, w), lambda k, b: (k, b, 0)),
        out_shape=jax.ShapeDtypeStruct((S5_PACKS, BATCH * S5_BROWS, w), BF16),
        compiler_params=_cp("arbitrary", "arbitrary"),
        name="s5_pack",
    )(u, u, perm)


def _s5_kernel(u_ref, wst_ref, wy_ref, a1_ref, a2_ref, y_ref, s_ref, ssw_ref, hf_ref, hr_ref):
    gb = wst_ref.shape[0]
    nc = S5_NCHUNK
    sw = 2 * S5_STATE
    for j in range(gb):
        ug = u_ref[:, j * S5_CW:(j + 1) * S5_CW]
        s4 = jnp.dot(ug, wst_ref[j], preferred_element_type=F32)
        for b in range(BATCH):
            rows = s4[b * nc:(b + 1) * nc]
            s_ref[:, j, b, :] = rows[:, 0:sw]
            s_ref[:, j, BATCH + b, :] = rows[:, sw:2 * sw]
            ssw_ref[:, j, b, :] = rows[:, 2 * sw:3 * sw]
            ssw_ref[:, j, BATCH + b, :] = rows[:, 3 * sw:4 * sw]
    a1 = a1_ref[...]
    a2 = a2_ref[...]
    fwd = lax.broadcasted_iota(jnp.int32, (gb, 2 * BATCH, sw), 1) < BATCH

    def step(i, carry):
        h, hs = carry
        ri = jnp.where(i < S5_CTX_CHUNKS, S5_CTX_CHUNKS - 1 - i, nc - 1 + S5_CTX_CHUNKS - i)
        hf_ref[i] = h
        hr_ref[ri] = h
        s = jnp.where(fwd, s_ref[i], s_ref[ri])
        ssw = jnp.where(fwd, ssw_ref[i], ssw_ref[ri])
        return a1 * h + a2 * hs + s, a1 * hs - a2 * h + ssw

    z = jnp.zeros((gb, 2 * BATCH, sw), F32)
    lax.fori_loop(0, nc, step, (z, z), unroll=4)
    for j in range(gb):
        ug = u_ref[:, j * S5_CW:(j + 1) * S5_CW]
        hf = jnp.concatenate([hf_ref[:, j, b, :] for b in range(BATCH)], axis=0)
        hr = jnp.concatenate([hr_ref[:, j, BATCH + b, :] for b in range(BATCH)], axis=0)
        lhs = jnp.concatenate([ug, hf.astype(BF16), hr.astype(BF16)], axis=1)
        y_ref[:, j * S5_CW:(j + 1) * S5_CW] = jnp.dot(lhs, wy_ref[j], preferred_element_type=F32)


def s5_core(u_packed, wst, wy, a1, a2):
    gb = S5_GB
    per_pack = S5_PACK // gb
    rows = u_packed.shape[1]
    sw = 2 * S5_STATE
    blk = lambda *shape: pl.BlockSpec((gb,) + shape, lambda i: (i,) + (0,) * len(shape))
    io = pl.BlockSpec((None, rows, gb * S5_CW), lambda i: (i // per_pack, 0, i % per_pack))
    state = pltpu.VMEM((S5_NCHUNK, gb, 2 * BATCH, sw), F32)
    return pl.pallas_call(
        _s5_kernel,
        grid=(S5_GROUPS // gb,),
        in_specs=[io, blk(S5_CW, 4 * sw), blk(S5_CW + 2 * sw, S5_CW), blk(2 * BATCH, sw), blk(2 * BATCH, sw)],
        out_specs=io,
        out_shape=jax.ShapeDtypeStruct(u_packed.shape, F32),
        scratch_shapes=[state, state, state, state],
        compiler_params=_cp("arbitrary"),
        name="s5_core",
    )(u_packed, wst, wy, a1, a2)


def _s5_unpack_kernel(y_ref, pt_ref, lat_ref, ctx_ref):
    y = y_ref[...]
    hi = y.astype(BF16)
    lo = (y - hi.astype(F32)).astype(BF16)
    yn = (jnp.dot(hi, pt_ref[...], preferred_element_type=F32)
          + jnp.dot(lo, pt_ref[...], preferred_element_type=F32))
    for tl in range(S5_CHUNK):
        piece = yn[:, tl * HEAD_DIM:(tl + 1) * HEAD_DIM]
        ctx_ref[pl.ds(tl, S5_CTX_CHUNKS, stride=S5_CHUNK), :] = piece[:S5_CTX_CHUNKS]
        lat_ref[pl.ds(tl, SEQ // S5_CHUNK, stride=S5_CHUNK), :] = piece[S5_CTX_CHUNKS:]


def s5_unpack(y_packed, perm_t):
    w = S5_PACK * S5_CW
    return pl.pallas_call(
        _s5_unpack_kernel,
        grid=(S5_PACKS, BATCH),
        in_specs=[pl.BlockSpec((None, S5_BROWS, w), lambda k, b: (k, b, 0)),
                  pl.BlockSpec((w, w), lambda k, b: (0, 0))],
        out_specs=[pl.BlockSpec((SEQ, HEAD_DIM), lambda k, b: (b, k)),
                   pl.BlockSpec((CTX_LEN, HEAD_DIM), lambda k, b: (b, k))],
        out_shape=[jax.ShapeDtypeStruct((T_LAT, D_MODEL), F32), jax.ShapeDtypeStruct((T_CTX, D_MODEL), F32)],
        compiler_params=_cp("arbitrary", "arbitrary"),
        name="s5_unpack",
    )(y_packed, perm_t)


def _glu_kernel(u_ref, y_ref, d_ref, wa_ref, wg_ref, x_ref, gate_ref, o_ref, h_ref):
    @pl.when(pl.program_id(1) == 0)
    def _():
        y = u_ref[...] * d_ref[...] + y_ref[...]
        h_ref[...] = jax.nn.gelu(y).astype(BF16)

    h = h_ref[...]
    za = jnp.dot(h, wa_ref[...].astype(BF16), preferred_element_type=F32)
    zg = jnp.dot(h, wg_ref[...].astype(BF16), preferred_element_type=F32)
    o_ref[...] = x_ref[...] + gate_ref[...] * (za * jax.nn.sigmoid(zg))


def glu_residual(u, y, d_all, w_all, wi, x, mod, layer, gate_col, tm=512, tn=512):
    t = u.shape[0]
    nj = D_MODEL // tn
    row = pl.BlockSpec((tm, D_MODEL), lambda i, j: (i, 0))
    return pl.pallas_call(
        _glu_kernel,
        grid=(t // tm, nj),
        in_specs=[row, row,
                  pl.BlockSpec((None, 1, D_MODEL), lambda i, j: (wi, 0, 0)),
                  pl.BlockSpec((None, D_MODEL, tn), lambda i, j: (wi, 0, j)),
                  pl.BlockSpec((None, D_MODEL, tn), lambda i, j: (wi, 0, nj + j)),
                  pl.BlockSpec((tm, tn), lambda i, j: (i, j)),
                  _gate_spec(layer, gate_col, tm, tn)],
        out_specs=pl.BlockSpec((tm, tn), lambda i, j: (i, j)),
        out_shape=jax.ShapeDtypeStruct((t, D_MODEL), F32),
        scratch_shapes=[pltpu.VMEM((tm, D_MODEL), BF16)],
        compiler_params=_cp("arbitrary", "arbitrary"),
        name="glu_residual",
    )(u, y, d_all.reshape(-1, 1, D_MODEL), w_all, w_all, x, mod)


ROUTE_LANES = 128


def _route_kernel(x_ref, g_ref, sh_ref, sc_ref, wr_ref, br_ref, h_ref, r_ref, cnt_ref, carry_ref):
    @pl.when(pl.program_id(0) == 0)
    def _():
        carry_ref[...] = jnp.zeros_like(carry_ref)

    h = _norm_mod(x_ref[...], g_ref[...], sh_ref[...], sc_ref[...])
    h_ref[...] = h
    lg = jnp.dot(h, wr_ref[...], precision=lax.Precision.HIGHEST, preferred_element_type=F32) + br_ref[...]
    lane = lax.broadcasted_iota(jnp.int32, lg.shape, 1)
    ninf = -jnp.inf
    coarse = lane < MOE_GROUPS
    lc = jnp.where(coarse, lg, ninf)
    mc = jnp.max(lc, axis=-1, keepdims=True)
    g_sel = jnp.min(jnp.where(lc == mc, lane, ROUTE_LANES), axis=-1, keepdims=True)
    p_sel = 1.0 / jnp.sum(jnp.where(coarse, jnp.exp(lc - mc), 0.0), axis=-1, keepdims=True)
    lo = MOE_GROUPS + MOE_EXPERTS_PER_GROUP * g_sel
    lf = jnp.where((lane >= lo) & (lane < lo + MOE_EXPERTS_PER_GROUP), lg, ninf)
    v0 = jnp.max(lf, axis=-1, keepdims=True)
    i0 = jnp.min(jnp.where(lf == v0, lane, ROUTE_LANES), axis=-1, keepdims=True)
    lf2 = jnp.where(lane == i0, ninf, lf)
    v1 = jnp.max(lf2, axis=-1, keepdims=True)
    i1 = jnp.min(jnp.where(lf2 == v1, lane, ROUTE_LANES), axis=-1, keepdims=True)
    e1 = jnp.exp(v1 - v0)
    w0 = p_sel / (1.0 + e1)
    w1 = w0 * e1
    tm = lg.shape[0]
    lower = (lax.broadcasted_iota(jnp.int32, (tm, tm), 1)
             < lax.broadcasted_iota(jnp.int32, (tm, tm), 0)).astype(BF16)
    carry = carry_ref[...]
    hot0 = (lane == i0).astype(F32)
    before0 = carry + jnp.dot(lower, hot0.astype(BF16), preferred_element_type=F32)
    rank0 = jnp.sum(hot0 * before0, axis=-1, keepdims=True)
    carry = carry + jnp.sum(hot0, axis=0, keepdims=True)
    hot1 = (lane == i1).astype(F32)
    before1 = carry + jnp.dot(lower, hot1.astype(BF16), preferred_element_type=F32)
    rank1 = jnp.sum(hot1 * before1, axis=-1, keepdims=True)
    carry = carry + jnp.sum(hot1, axis=0, keepdims=True)
    carry_ref[...] = carry
    cnt_ref[...] = carry
    cols = [(i0 - MOE_GROUPS).astype(F32), (i1 - MOE_GROUPS).astype(F32), w0, w1, rank0, rank1]
    r = jnp.zeros_like(lg)
    for c, val in enumerate(cols):
        r = jnp.where(lane == c, val, r)
    r_ref[...] = r


ROUTE_E0, ROUTE_E1, ROUTE_W0, ROUTE_W1, ROUTE_R0, ROUTE_R1 = range(6)


def moe_route(x, g_all, mod, layer, sh_col, sc_col, w_route, b_route, tm=512):
    t = x.shape[0]
    mspec = lambda col: pl.BlockSpec((None, 1, D_MODEL),
                                     lambda i: (layer * MOD_ROWS + _mod_row(i, tm), 0, col))
    return pl.pallas_call(
        _route_kernel,
        grid=(t // tm,),
        in_specs=[pl.BlockSpec((tm, D_MODEL), lambda i: (i, 0)),
                  pl.BlockSpec((None, 1, D_MODEL), lambda i: (layer, 0, 0)),
                  mspec(sh_col), mspec(sc_col),
                  pl.BlockSpec((D_MODEL, ROUTE_LANES), lambda i: (0, 0)),
                  pl.BlockSpec((1, ROUTE_LANES), lambda i: (0, 0))],
        out_specs=[pl.BlockSpec((tm, D_MODEL), lambda i: (i, 0)),
                   pl.BlockSpec((tm, ROUTE_LANES), lambda i: (i, 0)),
                   pl.BlockSpec((1, ROUTE_LANES), lambda i: (0, 0))],
        out_shape=[jax.ShapeDtypeStruct((t, D_MODEL), F32),
                   jax.ShapeDtypeStruct((t, ROUTE_LANES), F32),
                   jax.ShapeDtypeStruct((1, ROUTE_LANES), F32)],
        scratch_shapes=[pltpu.VMEM((1, ROUTE_LANES), F32)],
        compiler_params=_cp("arbitrary"),
        name="moe_route",
    )(x, g_all.reshape(DEPTH, 1, D_MODEL), mod, mod, w_route, b_route)


def moe_plan(route, counts):
    cnt = counts[0, MOE_GROUPS:MOE_GROUPS + MOE_EXPERTS].astype(jnp.int32)
    padded = ((cnt + MOE_TILE - 1) // MOE_TILE) * MOE_TILE
    pad_end = jnp.cumsum(padded)
    pad_off = pad_end - padded
    e = route[:, ROUTE_E0:ROUTE_E1 + 1].astype(jnp.int32)
    rank = route[:, ROUTE_R0:ROUTE_R1 + 1].astype(jnp.int32)
    hot = e[:, :, None] == jnp.arange(MOE_EXPERTS, dtype=jnp.int32)[None, None, :]
    dest = jnp.sum(jnp.where(hot, pad_off[None, None, :], 0), axis=-1) + rank
    n_used = (pad_end[-1] // MOE_TILE).astype(jnp.int32).reshape(1)
    tile_start = jnp.arange(MOE_NT, dtype=jnp.int32) * MOE_TILE
    tile_expert = jnp.sum((pad_end[None, :] <= tile_start[:, None]).astype(jnp.int32), axis=1)
    tile_expert = jnp.minimum(tile_expert, MOE_EXPERTS - 1).astype(jnp.int32)
    return dest, tile_expert, n_used


def _dispatch_kernel(dest_ref, h_hbm, zero_hbm, hs_hbm, sem):
    del zero_hbm
    i = pl.program_id(0)
    tm = dest_ref.shape[-1] // 2

    def row_copy(t, d):
        return pltpu.make_async_copy(h_hbm.at[pl.ds(t, 1), :], hs_hbm.at[pl.ds(d, 1), :], sem)

    def issue(r, c):
        t = i * tm + r
        row_copy(t, dest_ref[0, 0, 2 * r]).start()
        row_copy(t, dest_ref[0, 0, 2 * r + 1]).start()
        return c
    lax.fori_loop(0, tm, issue, 0, unroll=8)

    def drain(r, c):
        row_copy(0, 0).wait()
        row_copy(0, 0).wait()
        return c
    lax.fori_loop(0, tm, drain, 0, unroll=8)


def moe_dispatch(h, dest, tm=512):
    t = h.shape[0]
    zeros = jnp.zeros((MOE_NT * MOE_TILE, D_MODEL), h.dtype)
    return pl.pallas_call(
        _dispatch_kernel,
        grid=(t // tm,),
        in_specs=[pl.BlockSpec((1, 1, 2 * tm), lambda i: (i, 0, 0), memory_space=pltpu.SMEM),
                  pl.BlockSpec(memory_space=pl.ANY),
                  pl.BlockSpec(memory_space=pl.ANY)],
        out_specs=pl.BlockSpec(memory_space=pl.ANY),
        out_shape=jax.ShapeDtypeStruct(zeros.shape, h.dtype),
        scratch_shapes=[pltpu.SemaphoreType.DMA],
        input_output_aliases={2: 0},
        compiler_params=_cp("arbitrary"),
        name="moe_dispatch",
    )(dest.reshape(t // tm, 1, 2 * tm), h, zeros)


def _expert_kernel(te_ref, nu_ref, hs_ref, wg_ref, wu_ref, wd_ref, o_ref, wgb, wub, wdb):
    i = pl.program_id(0)

    @pl.when(i < nu_ref[0])
    def _():
        @pl.when((i == 0) | (te_ref[i] != te_ref[jnp.maximum(i - 1, 0)]))
        def _():
            wgb[...] = wg_ref[...].astype(BF16)
            wub[...] = wu_ref[...].astype(BF16)
            wdb[...] = wd_ref[...].astype(BF16)

        h = hs_ref[...].astype(BF16)
        g = jnp.dot(h, wgb[...], preferred_element_type=F32)
        u = jnp.dot(h, wub[...], preferred_element_type=F32)
        hid = (jax.nn.silu(g) * u).astype(BF16)
        o_ref[...] = jnp.dot(hid, wdb[...], preferred_element_type=F32)

    @pl.when(i >= nu_ref[0])
    def _():
        o_ref[...] = jnp.zeros_like(o_ref)


def moe_experts(hs, tile_expert, n_used, w_gate, w_up, w_down, layer):
    def wspec(shape):
        return pl.BlockSpec((None, None, None) + shape,
                            lambda i, te, nu: (layer, te[i] // MOE_EXPERTS_PER_GROUP,
                                               te[i] % MOE_EXPERTS_PER_GROUP, 0, 0))

    grid_spec = pltpu.PrefetchScalarGridSpec(
        num_scalar_prefetch=2,
        grid=(MOE_NT,),
        in_specs=[pl.BlockSpec((MOE_TILE, D_MODEL), lambda i, te, nu: (jnp.minimum(i, nu[0] - 1), 0)),
                  wspec((D_MODEL, MOE_FFN)), wspec((D_MODEL, MOE_FFN)), wspec((MOE_FFN, D_MODEL))],
        out_specs=pl.BlockSpec((MOE_TILE, D_MODEL), lambda i, te, nu: (i, 0)),
        scratch_shapes=[pltpu.VMEM((D_MODEL, MOE_FFN), BF16),
                        pltpu.VMEM((D_MODEL, MOE_FFN), BF16),
                        pltpu.VMEM((MOE_FFN, D_MODEL), BF16)])
    return pl.pallas_call(
        _expert_kernel,
        grid_spec=grid_spec,
        out_shape=jax.ShapeDtypeStruct((MOE_NT * MOE_TILE, D_MODEL), F32),
        compiler_params=_cp("arbitrary"),
        name="moe_experts",
    )(tile_expert, n_used, hs, w_gate, w_up, w_down)


def _combine_kernel(dest_ref, destn_ref, ys_hbm, r_ref, x_ref, gate_ref, o_ref, buf, sem):
    i = pl.program_id(0)
    n = pl.num_programs(0)
    tm = x_ref.shape[0]

    def row_copy(d, slot, k, r):
        return pltpu.make_async_copy(ys_hbm.at[pl.ds(d, 1), :], buf.at[slot, k, pl.ds(r, 1), :], sem.at[slot])

    def issue(dref, slot):
        def body(r, c):
            row_copy(dref[0, 0, 2 * r], slot, 0, r).start()
            row_copy(dref[0, 0, 2 * r + 1], slot, 1, r).start()
            return c
        lax.fori_loop(0, tm, body, 0, unroll=8)

    @pl.when(i == 0)
    def _():
        issue(dest_ref, 0)

    @pl.when(i + 1 < n)
    def _():
        issue(destn_ref, (i + 1) % 2)

    slot = i % 2

    def drain(r, c):
        row_copy(0, slot, 0, r).wait()
        row_copy(0, slot, 1, r).wait()
        return c
    lax.fori_loop(0, tm, drain, 0, unroll=8)
    r = r_ref[...]
    y = r[:, ROUTE_W0:ROUTE_W0 + 1] * buf[slot, 0] + r[:, ROUTE_W1:ROUTE_W1 + 1] * buf[slot, 1]
    o_ref[...] = x_ref[...] + gate_ref[...] * y


def moe_combine(ys, dest, route, x, mod, layer, gate_col, tm=256):
    t = x.shape[0]
    row = pl.BlockSpec((tm, D_MODEL), lambda i: (i, 0))
    nblk = t // tm
    smem = lambda off: pl.BlockSpec((1, 1, 2 * tm), lambda i: (jnp.minimum(i + off, nblk - 1), 0, 0),
                                    memory_space=pltpu.SMEM)
    dest3 = dest.reshape(nblk, 1, 2 * tm)
    return pl.pallas_call(
        _combine_kernel,
        grid=(nblk,),
        in_specs=[smem(0), smem(1),
                  pl.BlockSpec(memory_space=pl.ANY),
                  pl.BlockSpec((tm, ROUTE_LANES), lambda i: (i, 0)),
                  row,
                  pl.BlockSpec((None, 1, D_MODEL),
                               lambda i: (layer * MOD_ROWS + _mod_row(i, tm), 0, gate_col))],
        out_specs=row,
        out_shape=jax.ShapeDtypeStruct((t, D_MODEL), F32),
        scratch_shapes=[pltpu.VMEM((2, 2, tm, D_MODEL), F32), pltpu.SemaphoreType.DMA((2,))],
        compiler_params=_cp("arbitrary"),
        name="moe_combine",
    )(dest3, dest3, ys, route, x, mod)


def _final_norm_kernel(x_ref, g_ref, o_ref):
    x = x_ref[...]
    o_ref[...] = x * lax.rsqrt(jnp.mean(x * x, axis=-1, keepdims=True) + RMS_EPS) * g_ref[...]


def final_norm(x, g, rows, tm=512):
    row = pl.BlockSpec((tm, D_MODEL), lambda i: (i, 0))
    return pl.pallas_call(
        _final_norm_kernel,
        grid=(rows // tm,),
        in_specs=[row, pl.BlockSpec((1, D_MODEL), lambda i: (0, 0))],
        out_specs=row,
        out_shape=jax.ShapeDtypeStruct((rows, D_MODEL), F32),
        compiler_params=_cp("arbitrary"),
        name="final_norm",
    )(x, g.reshape(1, D_MODEL))


def attention_layer(xs, mod, layer, norm1_g, w_in, w_out, rpb, q_gain, k_gain, cos_t, sin_t):
    li = layer // 2
    p = norm_mod_matmul(xs, norm1_g, mod, w_in, layer, li, 0, 1)
    gains = jnp.stack([q_gain[li], k_gain[li]]).reshape(2, 1, HEAD_DIM)
    qkv = gqa_prep(p, gains, cos_t, sin_t)
    kcol, vcol = GQA_Q_HEADS, GQA_Q_HEADS + GQA_KV_HEADS
    grp = GQA_Q_HEADS // GQA_KV_HEADS
    ctx0 = T_LAT // CTX_LEN

    oa = na_attention(p, na_bias_table(rpb[li]))
    tq = 128
    nq = SEQ // tq
    ob = attention(
        qkv, lambda b, k, q: (b * nq + q, k),
        [qkv, qkv],
        [lambda b, k, q: (b, kcol + k), lambda b, k, q: (ctx0 + b, kcol + k)],
        [lambda b, k, q: (b, vcol + k), lambda b, k, q: (ctx0 + b, vcol + k)],
        [SEQ, CTX_LEN], (BATCH, GQA_KV_HEADS, nq), tq, grp, 1.0,
        T_LAT, GQA_Q_HEADS * HEAD_DIM, lambda b, k, q: (b * nq + q, k), "gqa_latent")
    oac = attention(
        p, lambda b, h: (ctx0 + b, h),
        [p],
        [lambda b, h: (ctx0 + b, NA_HEADS + h)],
        [lambda b, h: (ctx0 + b, 2 * NA_HEADS + h)],
        [CTX_LEN], (BATCH, NA_HEADS), CTX_LEN, 1, ATT_SCALE * LOG2E,
        T_CTX, NA_HEADS * HEAD_DIM, lambda b, h: (b, h), "na_context")
    obc = attention(
        qkv, lambda b, k: (ctx0 + b, k),
        [qkv],
        [lambda b, k: (ctx0 + b, kcol + k)],
        [lambda b, k: (ctx0 + b, vcol + k)],
        [CTX_LEN], (BATCH, GQA_KV_HEADS), CTX_LEN, grp, 1.0,
        T_CTX, GQA_Q_HEADS * HEAD_DIM, lambda b, k: (b, k), "gqa_context")
    oa_all = jnp.concatenate([oa, oac], axis=0)
    ob_all = jnp.concatenate([ob, obc], axis=0)
    return proj_residual(oa_all, ob_all, w_out, li, xs, mod, layer, 2)


def s5_layer(xs, mod, layer, norm1_g, w_in, lam_re, lam_im, log_dt, b_re, b_im, c_re, c_im, d_skip, w_glu):
    li = layer // 2
    u = norm_mod_matmul(xs, norm1_g, mod, w_in, layer, li, 0, 1)
    wst, wy, a1, a2 = s5_weights(lam_re[li], lam_im[li], log_dt[li], b_re[li], b_im[li], c_re[li], c_im[li])
    perm = s5_lane_permutation()
    y_lat, y_ctx = s5_unpack(s5_core(s5_pack(u, perm), wst, wy, a1, a2), perm.T)
    y = jnp.concatenate([y_lat, y_ctx], axis=0)
    return glu_residual(u, y, d_skip, w_glu, li, xs, mod, layer, 2)


def moe_layer(xs, mod, layer, norm2_g, w_coarse, b_coarse, w_fine, b_fine, w_gate, w_up, w_down):
    wf = w_fine[layer].transpose(1, 0, 2).reshape(D_MODEL, MOE_EXPERTS)
    w_route = jnp.concatenate([w_coarse[layer], wf], axis=1).astype(F32)
    w_route = jnp.pad(w_route, ((0, 0), (0, ROUTE_LANES - w_route.shape[1])))
    b_route = jnp.concatenate([b_coarse[layer], b_fine[layer].reshape(-1)]).astype(F32)
    b_route = jnp.pad(b_route, (0, ROUTE_LANES - b_route.shape[0])).reshape(1, ROUTE_LANES)
    h, route, counts = moe_route(xs, norm2_g, mod, layer, 3, 4, w_route, b_route)
    dest, tile_expert, n_used = moe_plan(route, counts)
    hs = moe_dispatch(h, dest)
    ys = moe_experts(hs, tile_expert, n_used, w_gate, w_up, w_down, layer)
    return moe_combine(ys, dest, route, xs, mod, layer, 5)


def kernel(x, c, ctx, c_ctx, ada_w, ada_b, norm1_g, norm2_g, final_g, attn_w_in, attn_w_out, na_rpb, q_gain, k_gain, s5_w_in, s5_lam_re, s5_lam_im, s5_log_dt, s5_b_re, s5_b_im, s5_c_re, s5_c_im, s5_d, s5_w_glu, moe_w_coarse, moe_b_coarse, moe_w_fine, moe_b_fine, moe_w_gate, moe_w_up, moe_w_down):
    xs = jnp.concatenate([x.reshape(T_LAT, D_MODEL), ctx.reshape(T_CTX, D_MODEL)], axis=0)
    c8 = jnp.concatenate([c, c_ctx[None, :], jnp.zeros((MOD_ROWS - BATCH - 1, D_MODEL), F32)], axis=0)
    mod = ada_mod(c8, ada_w, ada_b).reshape(DEPTH * MOD_ROWS, 1, 6 * D_MODEL)
    cos_t, sin_t = rope_tables()
    for layer in range(DEPTH):
        if layer % 2 == 0:
            xs = attention_layer(xs, mod, layer, norm1_g, attn_w_in, attn_w_out, na_rpb, q_gain, k_gain,
                                 cos_t, sin_t)
        else:
            xs = s5_layer(xs, mod, layer, norm1_g, s5_w_in, s5_lam_re, s5_lam_im, s5_log_dt,
                          s5_b_re, s5_b_im, s5_c_re, s5_c_im, s5_d, s5_w_glu)
        xs = moe_layer(xs, mod, layer, norm2_g, moe_w_coarse, moe_b_coarse, moe_w_fine, moe_b_fine,
                       moe_w_gate, moe_w_up, moe_w_down)
    return final_norm(xs, final_g, T_LAT).reshape(BATCH, SEQ, D_MODEL)
```

```python
import functools
import math

import jax
import jax.numpy as jnp
from jax import lax
from jax.experimental import pallas as pl
from jax.experimental.pallas import tpu as pltpu

F32 = jnp.float32
BF16 = jnp.bfloat16

D_MODEL = 2048
BATCH = 4
SEQ = 4096
DEPTH = 4
GRID_W = 64
CTX_LEN = 256
HEAD_DIM = 128
NA_HEADS = 8
NA_ROWS = 8
NA_COLS = 16
GQA_Q_HEADS = 8
GQA_KV_HEADS = 2
ROPE_THETA = 10000.0
ROPE_AXIS_DIM = HEAD_DIM // 2
ATTN_IN_COLS = (3 * NA_HEADS + GQA_Q_HEADS + 2 * GQA_KV_HEADS) * HEAD_DIM
S5_GROUP = 16
S5_GROUPS = D_MODEL // S5_GROUP
S5_STATE = 64
MOE_GROUPS = 4
MOE_EXPERTS_PER_GROUP = 8
MOE_EXPERTS = MOE_GROUPS * MOE_EXPERTS_PER_GROUP
MOE_FFN = D_MODEL // 4
RMS_EPS = 1e-6
NEG_INF = -1e30

T_LAT = BATCH * SEQ
T_CTX = BATCH * CTX_LEN
T_ALL = T_LAT + T_CTX
MOD_ROWS = 8
ATT_SCALE = HEAD_DIM ** -0.5

S5_CHUNK = 16
S5_SEQ = CTX_LEN + SEQ
S5_NCHUNK = S5_SEQ // S5_CHUNK
S5_CW = S5_CHUNK * S5_GROUP
S5_GB = 4

MOE_TILE = 512
MOE_SLOTS = T_ALL * 2
MOE_NT = MOE_SLOTS // MOE_TILE + MOE_EXPERTS

VMEM_LIMIT = 56 * 1024 * 1024


def _cp(*sem):
    return pltpu.CompilerParams(dimension_semantics=sem, vmem_limit_bytes=VMEM_LIMIT)


def _mod_row(i, tm):
    return jnp.minimum((i * tm) // SEQ, BATCH)


def _ada_kernel(c_ref, w_ref, b_ref, o_ref):
    s = jax.nn.silu(c_ref[...])
    o_ref[...] = jnp.dot(s, w_ref[...], precision=lax.Precision.HIGHEST,
                         preferred_element_type=F32) + b_ref[...]


def ada_mod(c8, ada_w, ada_b):
    tn = 1024
    n = ada_w.shape[-1]
    return pl.pallas_call(
        _ada_kernel,
        grid=(DEPTH, n // tn),
        in_specs=[pl.BlockSpec((MOD_ROWS, D_MODEL), lambda l, j: (0, 0)),
                  pl.BlockSpec((None, D_MODEL, tn), lambda l, j: (l, 0, j)),
                  pl.BlockSpec((None, 1, tn), lambda l, j: (l, 0, j))],
        out_specs=pl.BlockSpec((None, MOD_ROWS, tn), lambda l, j: (l, 0, j)),
        out_shape=jax.ShapeDtypeStruct((DEPTH, MOD_ROWS, n), F32),
        compiler_params=_cp("arbitrary", "arbitrary"),
        name="ada_mod",
    )(c8, ada_w, ada_b.reshape(DEPTH, 1, n))


def _norm_mod(x, g, sh, sc):
    y = x * lax.rsqrt(jnp.mean(x * x, axis=-1, keepdims=True) + RMS_EPS) * g
    return y * (1.0 + sc) + sh


def _mod_spec(layer, col, tm):
    return pl.BlockSpec((None, 1, D_MODEL),
                        lambda i, j: (layer * MOD_ROWS + _mod_row(i, tm), 0, col))


def _gate_spec(layer, col, tm, tn):
    return pl.BlockSpec((None, 1, tn),
                        lambda i, j: (layer * MOD_ROWS + _mod_row(i, tm), 0, col * (D_MODEL // tn) + j))


def _nm_mm_kernel(x_ref, g_ref, sh_ref, sc_ref, w_ref, o_ref, h_ref):
    @pl.when(pl.program_id(1) == 0)
    def _():
        h_ref[...] = _norm_mod(x_ref[...], g_ref[...], sh_ref[...], sc_ref[...]).astype(BF16)

    o_ref[...] = jnp.dot(h_ref[...], w_ref[...].astype(BF16),
                         preferred_element_type=F32).astype(o_ref.dtype)


def norm_mod_matmul(x, g_all, mod, w_all, layer, wi, sh_col, sc_col, tm=1024, tn=512):
    t = x.shape[0]
    n = w_all.shape[-1]
    return pl.pallas_call(
        _nm_mm_kernel,
        grid=(t // tm, n // tn),
        in_specs=[pl.BlockSpec((tm, D_MODEL), lambda i, j: (i, 0)),
                  pl.BlockSpec((None, 1, D_MODEL), lambda i, j: (layer, 0, 0)),
                  _mod_spec(layer, sh_col, tm),
                  _mod_spec(layer, sc_col, tm),
                  pl.BlockSpec((None, D_MODEL, tn), lambda i, j: (wi, 0, j))],
        out_specs=pl.BlockSpec((tm, tn), lambda i, j: (i, j)),
        out_shape=jax.ShapeDtypeStruct((t, n), F32),
        scratch_shapes=[pltpu.VMEM((tm, D_MODEL), BF16)],
        compiler_params=_cp("arbitrary", "arbitrary"),
        name="norm_mod_matmul",
    )(x, g_all.reshape(DEPTH, 1, D_MODEL), mod, mod, w_all)


def _proj_res_kernel(a_ref, b_ref, wa_ref, wb_ref, x_ref, gate_ref, o_ref):
    y = jnp.dot(a_ref[...], wa_ref[...].astype(BF16), preferred_element_type=F32)
    y = y + jnp.dot(b_ref[...], wb_ref[...].astype(BF16), preferred_element_type=F32)
    o_ref[...] = x_ref[...] + gate_ref[...] * y


def proj_residual(oa, ob, w_all, wi, x, mod, layer, gate_col, tm=1024, tn=512):
    t, ka = oa.shape
    kb = ob.shape[1]
    assert ka == kb
    return pl.pallas_call(
        _proj_res_kernel,
        grid=(t // tm, D_MODEL // tn),
        in_specs=[pl.BlockSpec((tm, ka), lambda i, j: (i, 0)),
                  pl.BlockSpec((tm, kb), lambda i, j: (i, 0)),
                  pl.BlockSpec((None, ka, tn), lambda i, j: (wi, 0, j)),
                  pl.BlockSpec((None, kb, tn), lambda i, j: (wi, 1, j)),
                  pl.BlockSpec((tm, tn), lambda i, j: (i, j)),
                  _gate_spec(layer, gate_col, tm, tn)],
        out_specs=pl.BlockSpec((tm, tn), lambda i, j: (i, j)),
        out_shape=jax.ShapeDtypeStruct((t, D_MODEL), F32),
        compiler_params=_cp("arbitrary", "arbitrary"),
        name="proj_residual",
    )(oa, ob, w_all, w_all, x, mod)


def _gqa_prep_kernel(p_ref, gain_ref, cos_ref, sin_ref, o_ref):
    h = pl.program_id(1)
    nqk = GQA_Q_HEADS + GQA_KV_HEADS

    @pl.when(h < nqk)
    def _():
        x = p_ref[...]
        scale = jnp.where(h < GQA_Q_HEADS, ATT_SCALE * LOG2E, 1.0).astype(F32)
        y = x * lax.rsqrt(jnp.mean(x * x, axis=-1, keepdims=True) + RMS_EPS) * (gain_ref[...] * scale)
        lane = lax.broadcasted_iota(jnp.int32, y.shape, 1)
        half = ROPE_AXIS_DIM // 2
        partner = jnp.where((lane % ROPE_AXIS_DIM) < half,
                            pltpu.roll(y, HEAD_DIM - half, 1), pltpu.roll(y, half, 1))
        o_ref[...] = (y * cos_ref[...] + partner * sin_ref[...]).astype(BF16)

    @pl.when(h >= nqk)
    def _():
        o_ref[...] = p_ref[...].astype(BF16)


def gqa_prep(p, gains, cos_t, sin_t, tm=512):
    t = p.shape[0]
    nh = GQA_Q_HEADS + 2 * GQA_KV_HEADS
    col0 = 3 * NA_HEADS
    lat_tiles = T_LAT // tm
    per_seq = SEQ // tm

    def tab_idx(i, h):
        return (jnp.where(i < lat_tiles, i % per_seq, per_seq), 0)

    return pl.pallas_call(
        _gqa_prep_kernel,
        grid=(t // tm, nh),
        in_specs=[pl.BlockSpec((tm, HEAD_DIM), lambda i, h: (i, col0 + h)),
                  pl.BlockSpec((None, 1, HEAD_DIM),
                               lambda i, h: (jnp.where(h < GQA_Q_HEADS, 0, 1), 0, 0)),
                  pl.BlockSpec((tm, HEAD_DIM), tab_idx),
                  pl.BlockSpec((tm, HEAD_DIM), tab_idx)],
        out_specs=pl.BlockSpec((tm, HEAD_DIM), lambda i, h: (i, h)),
        out_shape=jax.ShapeDtypeStruct((t, nh * HEAD_DIM), BF16),
        compiler_params=_cp("arbitrary", "arbitrary"),
        name="gqa_prep",
    )(p, gains, cos_t, sin_t)


def rope_tables(tm=512):
    t = jnp.arange(SEQ)
    row = (t // GRID_W).astype(F32)
    col = (t % GRID_W).astype(F32)
    inv = ROPE_THETA ** (-jnp.arange(0, ROPE_AXIS_DIM, 2, dtype=F32) / ROPE_AXIS_DIM)
    ar = row[:, None] * inv[None]
    ac = col[:, None] * inv[None]
    cos_t = jnp.concatenate([jnp.cos(ar), jnp.cos(ar), jnp.cos(ac), jnp.cos(ac)], axis=-1)
    sin_t = jnp.concatenate([-jnp.sin(ar), jnp.sin(ar), -jnp.sin(ac), jnp.sin(ac)], axis=-1)
    cos_t = jnp.concatenate([cos_t, jnp.ones((tm, HEAD_DIM), F32)], axis=0)
    sin_t = jnp.concatenate([sin_t, jnp.zeros((tm, HEAD_DIM), F32)], axis=0)
    return cos_t, sin_t


LOG2E = 1.4426950408889634
ATT_KCHUNK = 512


def _softmax_chunk(q, k, v, bias, m, l, acc):
    s = lax.dot_general(q, k.astype(BF16), (((1,), (1,)), ((), ())), preferred_element_type=F32)
    if bias is not None:
        s = s + bias
    m_new = jnp.maximum(m, jnp.max(s, axis=-1, keepdims=True))
    alpha = jnp.exp2(m - m_new)
    p = jnp.exp2(s - m_new)
    l = alpha * l + jnp.sum(p, axis=-1, keepdims=True)
    acc = alpha * acc + jnp.dot(p.astype(BF16), v.astype(BF16), preferred_element_type=F32)
    return m_new, l, acc


def _attn_kernel(*refs, nseg, group, scale):
    q_ref = refs[0]
    k_refs = refs[1:1 + nseg]
    v_refs = refs[1 + nseg:1 + 2 * nseg]
    o_ref = refs[1 + 2 * nseg]
    tq = q_ref.shape[0]
    q = jnp.concatenate([q_ref[:, g * HEAD_DIM:(g + 1) * HEAD_DIM] for g in range(group)], axis=0)
    if scale != 1.0:
        q = q.astype(F32) * scale
    q = q.astype(BF16)
    m = jnp.full((q.shape[0], 1), -jnp.inf, F32)
    l = jnp.zeros((q.shape[0], 1), F32)
    acc = jnp.zeros((q.shape[0], HEAD_DIM), F32)
    for k_ref, v_ref in zip(k_refs, v_refs):
        nk = k_ref.shape[0]
        for c0 in range(0, nk, ATT_KCHUNK):
            cs = min(ATT_KCHUNK, nk - c0)
            m, l, acc = _softmax_chunk(q, k_ref[c0:c0 + cs, :], v_ref[c0:c0 + cs, :], None, m, l, acc)
    o = acc / l
    for g in range(group):
        o_ref[:, g * HEAD_DIM:(g + 1) * HEAD_DIM] = o[g * tq:(g + 1) * tq].astype(o_ref.dtype)


def attention(q_arr, q_idx, kv_arrs, k_idx, v_idx, kv_rows, grid, tq, group, scale, out_rows, out_cols, o_idx, name):
    nseg = len(kv_rows)
    in_specs = [pl.BlockSpec((tq, group * HEAD_DIM), q_idx)]
    in_specs += [pl.BlockSpec((kv_rows[i], HEAD_DIM), k_idx[i]) for i in range(nseg)]
    in_specs += [pl.BlockSpec((kv_rows[i], HEAD_DIM), v_idx[i]) for i in range(nseg)]
    return pl.pallas_call(
        functools.partial(_attn_kernel, nseg=nseg, group=group, scale=scale),
        grid=grid,
        in_specs=in_specs,
        out_specs=pl.BlockSpec((tq, group * HEAD_DIM), o_idx),
        out_shape=jax.ShapeDtypeStruct((out_rows, out_cols), BF16),
        compiler_params=_cp(*(["arbitrary"] * len(grid))),
        name=name,
    )(q_arr, *kv_arrs, *kv_arrs)


NA_QROWS = 8
NA_KROWS = 16
NA_TQ = NA_QROWS * GRID_W
NA_TK = NA_KROWS * GRID_W
GRID_H = SEQ // GRID_W


def _na_kernel(q_ref, k_ref, v_ref, kc_ref, vc_ref, bias_ref, o_ref):
    rb = pl.program_id(2)
    w0 = jnp.clip(NA_QROWS * rb - (NA_KROWS - NA_QROWS) // 2, 0, GRID_H - NA_KROWS) * GRID_W
    w0 = pl.multiple_of(w0, 4 * GRID_W)
    q = (q_ref[...] * (ATT_SCALE * LOG2E)).astype(BF16)
    m = jnp.full((NA_TQ, 1), -jnp.inf, F32)
    l = jnp.zeros((NA_TQ, 1), F32)
    acc = jnp.zeros((NA_TQ, HEAD_DIM), F32)
    m, l, acc = _softmax_chunk(q, kc_ref[...], vc_ref[...], None, m, l, acc)
    for c0 in range(0, NA_TK, ATT_KCHUNK):
        kw = k_ref[pl.ds(w0 + c0, ATT_KCHUNK), :]
        vw = v_ref[pl.ds(w0 + c0, ATT_KCHUNK), :]
        m, l, acc = _softmax_chunk(q, kw, vw, bias_ref[:, c0:c0 + ATT_KCHUNK], m, l, acc)
    o_ref[...] = (acc / l).astype(o_ref.dtype)


def na_bias_table(rpb):
    aw = jnp.arange(GRID_W)
    col_start = jnp.clip(aw - NA_COLS // 2, 0, GRID_W - NA_COLS)
    col_ok = (aw[None, :] >= col_start[:, None]) & (aw[None, :] < col_start[:, None] + NA_COLS)
    off_c = jnp.clip(aw[None, :] - aw[:, None] + (NA_COLS - 1), 0, 2 * NA_COLS - 2)
    nh, nr, nc = rpb.shape
    pick = (off_c.reshape(1, -1) == jnp.arange(nc)[:, None]).astype(F32)
    tiles = jnp.dot(rpb.astype(F32).reshape(nh * nr, nc), pick, precision=lax.Precision.HIGHEST)
    tiles = jnp.where(col_ok.reshape(1, -1), tiles * LOG2E, NEG_INF).reshape(nh, nr, GRID_W, GRID_W)
    blocked = jnp.full((nh, 1, GRID_W, GRID_W), NEG_INF, F32)
    tiles = jnp.concatenate([tiles, blocked], axis=1)
    nblk = GRID_H // NA_QROWS
    out = []
    for rb in (0, nblk // 2, nblk - 1):
        w0 = min(max(NA_QROWS * rb - (NA_KROWS - NA_QROWS) // 2, 0), GRID_H - NA_KROWS)
        picks = []
        for qr in range(NA_QROWS * rb, NA_QROWS * (rb + 1)):
            rs = min(max(qr - NA_ROWS // 2, 0), GRID_H - NA_ROWS)
            for kr in range(w0, w0 + NA_KROWS):
                picks.append(kr - qr + NA_ROWS - 1 if rs <= kr < rs + NA_ROWS else nr)
        b = jnp.stack([tiles[:, k] for k in picks], axis=1)
        b = b.reshape(nh, NA_QROWS, NA_KROWS, GRID_W, GRID_W).transpose(0, 1, 3, 2, 4)
        out.append(b.reshape(nh, NA_TQ, NA_TK))
    return jnp.stack(out, axis=1)


def na_attention(p, bias):
    nblk = GRID_H // NA_QROWS
    ctx_blk0 = T_LAT // CTX_LEN

    def pat(rb):
        return jnp.where(rb == 0, 0, jnp.where(rb == nblk - 1, 2, 1))

    return pl.pallas_call(
        _na_kernel,
        grid=(NA_HEADS, BATCH, nblk),
        in_specs=[pl.BlockSpec((NA_TQ, HEAD_DIM), lambda h, b, r: (b * nblk + r, h)),
                  pl.BlockSpec((SEQ, HEAD_DIM), lambda h, b, r: (b, NA_HEADS + h)),
                  pl.BlockSpec((SEQ, HEAD_DIM), lambda h, b, r: (b, 2 * NA_HEADS + h)),
                  pl.BlockSpec((CTX_LEN, HEAD_DIM), lambda h, b, r: (ctx_blk0 + b, NA_HEADS + h)),
                  pl.BlockSpec((CTX_LEN, HEAD_DIM), lambda h, b, r: (ctx_blk0 + b, 2 * NA_HEADS + h)),
                  pl.BlockSpec((None, None, NA_TQ, NA_TK), lambda h, b, r: (h, pat(r), 0, 0))],
        out_specs=pl.BlockSpec((NA_TQ, HEAD_DIM), lambda h, b, r: (b * nblk + r, h)),
        out_shape=jax.ShapeDtypeStruct((T_LAT, NA_HEADS * HEAD_DIM), BF16),
        compiler_params=_cp("arbitrary", "arbitrary", "arbitrary"),
        name="na_attention",
    )(p, p, p, p, p, bias)


def s5_weights(lam_re, lam_im, log_dt, b_re, b_im, c_re, c_im):
    hi = lax.Precision.HIGHEST
    L = S5_CHUNK
    lr = jnp.minimum(lam_re.astype(F32), -1e-4)
    li = lam_im.astype(F32)
    dt = jnp.exp(log_dt.astype(F32))[..., None]
    mag = jnp.exp(lr * dt)
    ar = mag * jnp.cos(li * dt)
    ai = mag * jnp.sin(li * dt)
    den = lr * lr + li * li
    xr = ar - 1.0
    fr = (xr * lr + ai * li) / den
    fi = (ai * lr - xr * li) / den
    br = b_re.astype(F32)
    bi = b_im.astype(F32)
    bbr = fr[..., None] * br - fi[..., None] * bi
    bbi = fr[..., None] * bi + fi[..., None] * br
    k = jnp.arange(L + 1, dtype=F32)[:, None, None, None]
    pm = jnp.exp(lr * dt * k)
    pr = pm * jnp.cos(li * dt * k)
    pi = pm * jnp.sin(li * dt * k)
    e_r = pr[..., None] * bbr - pi[..., None] * bbi
    e_i = pr[..., None] * bbi + pi[..., None] * bbr
    cr = c_re.astype(F32)
    ci = c_im.astype(F32)
    kern = (jnp.einsum('dgop,kdgpi->kdgoi', cr, e_r[:L], precision=hi)
            - jnp.einsum('dgop,kdgpi->kdgoi', ci, e_i[:L], precision=hi))
    s_i = jnp.arange(L)
    lag = s_i[None, :] - s_i[:, None]
    toep = jnp.where((lag >= 0)[:, :, None, None, None, None], kern[jnp.clip(lag, 0, L - 1)], 0.0)
    ng = lr.shape[1]
    w_toep = toep.transpose(2, 3, 0, 5, 1, 4).reshape(2, ng, S5_CW, S5_CW)
    st_r = e_r[L - 1 - s_i].transpose(1, 2, 0, 4, 3).reshape(2, ng, S5_CW, S5_STATE)
    st_i = e_i[L - 1 - s_i].transpose(1, 2, 0, 4, 3).reshape(2, ng, S5_CW, S5_STATE)
    w_state = jnp.concatenate([st_r, st_i], axis=-1)
    w_state_sw = jnp.concatenate([st_i, st_r], axis=-1)
    qr = pr[1:, :, :, None, :]
    qi = pi[1:, :, :, None, :]
    d_r = cr[None] * qr - ci[None] * qi
    d_i = cr[None] * qi + ci[None] * qr
    wo_r = d_r.transpose(1, 2, 4, 0, 3).reshape(2, ng, S5_STATE, S5_CW)
    wo_i = (-d_i).transpose(1, 2, 4, 0, 3).reshape(2, ng, S5_STATE, S5_CW)
    w_out = jnp.concatenate([wo_r, wo_i], axis=2)
    flip_rows = lambda w: w.reshape(ng, L, S5_GROUP, w.shape[-1])[:, ::-1].reshape(w.shape)
    flip_cols = lambda w: w.reshape(ng, w.shape[1], L, S5_GROUP)[:, :, ::-1].reshape(w.shape)
    wst = jnp.concatenate([w_state[0], flip_rows(w_state[1]),
                           w_state_sw[0], flip_rows(w_state_sw[1])], axis=-1).astype(BF16)
    toep_r = flip_cols(flip_rows(w_toep[1]))
    wy = jnp.concatenate([w_toep[0] + toep_r, w_out[0], flip_cols(w_out[1])], axis=1).astype(BF16)
    a1 = jnp.concatenate([pr[L], pr[L]], axis=-1)
    a2 = jnp.concatenate([-pi[L], pi[L]], axis=-1)
    per_row = lambda a: jnp.repeat(a.transpose(1, 0, 2), BATCH, axis=1)
    return wst, wy, per_row(a1), per_row(a2)


S5_CTX_CHUNKS = CTX_LEN // S5_CHUNK
S5_PACK = HEAD_DIM // S5_GROUP
S5_PACKS = S5_GROUPS // S5_PACK
S5_BROWS = S5_NCHUNK


def s5_lane_permutation():
    tl, g, hh = jnp.meshgrid(jnp.arange(S5_CHUNK), jnp.arange(S5_PACK), jnp.arange(S5_GROUP), indexing="ij")
    dst = (g * S5_CW + tl * S5_GROUP + hh).reshape(-1)
    return (dst[:, None] == jnp.arange(S5_PACK * S5_CW)[None, :]).astype(BF16)


def _s5_pack_kernel(lat_ref, ctx_ref, p_ref, o_ref):
    cols = []
    for tl in range(S5_CHUNK):
        xc = ctx_ref[pl.ds(tl, S5_CTX_CHUNKS, stride=S5_CHUNK), :]
        xl = lat_ref[pl.ds(tl, SEQ // S5_CHUNK, stride=S5_CHUNK), :]
        cols.append(jnp.concatenate([xc, xl], axis=0))
    xcat = jnp.concatenate(cols, axis=1).astype(BF16)
    o_ref[...] = jnp.dot(xcat, p_ref[...], preferred_element_type=F32).astype(BF16)


def s5_pack(u, perm):
    ctx0 = T_LAT // CTX_LEN
    w = S5_PACK * S5_CW
    return pl.pallas_call(
        _s5_pack_kernel,
        grid=(S5_PACKS, BATCH),
        in_specs=[pl.BlockSpec((SEQ, HEAD_DIM), lambda k, b: (b, k)),
                  pl.BlockSpec((CTX_LEN, HEAD_DIM), lambda k, b: (ctx0 + b, k)),
                  pl.BlockSpec((w, w), lambda k, b: (0, 0))],
        out_specs=pl.BlockSpec((None, S5_BROWS, w), lambda k, b: (k, b, 0)),
        out_shape=jax.ShapeDtypeStruct((S5_PACKS, BATCH * S5_BROWS, w), BF16),
        compiler_params=_cp("arbitrary", "arbitrary"),
        name="s5_pack",
    )(u, u, perm)


def _s5_kernel(u_ref, wst_ref, wy_ref, a1_ref, a2_ref, y_ref, s_ref, ssw_ref, hf_ref, hr_ref):
    gb = wst_ref.shape[0]
    nc = S5_NCHUNK
    sw = 2 * S5_STATE
    for j in range(gb):
        ug = u_ref[:, j * S5_CW:(j + 1) * S5_CW]
        s4 = jnp.dot(ug, wst_ref[j], preferred_element_type=F32)
        for b in range(BATCH):
            rows = s4[b * nc:(b + 1) * nc]
            s_ref[:, j, b, :] = rows[:, 0:sw]
            s_ref[:, j, BATCH + b, :] = rows[:, sw:2 * sw]
            ssw_ref[:, j, b, :] = rows[:, 2 * sw:3 * sw]
            ssw_ref[:, j, BATCH + b, :] = rows[:, 3 * sw:4 * sw]
    a1 = a1_ref[...]
    a2 = a2_ref[...]
    fwd = lax.broadcasted_iota(jnp.int32, (gb, 2 * BATCH, sw), 1) < BATCH

    def step(i, carry):
        h, hs = carry
        ri = jnp.where(i < S5_CTX_CHUNKS, S5_CTX_CHUNKS - 1 - i, nc - 1 + S5_CTX_CHUNKS - i)
        hf_ref[i] = h
        hr_ref[ri] = h
        s = jnp.where(fwd, s_ref[i], s_ref[ri])
        ssw = jnp.where(fwd, ssw_ref[i], ssw_ref[ri])
        return a1 * h + a2 * hs + s, a1 * hs - a2 * h + ssw

    z = jnp.zeros((gb, 2 * BATCH, sw), F32)
    lax.fori_loop(0, nc, step, (z, z), unroll=4)
    for j in range(gb):
        ug = u_ref[:, j * S5_CW:(j + 1) * S5_CW]
        hf = jnp.concatenate([hf_ref[:, j, b, :] for b in range(BATCH)], axis=0)
        hr = jnp.concatenate([hr_ref[:, j, BATCH + b, :] for b in range(BATCH)], axis=0)
        lhs = jnp.concatenate([ug, hf.astype(BF16), hr.astype(BF16)], axis=1)
        y_ref[:, j * S5_CW:(j + 1) * S5_CW] = jnp.dot(lhs, wy_ref[j], preferred_element_type=F32)


def s5_core(u_packed, wst, wy, a1, a2):
    gb = S5_GB
    per_pack = S5_PACK // gb
    rows = u_packed.shape[1]
    sw = 2 * S5_STATE
    blk = lambda *shape: pl.BlockSpec((gb,) + shape, lambda i: (i,) + (0,) * len(shape))
    io = pl.BlockSpec((None, rows, gb * S5_CW), lambda i: (i // per_pack, 0, i % per_pack))
    state = pltpu.VMEM((S5_NCHUNK, gb, 2 * BATCH, sw), F32)
    return pl.pallas_call(
        _s5_kernel,
        grid=(S5_GROUPS // gb,),
        in_specs=[io, blk(S5_CW, 4 * sw), blk(S5_CW + 2 * sw, S5_CW), blk(2 * BATCH, sw), blk(2 * BATCH, sw)],
        out_specs=io,
        out_shape=jax.ShapeDtypeStruct(u_packed.shape, F32),
        scratch_shapes=[state, state, state, state],
        compiler_params=_cp("arbitrary"),
        name="s5_core",
    )(u_packed, wst, wy, a1, a2)


def _s5_unpack_kernel(y_ref, pt_ref, lat_ref, ctx_ref):
    y = y_ref[...]
    hi = y.astype(BF16)
    lo = (y - hi.astype(F32)).astype(BF16)
    yn = (jnp.dot(hi, pt_ref[...], preferred_element_type=F32)
          + jnp.dot(lo, pt_ref[...], preferred_element_type=F32))
    for tl in range(S5_CHUNK):
        piece = yn[:, tl * HEAD_DIM:(tl + 1) * HEAD_DIM]
        ctx_ref[pl.ds(tl, S5_CTX_CHUNKS, stride=S5_CHUNK), :] = piece[:S5_CTX_CHUNKS]
        lat_ref[pl.ds(tl, SEQ // S5_CHUNK, stride=S5_CHUNK), :] = piece[S5_CTX_CHUNKS:]


def s5_unpack(y_packed, perm_t):
    w = S5_PACK * S5_CW
    return pl.pallas_call(
        _s5_unpack_kernel,
        grid=(S5_PACKS, BATCH),
        in_specs=[pl.BlockSpec((None, S5_BROWS, w), lambda k, b: (k, b, 0)),
                  pl.BlockSpec((w, w), lambda k, b: (0, 0))],
        out_specs=[pl.BlockSpec((SEQ, HEAD_DIM), lambda k, b: (b, k)),
                   pl.BlockSpec((CTX_LEN, HEAD_DIM), lambda k, b: (b, k))],
        out_shape=[jax.ShapeDtypeStruct((T_LAT, D_MODEL), F32), jax.ShapeDtypeStruct((T_CTX, D_MODEL), F32)],
        compiler_params=_cp("arbitrary", "arbitrary"),
        name="s5_unpack",
    )(y_packed, perm_t)


def _glu_kernel(u_ref, y_ref, d_ref, wa_ref, wg_ref, x_ref, gate_ref, o_ref, h_ref):
    @pl.when(pl.program_id(1) == 0)
    def _():
        y = u_ref[...] * d_ref[...] + y_ref[...]
        h_ref[...] = jax.nn.gelu(y).astype(BF16)

    h = h_ref[...]
    za = jnp.dot(h, wa_ref[...].astype(BF16), preferred_element_type=F32)
    zg = jnp.dot(h, wg_ref[...].astype(BF16), preferred_element_type=F32)
    o_ref[...] = x_ref[...] + gate_ref[...] * (za * jax.nn.sigmoid(zg))


def glu_residual(u, y, d_all, w_all, wi, x, mod, layer, gate_col, tm=512, tn=512):
    t = u.shape[0]
    nj = D_MODEL // tn
    row = pl.BlockSpec((tm, D_MODEL), lambda i, j: (i, 0))
    return pl.pallas_call(
        _glu_kernel,
        grid=(t // tm, nj),
        in_specs=[row, row,
                  pl.BlockSpec((None, 1, D_MODEL), lambda i, j: (wi, 0, 0)),
                  pl.BlockSpec((None, D_MODEL, tn), lambda i, j: (wi, 0, j)),
                  pl.BlockSpec((None, D_MODEL, tn), lambda i, j: (wi, 0, nj + j)),
                  pl.BlockSpec((tm, tn), lambda i, j: (i, j)),
                  _gate_spec(layer, gate_col, tm, tn)],
        out_specs=pl.BlockSpec((tm, tn), lambda i, j: (i, j)),
        out_shape=jax.ShapeDtypeStruct((t, D_MODEL), F32),
        scratch_shapes=[pltpu.VMEM((tm, D_MODEL), BF16)],
        compiler_params=_cp("arbitrary", "arbitrary"),
        name="glu_residual",
    )(u, y, d_all.reshape(-1, 1, D_MODEL), w_all, w_all, x, mod)


ROUTE_LANES = 128


def _route_kernel(x_ref, g_ref, sh_ref, sc_ref, wr_ref, br_ref, h_ref, r_ref, cnt_ref, carry_ref):
    @pl.when(pl.program_id(0) == 0)
    def _():
        carry_ref[...] = jnp.zeros_like(carry_ref)

    h = _norm_mod(x_ref[...], g_ref[...], sh_ref[...], sc_ref[...])
    h_ref[...] = h
    lg = jnp.dot(h, wr_ref[...], precision=lax.Precision.HIGHEST, preferred_element_type=F32) + br_ref[...]
    lane = lax.broadcasted_iota(jnp.int32, lg.shape, 1)
    ninf = -jnp.inf
    coarse = lane < MOE_GROUPS
    lc = jnp.where(coarse, lg, ninf)
    mc = jnp.max(lc, axis=-1, keepdims=True)
    g_sel = jnp.min(jnp.where(lc == mc, lane, ROUTE_LANES), axis=-1, keepdims=True)
    p_sel = 1.0 / jnp.sum(jnp.where(coarse, jnp.exp(lc - mc), 0.0), axis=-1, keepdims=True)
    lo = MOE_GROUPS + MOE_EXPERTS_PER_GROUP * g_sel
    lf = jnp.where((lane >= lo) & (lane < lo + MOE_EXPERTS_PER_GROUP), lg, ninf)
    v0 = jnp.max(lf, axis=-1, keepdims=True)
    i0 = jnp.min(jnp.where(lf == v0, lane, ROUTE_LANES), axis=-1, keepdims=True)
    lf2 = jnp.where(lane == i0, ninf, lf)
    v1 = jnp.max(lf2, axis=-1, keepdims=True)
    i1 = jnp.min(jnp.where(lf2 == v1, lane, ROUTE_LANES), axis=-1, keepdims=True)
    e1 = jnp.exp(v1 - v0)
    w0 = p_sel / (1.0 + e1)
    w1 = w0 * e1
    tm = lg.shape[0]
    lower = (lax.broadcasted_iota(jnp.int32, (tm, tm), 1)
             < lax.broadcasted_iota(jnp.int32, (tm, tm), 0)).astype(BF16)
    carry = carry_ref[...]
    hot0 = (lane == i0).astype(F32)
    before0 = carry + jnp.dot(lower, hot0.astype(BF16), preferred_element_type=F32)
    rank0 = jnp.sum(hot0 * before0, axis=-1, keepdims=True)
    carry = carry + jnp.sum(hot0, axis=0, keepdims=True)
    hot1 = (lane == i1).astype(F32)
    before1 = carry + jnp.dot(lower, hot1.astype(BF16), preferred_element_type=F32)
    rank1 = jnp.sum(hot1 * before1, axis=-1, keepdims=True)
    carry = carry + jnp.sum(hot1, axis=0, keepdims=True)
    carry_ref[...] = carry
    cnt_ref[...] = carry
    cols = [(i0 - MOE_GROUPS).astype(F32), (i1 - MOE_GROUPS).astype(F32), w0, w1, rank0, rank1]
    r = jnp.zeros_like(lg)
    for c, val in enumerate(cols):
        r = jnp.where(lane == c, val, r)
    r_ref[...] = r


ROUTE_E0, ROUTE_E1, ROUTE_W0, ROUTE_W1, ROUTE_R0, ROUTE_R1 = range(6)


def moe_route(x, g_all, mod, layer, sh_col, sc_col, w_route, b_route, tm=512):
    t = x.shape[0]
    mspec = lambda col: pl.BlockSpec((None, 1, D_MODEL),
                                     lambda i: (layer * MOD_ROWS + _mod_row(i, tm), 0, col))
    return pl.pallas_call(
        _route_kernel,
        grid=(t // tm,),
        in_specs=[pl.BlockSpec((tm, D_MODEL), lambda i: (i, 0)),
                  pl.BlockSpec((None, 1, D_MODEL), lambda i: (layer, 0, 0)),
                  mspec(sh_col), mspec(sc_col),
                  pl.BlockSpec((D_MODEL, ROUTE_LANES), lambda i: (0, 0)),
                  pl.BlockSpec((1, ROUTE_LANES), lambda i: (0, 0))],
        out_specs=[pl.BlockSpec((tm, D_MODEL), lambda i: (i, 0)),
                   pl.BlockSpec((tm, ROUTE_LANES), lambda i: (i, 0)),
                   pl.BlockSpec((1, ROUTE_LANES), lambda i: (0, 0))],
        out_shape=[jax.ShapeDtypeStruct((t, D_MODEL), F32),
                   jax.ShapeDtypeStruct((t, ROUTE_LANES), F32),
                   jax.ShapeDtypeStruct((1, ROUTE_LANES), F32)],
        scratch_shapes=[pltpu.VMEM((1, ROUTE_LANES), F32)],
        compiler_params=_cp("arbitrary"),
        name="moe_route",
    )(x, g_all.reshape(DEPTH, 1, D_MODEL), mod, mod, w_route, b_route)


def moe_plan(route, counts):
    cnt = counts[0, MOE_GROUPS:MOE_GROUPS + MOE_EXPERTS].astype(jnp.int32)
    padded = ((cnt + MOE_TILE - 1) // MOE_TILE) * MOE_TILE
    pad_end = jnp.cumsum(padded)
    pad_off = pad_end - padded
    e = route[:, ROUTE_E0:ROUTE_E1 + 1].astype(jnp.int32)
    rank = route[:, ROUTE_R0:ROUTE_R1 + 1].astype(jnp.int32)
    hot = e[:, :, None] == jnp.arange(MOE_EXPERTS, dtype=jnp.int32)[None, None, :]
    dest = jnp.sum(jnp.where(hot, pad_off[None, None, :], 0), axis=-1) + rank
    n_used = (pad_end[-1] // MOE_TILE).astype(jnp.int32).reshape(1)
    tile_start = jnp.arange(MOE_NT, dtype=jnp.int32) * MOE_TILE
    tile_expert = jnp.sum((pad_end[None, :] <= tile_start[:, None]).astype(jnp.int32), axis=1)
    tile_expert = jnp.minimum(tile_expert, MOE_EXPERTS - 1).astype(jnp.int32)
    tok = jnp.zeros((MOE_NT * MOE_TILE,), jnp.int32).at[dest.reshape(-1)].set(
        jnp.arange(dest.size, dtype=jnp.int32) // 2)
    return dest, tok, tile_expert, n_used


def _dispatch_kernel(nu_ref, tok_ref, tokn_ref, h_hbm, o_ref, buf, sem):
    i = pl.program_id(0)
    nu = nu_ref[0]

    def row_copy(t, slot, r):
        return pltpu.make_async_copy(h_hbm.at[pl.ds(t, 1), :], buf.at[slot, pl.ds(r, 1), :], sem.at[slot])

    def issue(tref, slot):
        def body(r, c):
            row_copy(tref[0, 0, r], slot, r).start()
            return c
        lax.fori_loop(0, MOE_TILE, body, 0, unroll=8)

    @pl.when(i == 0)
    def _():
        issue(tok_ref, 0)

    @pl.when(i + 1 < nu)
    def _():
        issue(tokn_ref, (i + 1) % 2)

    @pl.when(i < nu)
    def _():
        slot = i % 2

        def drain(r, c):
            row_copy(0, slot, r).wait()
            return c
        lax.fori_loop(0, MOE_TILE, drain, 0, unroll=8)
        o_ref[...] = buf[slot]

    @pl.when(i >= nu)
    def _():
        o_ref[...] = jnp.zeros_like(o_ref)


def moe_dispatch(h, tok, n_used):
    smem_tile = lambda off: pl.BlockSpec(
        (1, 1, MOE_TILE), lambda i, nu: (jnp.minimum(i + off, MOE_NT - 1), 0, 0), memory_space=pltpu.SMEM)
    grid_spec = pltpu.PrefetchScalarGridSpec(
        num_scalar_prefetch=1,
        grid=(MOE_NT,),
        in_specs=[smem_tile(0), smem_tile(1), pl.BlockSpec(memory_space=pl.ANY)],
        out_specs=pl.BlockSpec((MOE_TILE, D_MODEL), lambda i, nu: (i, 0)),
        scratch_shapes=[pltpu.VMEM((2, MOE_TILE, D_MODEL), h.dtype), pltpu.SemaphoreType.DMA((2,))])
    tok3 = tok.reshape(MOE_NT, 1, MOE_TILE)
    return pl.pallas_call(
        _dispatch_kernel,
        grid_spec=grid_spec,
        out_shape=jax.ShapeDtypeStruct((MOE_NT * MOE_TILE, D_MODEL), h.dtype),
        compiler_params=_cp("arbitrary"),
        name="moe_dispatch",
    )(n_used, tok3, tok3, h)


def _expert_kernel(te_ref, nu_ref, hs_ref, wg_ref, wu_ref, wd_ref, o_ref, wgb, wub, wdb):
    i = pl.program_id(0)

    @pl.when(i < nu_ref[0])
    def _():
        @pl.when((i == 0) | (te_ref[i] != te_ref[jnp.maximum(i - 1, 0)]))
        def _():
            wgb[...] = wg_ref[...].astype(BF16)
            wub[...] = wu_ref[...].astype(BF16)
            wdb[...] = wd_ref[...].astype(BF16)

        h = hs_ref[...].astype(BF16)
        g = jnp.dot(h, wgb[...], preferred_element_type=F32)
        u = jnp.dot(h, wub[...], preferred_element_type=F32)
        hid = (jax.nn.silu(g) * u).astype(BF16)
        o_ref[...] = jnp.dot(hid, wdb[...], preferred_element_type=F32)

    @pl.when(i >= nu_ref[0])
    def _():
        o_ref[...] = jnp.zeros_like(o_ref)


def moe_experts(hs, tile_expert, n_used, w_gate, w_up, w_down, layer):
    def wspec(shape):
        return pl.BlockSpec((None, None, None) + shape,
                            lambda i, te, nu: (layer, te[i] // MOE_EXPERTS_PER_GROUP,
                                               te[i] % MOE_EXPERTS_PER_GROUP, 0, 0))

    grid_spec = pltpu.PrefetchScalarGridSpec(
        num_scalar_prefetch=2,
        grid=(MOE_NT,),
        in_specs=[pl.BlockSpec((MOE_TILE, D_MODEL), lambda i, te, nu: (jnp.minimum(i, nu[0] - 1), 0)),
                  wspec((D_MODEL, MOE_FFN)), wspec((D_MODEL, MOE_FFN)), wspec((MOE_FFN, D_MODEL))],
        out_specs=pl.BlockSpec((MOE_TILE, D_MODEL), lambda i, te, nu: (i, 0)),
        scratch_shapes=[pltpu.VMEM((D_MODEL, MOE_FFN), BF16),
                        pltpu.VMEM((D_MODEL, MOE_FFN), BF16),
                        pltpu.VMEM((MOE_FFN, D_MODEL), BF16)])
    return pl.pallas_call(
        _expert_kernel,
        grid_spec=grid_spec,
        out_shape=jax.ShapeDtypeStruct((MOE_NT * MOE_TILE, D_MODEL), F32),
        compiler_params=_cp("arbitrary"),
        name="moe_experts",
    )(tile_expert, n_used, hs, w_gate, w_up, w_down)


def _combine_kernel(dest_ref, destn_ref, ys_hbm, r_ref, x_ref, gate_ref, o_ref, buf, sem):
    i = pl.program_id(0)
    n = pl.num_programs(0)
    tm = x_ref.shape[0]

    def row_copy(d, slot, k, r):
        return pltpu.make_async_copy(ys_hbm.at[pl.ds(d, 1), :], buf.at[slot, k, pl.ds(r, 1), :], sem.at[slot])

    def issue(dref, slot):
        def body(r, c):
            row_copy(dref[0, 0, 2 * r], slot, 0, r).start()
            row_copy(dref[0, 0, 2 * r + 1], slot, 1, r).start()
            return c
        lax.fori_loop(0, tm, body, 0, unroll=8)

    @pl.when(i == 0)
    def _():
        issue(dest_ref, 0)

    @pl.when(i + 1 < n)
    def _():
        issue(destn_ref, (i + 1) % 2)

    slot = i % 2

    def drain(r, c):
        row_copy(0, slot, 0, r).wait()
        row_copy(0, slot, 1, r).wait()
        return c
    lax.fori_loop(0, tm, drain, 0, unroll=8)
    r = r_ref[...]
    y = r[:, ROUTE_W0:ROUTE_W0 + 1] * buf[slot, 0] + r[:, ROUTE_W1:ROUTE_W1 + 1] * buf[slot, 1]
    o_ref[...] = x_ref[...] + gate_ref[...] * y


def moe_combine(ys, dest, route, x, mod, layer, gate_col, tm=256):
    t = x.shape[0]
    row = pl.BlockSpec((tm, D_MODEL), lambda i: (i, 0))
    nblk = t // tm
    smem = lambda off: pl.BlockSpec((1, 1, 2 * tm), lambda i: (jnp.minimum(i + off, nblk - 1), 0, 0),
                                    memory_space=pltpu.SMEM)
    dest3 = dest.reshape(nblk, 1, 2 * tm)
    return pl.pallas_call(
        _combine_kernel,
        grid=(nblk,),
        in_specs=[smem(0), smem(1),
                  pl.BlockSpec(memory_space=pl.ANY),
                  pl.BlockSpec((tm, ROUTE_LANES), lambda i: (i, 0)),
                  row,
                  pl.BlockSpec((None, 1, D_MODEL),
                               lambda i: (layer * MOD_ROWS + _mod_row(i, tm), 0, gate_col))],
        out_specs=row,
        out_shape=jax.ShapeDtypeStruct((t, D_MODEL), F32),
        scratch_shapes=[pltpu.VMEM((2, 2, tm, D_MODEL), F32), pltpu.SemaphoreType.DMA((2,))],
        compiler_params=_cp("arbitrary"),
        name="moe_combine",
    )(dest3, dest3, ys, route, x, mod)


def _final_norm_kernel(x_ref, g_ref, o_ref):
    x = x_ref[...]
    o_ref[...] = x * lax.rsqrt(jnp.mean(x * x, axis=-1, keepdims=True) + RMS_EPS) * g_ref[...]


def final_norm(x, g, rows, tm=512):
    row = pl.BlockSpec((tm, D_MODEL), lambda i: (i, 0))
    return pl.pallas_call(
        _final_norm_kernel,
        grid=(rows // tm,),
        in_specs=[row, pl.BlockSpec((1, D_MODEL), lambda i: (0, 0))],
        out_specs=row,
        out_shape=jax.ShapeDtypeStruct((rows, D_MODEL), F32),
        compiler_params=_cp("arbitrary"),
        name="final_norm",
    )(x, g.reshape(1, D_MODEL))


def attention_layer(xs, mod, layer, norm1_g, w_in, w_out, rpb, q_gain, k_gain, cos_t, sin_t):
    li = layer // 2
    p = norm_mod_matmul(xs, norm1_g, mod, w_in, layer, li, 0, 1)
    gains = jnp.stack([q_gain[li], k_gain[li]]).reshape(2, 1, HEAD_DIM)
    qkv = gqa_prep(p, gains, cos_t, sin_t)
    kcol, vcol = GQA_Q_HEADS, GQA_Q_HEADS + GQA_KV_HEADS
    grp = GQA_Q_HEADS // GQA_KV_HEADS
    ctx0 = T_LAT // CTX_LEN

    oa = na_attention(p, na_bias_table(rpb[li]))
    tq = 128
    nq = SEQ // tq
    ob = attention(
        qkv, lambda b, k, q: (b * nq + q, k),
        [qkv, qkv],
        [lambda b, k, q: (b, kcol + k), lambda b, k, q: (ctx0 + b, kcol + k)],
        [lambda b, k, q: (b, vcol + k), lambda b, k, q: (ctx0 + b, vcol + k)],
        [SEQ, CTX_LEN], (BATCH, GQA_KV_HEADS, nq), tq, grp, 1.0,
        T_LAT, GQA_Q_HEADS * HEAD_DIM, lambda b, k, q: (b * nq + q, k), "gqa_latent")
    oac = attention(
        p, lambda b, h: (ctx0 + b, h),
        [p],
        [lambda b, h: (ctx0 + b, NA_HEADS + h)],
        [lambda b, h: (ctx0 + b, 2 * NA_HEADS + h)],
        [CTX_LEN], (BATCH, NA_HEADS), CTX_LEN, 1, ATT_SCALE * LOG2E,
        T_CTX, NA_HEADS * HEAD_DIM, lambda b, h: (b, h), "na_context")
    obc = attention(
        qkv, lambda b, k: (ctx0 + b, k),
        [qkv],
        [lambda b, k: (ctx0 + b, kcol + k)],
        [lambda b, k: (ctx0 + b, vcol + k)],
        [CTX_LEN], (BATCH, GQA_KV_HEADS), CTX_LEN, grp, 1.0,
        T_CTX, GQA_Q_HEADS * HEAD_DIM, lambda b, k: (b, k), "gqa_context")
    oa_all = jnp.concatenate([oa, oac], axis=0)
    ob_all = jnp.concatenate([ob, obc], axis=0)
    return proj_residual(oa_all, ob_all, w_out, li, xs, mod, layer, 2)


def s5_layer(xs, mod, layer, norm1_g, w_in, lam_re, lam_im, log_dt, b_re, b_im, c_re, c_im, d_skip, w_glu):
    li = layer // 2
    u = norm_mod_matmul(xs, norm1_g, mod, w_in, layer, li, 0, 1)
    wst, wy, a1, a2 = s5_weights(lam_re[li], lam_im[li], log_dt[li], b_re[li], b_im[li], c_re[li], c_im[li])
    perm = s5_lane_permutation()
    y_lat, y_ctx = s5_unpack(s5_core(s5_pack(u, perm), wst, wy, a1, a2), perm.T)
    y = jnp.concatenate([y_lat, y_ctx], axis=0)
    return glu_residual(u, y, d_skip, w_glu, li, xs, mod, layer, 2)


def moe_layer(xs, mod, layer, norm2_g, w_coarse, b_coarse, w_fine, b_fine, w_gate, w_up, w_down):
    wf = w_fine[layer].transpose(1, 0, 2).reshape(D_MODEL, MOE_EXPERTS)
    w_route = jnp.concatenate([w_coarse[layer], wf], axis=1).astype(F32)
    w_route = jnp.pad(w_route, ((0, 0), (0, ROUTE_LANES - w_route.shape[1])))
    b_route = jnp.concatenate([b_coarse[layer], b_fine[layer].reshape(-1)]).astype(F32)
    b_route = jnp.pad(b_route, (0, ROUTE_LANES - b_route.shape[0])).reshape(1, ROUTE_LANES)
    h, route, counts = moe_route(xs, norm2_g, mod, layer, 3, 4, w_route, b_route)
    dest, tok, tile_expert, n_used = moe_plan(route, counts)
    hs = moe_dispatch(h, tok, n_used)
    ys = moe_experts(hs, tile_expert, n_used, w_gate, w_up, w_down, layer)
    return moe_combine(ys, dest, route, xs, mod, layer, 5)


def kernel(x, c, ctx, c_ctx, ada_w, ada_b, norm1_g, norm2_g, final_g, attn_w_in, attn_w_out, na_rpb, q_gain, k_gain, s5_w_in, s5_lam_re, s5_lam_im, s5_log_dt, s5_b_re, s5_b_im, s5_c_re, s5_c_im, s5_d, s5_w_glu, moe_w_coarse, moe_b_coarse, moe_w_fine, moe_b_fine, moe_w_gate, moe_w_up, moe_w_down):
    xs = jnp.concatenate([x.reshape(T_LAT, D_MODEL), ctx.reshape(T_CTX, D_MODEL)], axis=0)
    c8 = jnp.concatenate([c, c_ctx[None, :], jnp.zeros((MOD_ROWS - BATCH - 1, D_MODEL), F32)], axis=0)
    mod = ada_mod(c8, ada_w, ada_b).reshape(DEPTH * MOD_ROWS, 1, 6 * D_MODEL)
    cos_t, sin_t = rope_tables()
    for layer in range(DEPTH):
        if layer % 2 == 0:
            xs = attention_layer(xs, mod, layer, norm1_g, attn_w_in, attn_w_out, na_rpb, q_gain, k_gain,
                                 cos_t, sin_t)
        else:
            xs = s5_layer(xs, mod, layer, norm1_g, s5_w_in, s5_lam_re, s5_lam_im, s5_log_dt,
                          s5_b_re, s5_b_im, s5_c_re, s5_c_im, s5_d, s5_w_glu)
        xs = moe_layer(xs, mod, layer, norm2_g, moe_w_coarse, moe_b_coarse, moe_w_fine, moe_b_fine,
                       moe_w_gate, moe_w_up, moe_w_down)
    return final_norm(xs, final_g, T_LAT).reshape(BATCH, SEQ, D_MODEL)
```

```python
import functools
import math

import jax
import jax.numpy as jnp
from jax import lax
from jax.experimental import pallas as pl
from jax.experimental.pallas import tpu as pltpu

F32 = jnp.float32
BF16 = jnp.bfloat16

D_MODEL = 2048
BATCH = 4
SEQ = 4096
DEPTH = 4
GRID_W = 64
CTX_LEN = 256
HEAD_DIM = 128
NA_HEADS = 8
NA_ROWS = 8
NA_COLS = 16
GQA_Q_HEADS = 8
GQA_KV_HEADS = 2
ROPE_THETA = 10000.0
ROPE_AXIS_DIM = HEAD_DIM // 2
ATTN_IN_COLS = (3 * NA_HEADS + GQA_Q_HEADS + 2 * GQA_KV_HEADS) * HEAD_DIM
S5_GROUP = 16
S5_GROUPS = D_MODEL // S5_GROUP
S5_STATE = 64
MOE_GROUPS = 4
MOE_EXPERTS_PER_GROUP = 8
MOE_EXPERTS = MOE_GROUPS * MOE_EXPERTS_PER_GROUP
MOE_FFN = D_MODEL // 4
RMS_EPS = 1e-6
NEG_INF = -1e30

T_LAT = BATCH * SEQ
T_CTX = BATCH * CTX_LEN
T_ALL = T_LAT + T_CTX
MOD_ROWS = 8
ATT_SCALE = HEAD_DIM ** -0.5

S5_CHUNK = 16
S5_SEQ = CTX_LEN + SEQ
S5_NCHUNK = S5_SEQ // S5_CHUNK
S5_CW = S5_CHUNK * S5_GROUP
S5_GB = 4

MOE_TILE = 512
MOE_SLOTS = T_ALL * 2
MOE_NT = MOE_SLOTS // MOE_TILE + MOE_EXPERTS

VMEM_LIMIT = 56 * 1024 * 1024


def _cp(*sem):
    return pltpu.CompilerParams(dimension_semantics=sem, vmem_limit_bytes=VMEM_LIMIT)


def _mod_row(i, tm):
    return jnp.minimum((i * tm) // SEQ, BATCH)


def _ada_kernel(c_ref, w_ref, b_ref, o_ref):
    s = jax.nn.silu(c_ref[...])
    o_ref[...] = jnp.dot(s, w_ref[...], precision=lax.Precision.HIGHEST,
                         preferred_element_type=F32) + b_ref[...]


def ada_mod(c8, ada_w, ada_b):
    tn = 1024
    n = ada_w.shape[-1]
    return pl.pallas_call(
        _ada_kernel,
        grid=(DEPTH, n // tn),
        in_specs=[pl.BlockSpec((MOD_ROWS, D_MODEL), lambda l, j: (0, 0)),
                  pl.BlockSpec((None, D_MODEL, tn), lambda l, j: (l, 0, j)),
                  pl.BlockSpec((None, 1, tn), lambda l, j: (l, 0, j))],
        out_specs=pl.BlockSpec((None, MOD_ROWS, tn), lambda l, j: (l, 0, j)),
        out_shape=jax.ShapeDtypeStruct((DEPTH, MOD_ROWS, n), F32),
        compiler_params=_cp("arbitrary", "arbitrary"),
        name="ada_mod",
    )(c8, ada_w, ada_b.reshape(DEPTH, 1, n))


def _norm_mod(x, g, sh, sc):
    y = x * lax.rsqrt(jnp.mean(x * x, axis=-1, keepdims=True) + RMS_EPS) * g
    return y * (1.0 + sc) + sh


def _mod_spec(layer, col, tm):
    return pl.BlockSpec((None, 1, D_MODEL),
                        lambda i, j: (layer * MOD_ROWS + _mod_row(i, tm), 0, col))


def _gate_spec(layer, col, tm, tn):
    return pl.BlockSpec((None, 1, tn),
                        lambda i, j: (layer * MOD_ROWS + _mod_row(i, tm), 0, col * (D_MODEL // tn) + j))


def _nm_mm_kernel(x_ref, g_ref, sh_ref, sc_ref, w_ref, o_ref, h_ref):
    @pl.when(pl.program_id(1) == 0)
    def _():
        h_ref[...] = _norm_mod(x_ref[...], g_ref[...], sh_ref[...], sc_ref[...]).astype(BF16)

    o_ref[...] = jnp.dot(h_ref[...], w_ref[...].astype(BF16),
                         preferred_element_type=F32).astype(o_ref.dtype)


def norm_mod_matmul(x, g_all, mod, w_all, layer, wi, sh_col, sc_col, tm=1024, tn=512):
    t = x.shape[0]
    n = w_all.shape[-1]
    return pl.pallas_call(
        _nm_mm_kernel,
        grid=(t // tm, n // tn),
        in_specs=[pl.BlockSpec((tm, D_MODEL), lambda i, j: (i, 0)),
                  pl.BlockSpec((None, 1, D_MODEL), lambda i, j: (layer, 0, 0)),
                  _mod_spec(layer, sh_col, tm),
                  _mod_spec(layer, sc_col, tm),
                  pl.BlockSpec((None, D_MODEL, tn), lambda i, j: (wi, 0, j))],
        out_specs=pl.BlockSpec((tm, tn), lambda i, j: (i, j)),
        out_shape=jax.ShapeDtypeStruct((t, n), F32),
        scratch_shapes=[pltpu.VMEM((tm, D_MODEL), BF16)],
        compiler_params=_cp("arbitrary", "arbitrary"),
        name="norm_mod_matmul",
    )(x, g_all.reshape(DEPTH, 1, D_MODEL), mod, mod, w_all)


def _proj_res_kernel(a_ref, b_ref, wa_ref, wb_ref, x_ref, gate_ref, o_ref):
    y = jnp.dot(a_ref[...], wa_ref[...].astype(BF16), preferred_element_type=F32)
    y = y + jnp.dot(b_ref[...], wb_ref[...].astype(BF16), preferred_element_type=F32)
    o_ref[...] = x_ref[...] + gate_ref[...] * y


def proj_residual(oa, ob, w_all, wi, x, mod, layer, gate_col, tm=1024, tn=512):
    t, ka = oa.shape
    kb = ob.shape[1]
    assert ka == kb
    return pl.pallas_call(
        _proj_res_kernel,
        grid=(t // tm, D_MODEL // tn),
        in_specs=[pl.BlockSpec((tm, ka), lambda i, j: (i, 0)),
                  pl.BlockSpec((tm, kb), lambda i, j: (i, 0)),
                  pl.BlockSpec((None, ka, tn), lambda i, j: (wi, 0, j)),
                  pl.BlockSpec((None, kb, tn), lambda i, j: (wi, 1, j)),
                  pl.BlockSpec((tm, tn), lambda i, j: (i, j)),
                  _gate_spec(layer, gate_col, tm, tn)],
        out_specs=pl.BlockSpec((tm, tn), lambda i, j: (i, j)),
        out_shape=jax.ShapeDtypeStruct((t, D_MODEL), F32),
        compiler_params=_cp("arbitrary", "arbitrary"),
        name="proj_residual",
    )(oa, ob, w_all, w_all, x, mod)


def _gqa_prep_kernel(p_ref, gain_ref, cos_ref, sin_ref, o_ref):
    h = pl.program_id(1)
    nqk = GQA_Q_HEADS + GQA_KV_HEADS

    @pl.when(h < nqk)
    def _():
        x = p_ref[...]
        scale = jnp.where(h < GQA_Q_HEADS, ATT_SCALE * LOG2E, 1.0).astype(F32)
        y = x * lax.rsqrt(jnp.mean(x * x, axis=-1, keepdims=True) + RMS_EPS) * (gain_ref[...] * scale)
        lane = lax.broadcasted_iota(jnp.int32, y.shape, 1)
        half = ROPE_AXIS_DIM // 2
        partner = jnp.where((lane % ROPE_AXIS_DIM) < half,
                            pltpu.roll(y, HEAD_DIM - half, 1), pltpu.roll(y, half, 1))
        o_ref[...] = (y * cos_ref[...] + partner * sin_ref[...]).astype(BF16)

    @pl.when(h >= nqk)
    def _():
        o_ref[...] = p_ref[...].astype(BF16)


def gqa_prep(p, gains, cos_t, sin_t, tm=512):
    t = p.shape[0]
    nh = GQA_Q_HEADS + 2 * GQA_KV_HEADS
    col0 = 3 * NA_HEADS
    lat_tiles = T_LAT // tm
    per_seq = SEQ // tm

    def tab_idx(i, h):
        return (jnp.where(i < lat_tiles, i % per_seq, per_seq), 0)

    return pl.pallas_call(
        _gqa_prep_kernel,
        grid=(t // tm, nh),
        in_specs=[pl.BlockSpec((tm, HEAD_DIM), lambda i, h: (i, col0 + h)),
                  pl.BlockSpec((None, 1, HEAD_DIM),
                               lambda i, h: (jnp.where(h < GQA_Q_HEADS, 0, 1), 0, 0)),
                  pl.BlockSpec((tm, HEAD_DIM), tab_idx),
                  pl.BlockSpec((tm, HEAD_DIM), tab_idx)],
        out_specs=pl.BlockSpec((tm, HEAD_DIM), lambda i, h: (i, h)),
        out_shape=jax.ShapeDtypeStruct((t, nh * HEAD_DIM), BF16),
        compiler_params=_cp("arbitrary", "arbitrary"),
        name="gqa_prep",
    )(p, gains, cos_t, sin_t)


def rope_tables(tm=512):
    t = jnp.arange(SEQ)
    row = (t // GRID_W).astype(F32)
    col = (t % GRID_W).astype(F32)
    inv = ROPE_THETA ** (-jnp.arange(0, ROPE_AXIS_DIM, 2, dtype=F32) / ROPE_AXIS_DIM)
    ar = row[:, None] * inv[None]
    ac = col[:, None] * inv[None]
    cos_t = jnp.concatenate([jnp.cos(ar), jnp.cos(ar), jnp.cos(ac), jnp.cos(ac)], axis=-1)
    sin_t = jnp.concatenate([-jnp.sin(ar), jnp.sin(ar), -jnp.sin(ac), jnp.sin(ac)], axis=-1)
    cos_t = jnp.concatenate([cos_t, jnp.ones((tm, HEAD_DIM), F32)], axis=0)
    sin_t = jnp.concatenate([sin_t, jnp.zeros((tm, HEAD_DIM), F32)], axis=0)
    return cos_t, sin_t


LOG2E = 1.4426950408889634
ATT_KCHUNK = 512


def _softmax_chunk(q, k, v, bias, m, l, acc):
    s = lax.dot_general(q, k.astype(BF16), (((1,), (1,)), ((), ())), preferred_element_type=F32)
    if bias is not None:
        s = s + bias
    m_new = jnp.maximum(m, jnp.max(s, axis=-1, keepdims=True))
    alpha = jnp.exp2(m - m_new)
    p = jnp.exp2(s - m_new)
    l = alpha * l + jnp.sum(p, axis=-1, keepdims=True)
    acc = alpha * acc + jnp.dot(p.astype(BF16), v.astype(BF16), preferred_element_type=F32)
    return m_new, l, acc


def _attn_kernel(*refs, nseg, group, scale):
    q_ref = refs[0]
    k_refs = refs[1:1 + nseg]
    v_refs = refs[1 + nseg:1 + 2 * nseg]
    o_ref = refs[1 + 2 * nseg]
    tq = q_ref.shape[0]
    q = jnp.concatenate([q_ref[:, g * HEAD_DIM:(g + 1) * HEAD_DIM] for g in range(group)], axis=0)
    if scale != 1.0:
        q = q.astype(F32) * scale
    q = q.astype(BF16)
    m = jnp.full((q.shape[0], 1), -jnp.inf, F32)
    l = jnp.zeros((q.shape[0], 1), F32)
    acc = jnp.zeros((q.shape[0], HEAD_DIM), F32)
    for k_ref, v_ref in zip(k_refs, v_refs):
        nk = k_ref.shape[0]
        for c0 in range(0, nk, ATT_KCHUNK):
            cs = min(ATT_KCHUNK, nk - c0)
            m, l, acc = _softmax_chunk(q, k_ref[c0:c0 + cs, :], v_ref[c0:c0 + cs, :], None, m, l, acc)
    o = acc / l
    for g in range(group):
        o_ref[:, g * HEAD_DIM:(g + 1) * HEAD_DIM] = o[g * tq:(g + 1) * tq].astype(o_ref.dtype)


def attention(q_arr, q_idx, kv_arrs, k_idx, v_idx, kv_rows, grid, tq, group, scale, out_rows, out_cols, o_idx, name):
    nseg = len(kv_rows)
    in_specs = [pl.BlockSpec((tq, group * HEAD_DIM), q_idx)]
    in_specs += [pl.BlockSpec((kv_rows[i], HEAD_DIM), k_idx[i]) for i in range(nseg)]
    in_specs += [pl.BlockSpec((kv_rows[i], HEAD_DIM), v_idx[i]) for i in range(nseg)]
    return pl.pallas_call(
        functools.partial(_attn_kernel, nseg=nseg, group=group, scale=scale),
        grid=grid,
        in_specs=in_specs,
        out_specs=pl.BlockSpec((tq, group * HEAD_DIM), o_idx),
        out_shape=jax.ShapeDtypeStruct((out_rows, out_cols), BF16),
        compiler_params=_cp(*(["arbitrary"] * len(grid))),
        name=name,
    )(q_arr, *kv_arrs, *kv_arrs)


NA_QROWS = 8
NA_KROWS = 16
NA_TQ = NA_QROWS * GRID_W
NA_TK = NA_KROWS * GRID_W
GRID_H = SEQ // GRID_W


def _na_kernel(q_ref, k_ref, v_ref, kc_ref, vc_ref, bias_ref, o_ref):
    rb = pl.program_id(2)
    w0 = jnp.clip(NA_QROWS * rb - (NA_KROWS - NA_QROWS) // 2, 0, GRID_H - NA_KROWS) * GRID_W
    w0 = pl.multiple_of(w0, 4 * GRID_W)
    q = (q_ref[...] * (ATT_SCALE * LOG2E)).astype(BF16)
    m = jnp.full((NA_TQ, 1), -jnp.inf, F32)
    l = jnp.zeros((NA_TQ, 1), F32)
    acc = jnp.zeros((NA_TQ, HEAD_DIM), F32)
    m, l, acc = _softmax_chunk(q, kc_ref[...], vc_ref[...], None, m, l, acc)
    for c0 in range(0, NA_TK, ATT_KCHUNK):
        kw = k_ref[pl.ds(w0 + c0, ATT_KCHUNK), :]
        vw = v_ref[pl.ds(w0 + c0, ATT_KCHUNK), :]
        m, l, acc = _softmax_chunk(q, kw, vw, bias_ref[:, c0:c0 + ATT_KCHUNK], m, l, acc)
    o_ref[...] = (acc / l).astype(o_ref.dtype)


def na_bias_table(rpb):
    aw = jnp.arange(GRID_W)
    col_start = jnp.clip(aw - NA_COLS // 2, 0, GRID_W - NA_COLS)
    col_ok = (aw[None, :] >= col_start[:, None]) & (aw[None, :] < col_start[:, None] + NA_COLS)
    off_c = jnp.clip(aw[None, :] - aw[:, None] + (NA_COLS - 1), 0, 2 * NA_COLS - 2)
    nh, nr, nc = rpb.shape
    pick = (off_c.reshape(1, -1) == jnp.arange(nc)[:, None]).astype(F32)
    tiles = jnp.dot(rpb.astype(F32).reshape(nh * nr, nc), pick, precision=lax.Precision.HIGHEST)
    tiles = jnp.where(col_ok.reshape(1, -1), tiles * LOG2E, NEG_INF).reshape(nh, nr, GRID_W, GRID_W)
    blocked = jnp.full((nh, 1, GRID_W, GRID_W), NEG_INF, F32)
    tiles = jnp.concatenate([tiles, blocked], axis=1)
    nblk = GRID_H // NA_QROWS
    out = []
    for rb in (0, nblk // 2, nblk - 1):
        w0 = min(max(NA_QROWS * rb - (NA_KROWS - NA_QROWS) // 2, 0), GRID_H - NA_KROWS)
        picks = []
        for qr in range(NA_QROWS * rb, NA_QROWS * (rb + 1)):
            rs = min(max(qr - NA_ROWS // 2, 0), GRID_H - NA_ROWS)
            for kr in range(w0, w0 + NA_KROWS):
                picks.append(kr - qr + NA_ROWS - 1 if rs <= kr < rs + NA_ROWS else nr)
        b = jnp.stack([tiles[:, k] for k in picks], axis=1)
        b = b.reshape(nh, NA_QROWS, NA_KROWS, GRID_W, GRID_W).transpose(0, 1, 3, 2, 4)
        out.append(b.reshape(nh, NA_TQ, NA_TK))
    return jnp.stack(out, axis=1)


def na_attention(p, bias):
    nblk = GRID_H // NA_QROWS
    ctx_blk0 = T_LAT // CTX_LEN

    def pat(rb):
        return jnp.where(rb == 0, 0, jnp.where(rb == nblk - 1, 2, 1))

    return pl.pallas_call(
        _na_kernel,
        grid=(NA_HEADS, BATCH, nblk),
        in_specs=[pl.BlockSpec((NA_TQ, HEAD_DIM), lambda h, b, r: (b * nblk + r, h)),
                  pl.BlockSpec((SEQ, HEAD_DIM), lambda h, b, r: (b, NA_HEADS + h)),
                  pl.BlockSpec((SEQ, HEAD_DIM), lambda h, b, r: (b, 2 * NA_HEADS + h)),
                  pl.BlockSpec((CTX_LEN, HEAD_DIM), lambda h, b, r: (ctx_blk0 + b, NA_HEADS + h)),
                  pl.BlockSpec((CTX_LEN, HEAD_DIM), lambda h, b, r: (ctx_blk0 + b, 2 * NA_HEADS + h)),
                  pl.BlockSpec((None, None, NA_TQ, NA_TK), lambda h, b, r: (h, pat(r), 0, 0))],
        out_specs=pl.BlockSpec((NA_TQ, HEAD_DIM), lambda h, b, r: (b * nblk + r, h)),
        out_shape=jax.ShapeDtypeStruct((T_LAT, NA_HEADS * HEAD_DIM), BF16),
        compiler_params=_cp("arbitrary", "arbitrary", "arbitrary"),
        name="na_attention",
    )(p, p, p, p, p, bias)


def s5_weights(lam_re, lam_im, log_dt, b_re, b_im, c_re, c_im):
    hi = lax.Precision.HIGHEST
    L = S5_CHUNK
    lr = jnp.minimum(lam_re.astype(F32), -1e-4)
    li = lam_im.astype(F32)
    dt = jnp.exp(log_dt.astype(F32))[..., None]
    mag = jnp.exp(lr * dt)
    ar = mag * jnp.cos(li * dt)
    ai = mag * jnp.sin(li * dt)
    den = lr * lr + li * li
    xr = ar - 1.0
    fr = (xr * lr + ai * li) / den
    fi = (ai * lr - xr * li) / den
    br = b_re.astype(F32)
    bi = b_im.astype(F32)
    bbr = fr[..., None] * br - fi[..., None] * bi
    bbi = fr[..., None] * bi + fi[..., None] * br
    k = jnp.arange(L + 1, dtype=F32)[:, None, None, None]
    pm = jnp.exp(lr * dt * k)
    pr = pm * jnp.cos(li * dt * k)
    pi = pm * jnp.sin(li * dt * k)
    e_r = pr[..., None] * bbr - pi[..., None] * bbi
    e_i = pr[..., None] * bbi + pi[..., None] * bbr
    cr = c_re.astype(F32)
    ci = c_im.astype(F32)
    kern = (jnp.einsum('dgop,kdgpi->kdgoi', cr, e_r[:L], precision=hi)
            - jnp.einsum('dgop,kdgpi->kdgoi', ci, e_i[:L], precision=hi))
    s_i = jnp.arange(L)
    ng = lr.shape[1]
    lag = s_i[None, :, None] - s_i[:, None, None]
    place_f = (lag == s_i[None, None, :]).astype(F32)
    place = jnp.stack([place_f, place_f[::-1, ::-1]], axis=-1)
    toep = jnp.einsum('stkd,kdgoi->gsito', place, kern, precision=hi)
    w_toep_sum = toep.reshape(ng, S5_CW, S5_CW)
    st_r = e_r[L - 1 - s_i].transpose(1, 2, 0, 4, 3).reshape(2, ng, S5_CW, S5_STATE)
    st_i = e_i[L - 1 - s_i].transpose(1, 2, 0, 4, 3).reshape(2, ng, S5_CW, S5_STATE)
    w_state = jnp.concatenate([st_r, st_i], axis=-1)
    w_state_sw = jnp.concatenate([st_i, st_r], axis=-1)
    qr = pr[1:, :, :, None, :]
    qi = pi[1:, :, :, None, :]
    d_r = cr[None] * qr - ci[None] * qi
    d_i = cr[None] * qi + ci[None] * qr
    wo_r = d_r.transpose(1, 2, 4, 0, 3).reshape(2, ng, S5_STATE, S5_CW)
    wo_i = (-d_i).transpose(1, 2, 4, 0, 3).reshape(2, ng, S5_STATE, S5_CW)
    w_out = jnp.concatenate([wo_r, wo_i], axis=2)
    flip_rows = lambda w: w.reshape(ng, L, S5_GROUP, w.shape[-1])[:, ::-1].reshape(w.shape)
    flip_cols = lambda w: w.reshape(ng, w.shape[1], L, S5_GROUP)[:, :, ::-1].reshape(w.shape)
    wst = jnp.concatenate([w_state[0], flip_rows(w_state[1]),
                           w_state_sw[0], flip_rows(w_state_sw[1])], axis=-1).astype(BF16)
    wy = jnp.concatenate([w_toep_sum, w_out[0], flip_cols(w_out[1])], axis=1).astype(BF16)
    a1 = jnp.concatenate([pr[L], pr[L]], axis=-1)
    a2 = jnp.concatenate([-pi[L], pi[L]], axis=-1)
    per_row = lambda a: jnp.repeat(a.transpose(1, 0, 2), BATCH, axis=1)
    return wst, wy, per_row(a1), per_row(a2)


S5_CTX_CHUNKS = CTX_LEN // S5_CHUNK
S5_PACK = HEAD_DIM // S5_GROUP
S5_PACKS = S5_GROUPS // S5_PACK
S5_BROWS = S5_NCHUNK


def s5_lane_permutation():
    tl, g, hh = jnp.meshgrid(jnp.arange(S5_CHUNK), jnp.arange(S5_PACK), jnp.arange(S5_GROUP), indexing="ij")
    dst = (g * S5_CW + tl * S5_GROUP + hh).reshape(-1)
    return (dst[:, None] == jnp.arange(S5_PACK * S5_CW)[None, :]).astype(BF16)


S5_SB = 2


def _s5_pack_kernel(lat_ref, ctx_ref, p_ref, o_ref):
    cols = []
    for tl in range(S5_CHUNK):
        rows = []
        for b in range(S5_SB):
            rows.append(ctx_ref[pl.ds(b * CTX_LEN + tl, S5_CTX_CHUNKS, stride=S5_CHUNK), :])
            rows.append(lat_ref[pl.ds(b * SEQ + tl, SEQ // S5_CHUNK, stride=S5_CHUNK), :])
        cols.append(jnp.concatenate(rows, axis=0))
    xcat = jnp.concatenate(cols, axis=1).astype(BF16)
    o_ref[...] = jnp.dot(xcat, p_ref[...], preferred_element_type=F32).astype(BF16)


def s5_pack(u, perm):
    ctx0 = T_LAT // (S5_SB * CTX_LEN)
    w = S5_PACK * S5_CW
    return pl.pallas_call(
        _s5_pack_kernel,
        grid=(S5_PACKS, BATCH // S5_SB),
        in_specs=[pl.BlockSpec((S5_SB * SEQ, HEAD_DIM), lambda k, b: (b, k)),
                  pl.BlockSpec((S5_SB * CTX_LEN, HEAD_DIM), lambda k, b: (ctx0 + b, k)),
                  pl.BlockSpec((w, w), lambda k, b: (0, 0))],
        out_specs=pl.BlockSpec((None, S5_SB * S5_BROWS, w), lambda k, b: (k, b, 0)),
        out_shape=jax.ShapeDtypeStruct((S5_PACKS, BATCH * S5_BROWS, w), BF16),
        compiler_params=_cp("arbitrary", "arbitrary"),
        name="s5_pack",
    )(u, u, perm)


def _s5_kernel(u_ref, wst_ref, wy_ref, a1_ref, a2_ref, y_ref, s_ref, ssw_ref, hf_ref, hr_ref):
    gb = wst_ref.shape[0]
    nc = S5_NCHUNK
    sw = 2 * S5_STATE
    for j in range(gb):
        ug = u_ref[:, j * S5_CW:(j + 1) * S5_CW]
        s4 = jnp.dot(ug, wst_ref[j], preferred_element_type=F32)
        for b in range(BATCH):
            rows = s4[b * nc:(b + 1) * nc]
            s_ref[:, j, b, :] = rows[:, 0:sw]
            s_ref[:, j, BATCH + b, :] = rows[:, sw:2 * sw]
            ssw_ref[:, j, b, :] = rows[:, 2 * sw:3 * sw]
            ssw_ref[:, j, BATCH + b, :] = rows[:, 3 * sw:4 * sw]
    a1 = a1_ref[...]
    a2 = a2_ref[...]
    fwd = lax.broadcasted_iota(jnp.int32, (gb, 2 * BATCH, sw), 1) < BATCH

    def step(i, carry):
        h, hs = carry
        ri = jnp.where(i < S5_CTX_CHUNKS, S5_CTX_CHUNKS - 1 - i, nc - 1 + S5_CTX_CHUNKS - i)
        hf_ref[i] = h
        hr_ref[ri] = h
        s = jnp.where(fwd, s_ref[i], s_ref[ri])
        ssw = jnp.where(fwd, ssw_ref[i], ssw_ref[ri])
        return a1 * h + a2 * hs + s, a1 * hs - a2 * h + ssw

    z = jnp.zeros((gb, 2 * BATCH, sw), F32)
    lax.fori_loop(0, nc, step, (z, z), unroll=4)
    for j in range(gb):
        ug = u_ref[:, j * S5_CW:(j + 1) * S5_CW]
        hf = jnp.concatenate([hf_ref[:, j, b, :] for b in range(BATCH)], axis=0)
        hr = jnp.concatenate([hr_ref[:, j, BATCH + b, :] for b in range(BATCH)], axis=0)
        lhs = jnp.concatenate([ug, hf.astype(BF16), hr.astype(BF16)], axis=1)
        y_ref[:, j * S5_CW:(j + 1) * S5_CW] = jnp.dot(lhs, wy_ref[j], preferred_element_type=F32)


def s5_core(u_packed, wst, wy, a1, a2):
    gb = S5_GB
    per_pack = S5_PACK // gb
    rows = u_packed.shape[1]
    sw = 2 * S5_STATE
    blk = lambda *shape: pl.BlockSpec((gb,) + shape, lambda i: (i,) + (0,) * len(shape))
    io = pl.BlockSpec((None, rows, gb * S5_CW), lambda i: (i // per_pack, 0, i % per_pack))
    state = pltpu.VMEM((S5_NCHUNK, gb, 2 * BATCH, sw), F32)
    return pl.pallas_call(
        _s5_kernel,
        grid=(S5_GROUPS // gb,),
        in_specs=[io, blk(S5_CW, 4 * sw), blk(S5_CW + 2 * sw, S5_CW), blk(2 * BATCH, sw), blk(2 * BATCH, sw)],
        out_specs=io,
        out_shape=jax.ShapeDtypeStruct(u_packed.shape, F32),
        scratch_shapes=[state, state, state, state],
        compiler_params=_cp("arbitrary"),
        name="s5_core",
    )(u_packed, wst, wy, a1, a2)


def _s5_unpack_kernel(y_ref, pt_ref, lat_ref, ctx_ref):
    y = y_ref[...]
    hi = y.astype(BF16)
    lo = (y - hi.astype(F32)).astype(BF16)
    yn = (jnp.dot(hi, pt_ref[...], preferred_element_type=F32)
          + jnp.dot(lo, pt_ref[...], preferred_element_type=F32))
    for tl in range(S5_CHUNK):
        piece = yn[:, tl * HEAD_DIM:(tl + 1) * HEAD_DIM]
        for b in range(S5_SB):
            r0 = b * S5_BROWS
            ctx_ref[pl.ds(b * CTX_LEN + tl, S5_CTX_CHUNKS, stride=S5_CHUNK), :] = piece[r0:r0 + S5_CTX_CHUNKS]
            lat_ref[pl.ds(b * SEQ + tl, SEQ // S5_CHUNK, stride=S5_CHUNK), :] = (
                piece[r0 + S5_CTX_CHUNKS:r0 + S5_BROWS])


def s5_unpack(y_packed, perm_t):
    w = S5_PACK * S5_CW
    return pl.pallas_call(
        _s5_unpack_kernel,
        grid=(S5_PACKS, BATCH // S5_SB),
        in_specs=[pl.BlockSpec((None, S5_SB * S5_BROWS, w), lambda k, b: (k, b, 0)),
                  pl.BlockSpec((w, w), lambda k, b: (0, 0))],
        out_specs=[pl.BlockSpec((S5_SB * SEQ, HEAD_DIM), lambda k, b: (b, k)),
                   pl.BlockSpec((S5_SB * CTX_LEN, HEAD_DIM), lambda k, b: (b, k))],
        out_shape=[jax.ShapeDtypeStruct((T_LAT, D_MODEL), F32), jax.ShapeDtypeStruct((T_CTX, D_MODEL), F32)],
        compiler_params=_cp("arbitrary", "arbitrary"),
        name="s5_unpack",
    )(y_packed, perm_t)


def _glu_kernel(u_ref, y_ref, d_ref, wa_ref, wg_ref, x_ref, gate_ref, o_ref, h_ref):
    @pl.when(pl.program_id(1) == 0)
    def _():
        y = u_ref[...] * d_ref[...] + y_ref[...]
        h_ref[...] = jax.nn.gelu(y).astype(BF16)

    h = h_ref[...]
    za = jnp.dot(h, wa_ref[...].astype(BF16), preferred_element_type=F32)
    zg = jnp.dot(h, wg_ref[...].astype(BF16), preferred_element_type=F32)
    o_ref[...] = x_ref[...] + gate_ref[...] * (za * jax.nn.sigmoid(zg))


def glu_residual(u, y, d_all, w_all, wi, x, mod, layer, gate_col, tm=512, tn=512):
    t = u.shape[0]
    nj = D_MODEL // tn
    row = pl.BlockSpec((tm, D_MODEL), lambda i, j: (i, 0))
    return pl.pallas_call(
        _glu_kernel,
        grid=(t // tm, nj),
        in_specs=[row, row,
                  pl.BlockSpec((None, 1, D_MODEL), lambda i, j: (wi, 0, 0)),
                  pl.BlockSpec((None, D_MODEL, tn), lambda i, j: (wi, 0, j)),
                  pl.BlockSpec((None, D_MODEL, tn), lambda i, j: (wi, 0, nj + j)),
                  pl.BlockSpec((tm, tn), lambda i, j: (i, j)),
                  _gate_spec(layer, gate_col, tm, tn)],
        out_specs=pl.BlockSpec((tm, tn), lambda i, j: (i, j)),
        out_shape=jax.ShapeDtypeStruct((t, D_MODEL), F32),
        scratch_shapes=[pltpu.VMEM((tm, D_MODEL), BF16)],
        compiler_params=_cp("arbitrary", "arbitrary"),
        name="glu_residual",
    )(u, y, d_all.reshape(-1, 1, D_MODEL), w_all, w_all, x, mod)


ROUTE_LANES = 128


def _route_kernel(x_ref, g_ref, sh_ref, sc_ref, wr_ref, br_ref, h_ref, r_ref, cnt_ref, carry_ref):
    @pl.when(pl.program_id(0) == 0)
    def _():
        carry_ref[...] = jnp.zeros_like(carry_ref)

    h = _norm_mod(x_ref[...], g_ref[...], sh_ref[...], sc_ref[...])
    bits = lax.bitcast_convert_type(h.astype(BF16).astype(F32), jnp.uint32)
    half = h.shape[1] // 2
    h_ref[...] = (bits[:, :half] >> 16) | (bits[:, half:] & jnp.uint32(0xFFFF0000))
    lg = jnp.dot(h, wr_ref[...], precision=lax.Precision.HIGHEST, preferred_element_type=F32) + br_ref[...]
    lane = lax.broadcasted_iota(jnp.int32, lg.shape, 1)
    ninf = -jnp.inf
    coarse = lane < MOE_GROUPS
    lc = jnp.where(coarse, lg, ninf)
    mc = jnp.max(lc, axis=-1, keepdims=True)
    g_sel = jnp.min(jnp.where(lc == mc, lane, ROUTE_LANES), axis=-1, keepdims=True)
    p_sel = 1.0 / jnp.sum(jnp.where(coarse, jnp.exp(lc - mc), 0.0), axis=-1, keepdims=True)
    lo = MOE_GROUPS + MOE_EXPERTS_PER_GROUP * g_sel
    lf = jnp.where((lane >= lo) & (lane < lo + MOE_EXPERTS_PER_GROUP), lg, ninf)
    v0 = jnp.max(lf, axis=-1, keepdims=True)
    i0 = jnp.min(jnp.where(lf == v0, lane, ROUTE_LANES), axis=-1, keepdims=True)
    lf2 = jnp.where(lane == i0, ninf, lf)
    v1 = jnp.max(lf2, axis=-1, keepdims=True)
    i1 = jnp.min(jnp.where(lf2 == v1, lane, ROUTE_LANES), axis=-1, keepdims=True)
    e1 = jnp.exp(v1 - v0)
    w0 = p_sel / (1.0 + e1)
    w1 = w0 * e1
    tm = lg.shape[0]
    lower = (lax.broadcasted_iota(jnp.int32, (tm, tm), 1)
             < lax.broadcasted_iota(jnp.int32, (tm, tm), 0)).astype(BF16)
    carry = carry_ref[...]
    hot0 = (lane == i0).astype(F32)
    before0 = carry + jnp.dot(lower, hot0.astype(BF16), preferred_element_type=F32)
    rank0 = jnp.sum(hot0 * before0, axis=-1, keepdims=True)
    carry = carry + jnp.sum(hot0, axis=0, keepdims=True)
    hot1 = (lane == i1).astype(F32)
    before1 = carry + jnp.dot(lower, hot1.astype(BF16), preferred_element_type=F32)
    rank1 = jnp.sum(hot1 * before1, axis=-1, keepdims=True)
    carry = carry + jnp.sum(hot1, axis=0, keepdims=True)
    carry_ref[...] = carry
    cnt_ref[...] = carry
    cols = [(i0 - MOE_GROUPS).astype(F32), (i1 - MOE_GROUPS).astype(F32), w0, w1, rank0, rank1]
    r = jnp.zeros_like(lg)
    for c, val in enumerate(cols):
        r = jnp.where(lane == c, val, r)
    r_ref[...] = r


ROUTE_E0, ROUTE_E1, ROUTE_W0, ROUTE_W1, ROUTE_R0, ROUTE_R1 = range(6)


def moe_route(x, g_all, mod, layer, sh_col, sc_col, w_route, b_route, tm=512):
    t = x.shape[0]
    mspec = lambda col: pl.BlockSpec((None, 1, D_MODEL),
                                     lambda i: (layer * MOD_ROWS + _mod_row(i, tm), 0, col))
    return pl.pallas_call(
        _route_kernel,
        grid=(t // tm,),
        in_specs=[pl.BlockSpec((tm, D_MODEL), lambda i: (i, 0)),
                  pl.BlockSpec((None, 1, D_MODEL), lambda i: (layer, 0, 0)),
                  mspec(sh_col), mspec(sc_col),
                  pl.BlockSpec((D_MODEL, ROUTE_LANES), lambda i: (0, 0)),
                  pl.BlockSpec((1, ROUTE_LANES), lambda i: (0, 0))],
        out_specs=[pl.BlockSpec((tm, D_MODEL // 2), lambda i: (i, 0)),
                   pl.BlockSpec((tm, ROUTE_LANES), lambda i: (i, 0)),
                   pl.BlockSpec((1, ROUTE_LANES), lambda i: (0, 0))],
        out_shape=[jax.ShapeDtypeStruct((t, D_MODEL // 2), jnp.uint32),
                   jax.ShapeDtypeStruct((t, ROUTE_LANES), F32),
                   jax.ShapeDtypeStruct((1, ROUTE_LANES), F32)],
        scratch_shapes=[pltpu.VMEM((1, ROUTE_LANES), F32)],
        compiler_params=_cp("arbitrary"),
        name="moe_route",
    )(x, g_all.reshape(DEPTH, 1, D_MODEL), mod, mod, w_route, b_route)


def moe_plan(route, counts):
    cnt = counts[0, MOE_GROUPS:MOE_GROUPS + MOE_EXPERTS].astype(jnp.int32)
    padded = ((cnt + MOE_TILE - 1) // MOE_TILE) * MOE_TILE
    pad_end = jnp.cumsum(padded)
    pad_off = pad_end - padded
    e = route[:, ROUTE_E0:ROUTE_E1 + 1].astype(jnp.int32)
    rank = route[:, ROUTE_R0:ROUTE_R1 + 1].astype(jnp.int32)
    hot = e[:, :, None] == jnp.arange(MOE_EXPERTS, dtype=jnp.int32)[None, None, :]
    dest = jnp.sum(jnp.where(hot, pad_off[None, None, :], 0), axis=-1) + rank
    n_used = (pad_end[-1] // MOE_TILE).astype(jnp.int32).reshape(1)
    tile_start = jnp.arange(MOE_NT, dtype=jnp.int32) * MOE_TILE
    tile_expert = jnp.sum((pad_end[None, :] <= tile_start[:, None]).astype(jnp.int32), axis=1)
    tile_expert = jnp.minimum(tile_expert, MOE_EXPERTS - 1).astype(jnp.int32)
    tok = jnp.zeros((MOE_NT * MOE_TILE,), jnp.int32).at[dest.reshape(-1)].set(
        jnp.arange(dest.size, dtype=jnp.int32) // 2)
    return dest, tok, tile_expert, n_used


def _dispatch_kernel(nu_ref, tok_ref, tokn_ref, h_hbm, o_ref, buf, sem):
    i = pl.program_id(0)
    nu = nu_ref[0]

    def row_copy(t, slot, r):
        return pltpu.make_async_copy(h_hbm.at[pl.ds(t, 1), :], buf.at[slot, pl.ds(r, 1), :], sem.at[slot])

    def issue(tref, slot):
        def body(r, c):
            row_copy(tref[0, 0, r], slot, r).start()
            return c
        lax.fori_loop(0, MOE_TILE, body, 0, unroll=8)

    @pl.when(i == 0)
    def _():
        issue(tok_ref, 0)

    @pl.when(i + 1 < nu)
    def _():
        issue(tokn_ref, (i + 1) % 2)

    @pl.when(i < nu)
    def _():
        slot = i % 2

        def drain(r, c):
            row_copy(0, slot, r).wait()
            return c
        lax.fori_loop(0, MOE_TILE, drain, 0, unroll=8)
        o_ref[...] = buf[slot]

    @pl.when(i >= nu)
    def _():
        o_ref[...] = jnp.zeros_like(o_ref)


def moe_dispatch(h, tok, n_used):
    smem_tile = lambda off: pl.BlockSpec(
        (1, 1, MOE_TILE), lambda i, nu: (jnp.minimum(i + off, MOE_NT - 1), 0, 0), memory_space=pltpu.SMEM)
    grid_spec = pltpu.PrefetchScalarGridSpec(
        num_scalar_prefetch=1,
        grid=(MOE_NT,),
        in_specs=[smem_tile(0), smem_tile(1), pl.BlockSpec(memory_space=pl.ANY)],
        out_specs=pl.BlockSpec((MOE_TILE, h.shape[1]), lambda i, nu: (i, 0)),
        scratch_shapes=[pltpu.VMEM((2, MOE_TILE, h.shape[1]), h.dtype), pltpu.SemaphoreType.DMA((2,))])
    tok3 = tok.reshape(MOE_NT, 1, MOE_TILE)
    return pl.pallas_call(
        _dispatch_kernel,
        grid_spec=grid_spec,
        out_shape=jax.ShapeDtypeStruct((MOE_NT * MOE_TILE, h.shape[1]), h.dtype),
        compiler_params=_cp("arbitrary"),
        name="moe_dispatch",
    )(n_used, tok3, tok3, h)


def _expert_kernel(te_ref, nu_ref, hs_ref, wg_ref, wu_ref, wd_ref, o_ref, wgb, wub, wdb):
    i = pl.program_id(0)

    @pl.when(i < nu_ref[0])
    def _():
        @pl.when((i == 0) | (te_ref[i] != te_ref[jnp.maximum(i - 1, 0)]))
        def _():
            wgb[...] = wg_ref[...].astype(BF16)
            wub[...] = wu_ref[...].astype(BF16)
            wdb[...] = wd_ref[...].astype(BF16)

        w = hs_ref[...]
        lo = lax.bitcast_convert_type(w << 16, F32).astype(BF16)
        hi = lax.bitcast_convert_type(w & jnp.uint32(0xFFFF0000), F32).astype(BF16)
        h = jnp.concatenate([lo, hi], axis=1)
        g = jnp.dot(h, wgb[...], preferred_element_type=F32)
        u = jnp.dot(h, wub[...], preferred_element_type=F32)
        hid = (jax.nn.silu(g) * u).astype(BF16)
        o_ref[...] = jnp.dot(hid, wdb[...], preferred_element_type=F32)

    @pl.when(i >= nu_ref[0])
    def _():
        o_ref[...] = jnp.zeros_like(o_ref)


def moe_experts(hs, tile_expert, n_used, w_gate, w_up, w_down, layer):
    def wspec(shape):
        return pl.BlockSpec((None, None, None) + shape,
                            lambda i, te, nu: (layer, te[i] // MOE_EXPERTS_PER_GROUP,
                                               te[i] % MOE_EXPERTS_PER_GROUP, 0, 0))

    grid_spec = pltpu.PrefetchScalarGridSpec(
        num_scalar_prefetch=2,
        grid=(MOE_NT,),
        in_specs=[pl.BlockSpec((MOE_TILE, D_MODEL // 2), lambda i, te, nu: (jnp.minimum(i, nu[0] - 1), 0)),
                  wspec((D_MODEL, MOE_FFN)), wspec((D_MODEL, MOE_FFN)), wspec((MOE_FFN, D_MODEL))],
        out_specs=pl.BlockSpec((MOE_TILE, D_MODEL), lambda i, te, nu: (i, 0)),
        scratch_shapes=[pltpu.VMEM((D_MODEL, MOE_FFN), BF16),
                        pltpu.VMEM((D_MODEL, MOE_FFN), BF16),
                        pltpu.VMEM((MOE_FFN, D_MODEL), BF16)])
    return pl.pallas_call(
        _expert_kernel,
        grid_spec=grid_spec,
        out_shape=jax.ShapeDtypeStruct((MOE_NT * MOE_TILE, D_MODEL), F32),
        compiler_params=_cp("arbitrary"),
        name="moe_experts",
    )(tile_expert, n_used, hs, w_gate, w_up, w_down)


def _combine_kernel(dest_ref, destn_ref, ys_hbm, r_ref, x_ref, gate_ref, o_ref, buf, sem):
    i = pl.program_id(0)
    n = pl.num_programs(0)
    tm = x_ref.shape[0]

    def row_copy(d, slot, k, r):
        return pltpu.make_async_copy(ys_hbm.at[pl.ds(d, 1), :], buf.at[slot, k, pl.ds(r, 1), :], sem.at[slot])

    def issue(dref, slot):
        def body(r, c):
            row_copy(dref[0, 0, 2 * r], slot, 0, r).start()
            row_copy(dref[0, 0, 2 * r + 1], slot, 1, r).start()
            return c
        lax.fori_loop(0, tm, body, 0, unroll=8)

    @pl.when(i == 0)
    def _():
        issue(dest_ref, 0)

    @pl.when(i + 1 < n)
    def _():
        issue(destn_ref, (i + 1) % 2)

    slot = i % 2

    def drain(r, c):
        row_copy(0, slot, 0, r).wait()
        row_copy(0, slot, 1, r).wait()
        return c
    lax.fori_loop(0, tm, drain, 0, unroll=8)
    r = r_ref[...]
    y = r[:, ROUTE_W0:ROUTE_W0 + 1] * buf[slot, 0] + r[:, ROUTE_W1:ROUTE_W1 + 1] * buf[slot, 1]
    o_ref[...] = x_ref[...] + gate_ref[...] * y


def moe_combine(ys, dest, route, x, mod, layer, gate_col, tm=256):
    t = x.shape[0]
    row = pl.BlockSpec((tm, D_MODEL), lambda i: (i, 0))
    nblk = t // tm
    smem = lambda off: pl.BlockSpec((1, 1, 2 * tm), lambda i: (jnp.minimum(i + off, nblk - 1), 0, 0),
                                    memory_space=pltpu.SMEM)
    dest3 = dest.reshape(nblk, 1, 2 * tm)
    return pl.pallas_call(
        _combine_kernel,
        grid=(nblk,),
        in_specs=[smem(0), smem(1),
                  pl.BlockSpec(memory_space=pl.ANY),
                  pl.BlockSpec((tm, ROUTE_LANES), lambda i: (i, 0)),
                  row,
                  pl.BlockSpec((None, 1, D_MODEL),
                               lambda i: (layer * MOD_ROWS + _mod_row(i, tm), 0, gate_col))],
        out_specs=row,
        out_shape=jax.ShapeDtypeStruct((t, D_MODEL), F32),
        scratch_shapes=[pltpu.VMEM((2, 2, tm, D_MODEL), F32), pltpu.SemaphoreType.DMA((2,))],
        compiler_params=_cp("arbitrary"),
        name="moe_combine",
    )(dest3, dest3, ys, route, x, mod)


def _final_norm_kernel(x_ref, g_ref, o_ref):
    x = x_ref[...]
    o_ref[...] = x * lax.rsqrt(jnp.mean(x * x, axis=-1, keepdims=True) + RMS_EPS) * g_ref[...]


def final_norm(x, g, rows, tm=512):
    row = pl.BlockSpec((tm, D_MODEL), lambda i: (i, 0))
    return pl.pallas_call(
        _final_norm_kernel,
        grid=(rows // tm,),
        in_specs=[row, pl.BlockSpec((1, D_MODEL), lambda i: (0, 0))],
        out_specs=row,
        out_shape=jax.ShapeDtypeStruct((rows, D_MODEL), F32),
        compiler_params=_cp("arbitrary"),
        name="final_norm",
    )(x, g.reshape(1, D_MODEL))


def attention_layer(xs, mod, layer, norm1_g, w_in, w_out, rpb, q_gain, k_gain, cos_t, sin_t):
    li = layer // 2
    p = norm_mod_matmul(xs, norm1_g, mod, w_in, layer, li, 0, 1)
    gains = jnp.stack([q_gain[li], k_gain[li]]).reshape(2, 1, HEAD_DIM)
    qkv = gqa_prep(p, gains, cos_t, sin_t)
    kcol, vcol = GQA_Q_HEADS, GQA_Q_HEADS + GQA_KV_HEADS
    grp = GQA_Q_HEADS // GQA_KV_HEADS
    ctx0 = T_LAT // CTX_LEN

    oa = na_attention(p, na_bias_table(rpb[li]))
    tq = 256
    nq = SEQ // tq
    ob = attention(
        qkv, lambda b, k, q: (b * nq + q, k),
        [qkv, qkv],
        [lambda b, k, q: (b, kcol + k), lambda b, k, q: (ctx0 + b, kcol + k)],
        [lambda b, k, q: (b, vcol + k), lambda b, k, q: (ctx0 + b, vcol + k)],
        [SEQ, CTX_LEN], (BATCH, GQA_KV_HEADS, nq), tq, grp, 1.0,
        T_LAT, GQA_Q_HEADS * HEAD_DIM, lambda b, k, q: (b * nq + q, k), "gqa_latent")
    oac = attention(
        p, lambda b, h: (ctx0 + b, h),
        [p],
        [lambda b, h: (ctx0 + b, NA_HEADS + h)],
        [lambda b, h: (ctx0 + b, 2 * NA_HEADS + h)],
        [CTX_LEN], (BATCH, NA_HEADS), CTX_LEN, 1, ATT_SCALE * LOG2E,
        T_CTX, NA_HEADS * HEAD_DIM, lambda b, h: (b, h), "na_context")
    obc = attention(
        qkv, lambda b, k: (ctx0 + b, k),
        [qkv],
        [lambda b, k: (ctx0 + b, kcol + k)],
        [lambda b, k: (ctx0 + b, vcol + k)],
        [CTX_LEN], (BATCH, GQA_KV_HEADS), CTX_LEN, grp, 1.0,
        T_CTX, GQA_Q_HEADS * HEAD_DIM, lambda b, k: (b, k), "gqa_context")
    oa_all = jnp.concatenate([oa, oac], axis=0)
    ob_all = jnp.concatenate([ob, obc], axis=0)
    return proj_residual(oa_all, ob_all, w_out, li, xs, mod, layer, 2)


def s5_layer(xs, mod, layer, norm1_g, w_in, lam_re, lam_im, log_dt, b_re, b_im, c_re, c_im, d_skip, w_glu):
    li = layer // 2
    u = norm_mod_matmul(xs, norm1_g, mod, w_in, layer, li, 0, 1)
    wst, wy, a1, a2 = s5_weights(lam_re[li], lam_im[li], log_dt[li], b_re[li], b_im[li], c_re[li], c_im[li])
    perm = s5_lane_permutation()
    y_lat, y_ctx = s5_unpack(s5_core(s5_pack(u, perm), wst, wy, a1, a2), perm.T)
    y = jnp.concatenate([y_lat, y_ctx], axis=0)
    return glu_residual(u, y, d_skip, w_glu, li, xs, mod, layer, 2)


def moe_layer(xs, mod, layer, norm2_g, w_coarse, b_coarse, w_fine, b_fine, w_gate, w_up, w_down):
    wf = w_fine[layer].transpose(1, 0, 2).reshape(D_MODEL, MOE_EXPERTS)
    w_route = jnp.concatenate([w_coarse[layer], wf], axis=1).astype(F32)
    w_route = jnp.pad(w_route, ((0, 0), (0, ROUTE_LANES - w_route.shape[1])))
    b_route = jnp.concatenate([b_coarse[layer], b_fine[layer].reshape(-1)]).astype(F32)
    b_route = jnp.pad(b_route, (0, ROUTE_LANES - b_route.shape[0])).reshape(1, ROUTE_LANES)
    h, route, counts = moe_route(xs, norm2_g, mod, layer, 3, 4, w_route, b_route)
    dest, tok, tile_expert, n_used = moe_plan(route, counts)
    hs = moe_dispatch(h, tok, n_used)
    ys = moe_experts(hs, tile_expert, n_used, w_gate, w_up, w_down, layer)
    return moe_combine(ys, dest, route, xs, mod, layer, 5)


def kernel(x, c, ctx, c_ctx, ada_w, ada_b, norm1_g, norm2_g, final_g, attn_w_in, attn_w_out, na_rpb, q_gain, k_gain, s5_w_in, s5_lam_re, s5_lam_im, s5_log_dt, s5_b_re, s5_b_im, s5_c_re, s5_c_im, s5_d, s5_w_glu, moe_w_coarse, moe_b_coarse, moe_w_fine, moe_b_fine, moe_w_gate, moe_w_up, moe_w_down):
    xs = jnp.concatenate([x.reshape(T_LAT, D_MODEL), ctx.reshape(T_CTX, D_MODEL)], axis=0)
    c8 = jnp.concatenate([c, c_ctx[None, :], jnp.zeros((MOD_ROWS - BATCH - 1, D_MODEL), F32)], axis=0)
    mod = ada_mod(c8, ada_w, ada_b).reshape(DEPTH * MOD_ROWS, 1, 6 * D_MODEL)
    cos_t, sin_t = rope_tables()
    for layer in range(DEPTH):
        if layer % 2 == 0:
            xs = attention_layer(xs, mod, layer, norm1_g, attn_w_in, attn_w_out, na_rpb, q_gain, k_gain,
                                 cos_t, sin_t)
        else:
            xs = s5_layer(xs, mod, layer, norm1_g, s5_w_in, s5_lam_re, s5_lam_im, s5_log_dt,
                          s5_b_re, s5_b_im, s5_c_re, s5_c_im, s5_d, s5_w_glu)
        xs = moe_layer(xs, mod, layer, norm2_g, moe_w_coarse, moe_b_coarse, moe_w_fine, moe_b_fine,
                       moe_w_gate, moe_w_up, moe_w_down)
    return final_norm(xs, final_g, T_LAT).reshape(BATCH, SEQ, D_MODEL)
```

```python
import functools
import math

import jax
import jax.numpy as jnp
from jax import lax
from jax.experimental import pallas as pl
from jax.experimental.pallas import tpu as pltpu

F32 = jnp.float32
BF16 = jnp.bfloat16

D_MODEL = 2048
BATCH = 4
SEQ = 4096
DEPTH = 4
GRID_W = 64
CTX_LEN = 256
HEAD_DIM = 128
NA_HEADS = 8
NA_ROWS = 8
NA_COLS = 16
GQA_Q_HEADS = 8
GQA_KV_HEADS = 2
ROPE_THETA = 10000.0
ROPE_AXIS_DIM = HEAD_DIM // 2
ATTN_IN_COLS = (3 * NA_HEADS + GQA_Q_HEADS + 2 * GQA_KV_HEADS) * HEAD_DIM
S5_GROUP = 16
S5_GROUPS = D_MODEL // S5_GROUP
S5_STATE = 64
MOE_GROUPS = 4
MOE_EXPERTS_PER_GROUP = 8
MOE_EXPERTS = MOE_GROUPS * MOE_EXPERTS_PER_GROUP
MOE_FFN = D_MODEL // 4
RMS_EPS = 1e-6
NEG_INF = -1e30

T_LAT = BATCH * SEQ
T_CTX = BATCH * CTX_LEN
T_ALL = T_LAT + T_CTX
MOD_ROWS = 8
ATT_SCALE = HEAD_DIM ** -0.5

S5_CHUNK = 16
S5_SEQ = CTX_LEN + SEQ
S5_NCHUNK = S5_SEQ // S5_CHUNK
S5_CW = S5_CHUNK * S5_GROUP
S5_GB = 4

MOE_TILE = 512
MOE_SLOTS = T_ALL * 2
MOE_NT = MOE_SLOTS // MOE_TILE + MOE_EXPERTS

VMEM_LIMIT = 56 * 1024 * 1024


def _cp(*sem):
    return pltpu.CompilerParams(dimension_semantics=sem, vmem_limit_bytes=VMEM_LIMIT)


def _mod_row(i, tm):
    return jnp.minimum((i * tm) // SEQ, BATCH)


def _ada_kernel(c_ref, w_ref, b_ref, o_ref):
    s = jax.nn.silu(c_ref[...])
    o_ref[...] = jnp.dot(s, w_ref[...], precision=lax.Precision.HIGHEST,
                         preferred_element_type=F32) + b_ref[...]


def ada_mod(c8, ada_w, ada_b):
    tn = 1024
    n = ada_w.shape[-1]
    return pl.pallas_call(
        _ada_kernel,
        grid=(DEPTH, n // tn),
        in_specs=[pl.BlockSpec((MOD_ROWS, D_MODEL), lambda l, j: (0, 0)),
                  pl.BlockSpec((None, D_MODEL, tn), lambda l, j: (l, 0, j)),
                  pl.BlockSpec((None, 1, tn), lambda l, j: (l, 0, j))],
        out_specs=pl.BlockSpec((None, MOD_ROWS, tn), lambda l, j: (l, 0, j)),
        out_shape=jax.ShapeDtypeStruct((DEPTH, MOD_ROWS, n), F32),
        compiler_params=_cp("arbitrary", "arbitrary"),
        name="ada_mod",
    )(c8, ada_w, ada_b.reshape(DEPTH, 1, n))


def _norm_mod(x, g, sh, sc):
    y = x * lax.rsqrt(jnp.mean(x * x, axis=-1, keepdims=True) + RMS_EPS) * g
    return y * (1.0 + sc) + sh


def _mod_spec(layer, col, tm):
    return pl.BlockSpec((None, 1, D_MODEL),
                        lambda i, j: (layer * MOD_ROWS + _mod_row(i, tm), 0, col))


def _gate_spec(layer, col, tm, tn):
    return pl.BlockSpec((None, 1, tn),
                        lambda i, j: (layer * MOD_ROWS + _mod_row(i, tm), 0, col * (D_MODEL // tn) + j))


def _nm_mm_kernel(x_ref, g_ref, sh_ref, sc_ref, w_ref, o_ref, h_ref):
    @pl.when(pl.program_id(1) == 0)
    def _():
        h_ref[...] = _norm_mod(x_ref[...], g_ref[...], sh_ref[...], sc_ref[...]).astype(BF16)

    o_ref[...] = jnp.dot(h_ref[...], w_ref[...],
                         preferred_element_type=F32).astype(o_ref.dtype)


def norm_mod_matmul(x, g_all, mod, w_all, layer, wi, sh_col, sc_col, tm=1024, tn=512):
    t = x.shape[0]
    n = w_all.shape[-1]
    return pl.pallas_call(
        _nm_mm_kernel,
        grid=(t // tm, n // tn),
        in_specs=[pl.BlockSpec((tm, D_MODEL), lambda i, j: (i, 0)),
                  pl.BlockSpec((None, 1, D_MODEL), lambda i, j: (layer, 0, 0)),
                  _mod_spec(layer, sh_col, tm),
                  _mod_spec(layer, sc_col, tm),
                  pl.BlockSpec((None, D_MODEL, tn), lambda i, j: (wi, 0, j))],
        out_specs=pl.BlockSpec((tm, tn), lambda i, j: (i, j)),
        out_shape=jax.ShapeDtypeStruct((t, n), F32),
        scratch_shapes=[pltpu.VMEM((tm, D_MODEL), BF16)],
        compiler_params=_cp("arbitrary", "arbitrary"),
        name="norm_mod_matmul",
    )(x, g_all.reshape(DEPTH, 1, D_MODEL), mod, mod, w_all)


def _proj_res_kernel(a_ref, b_ref, wa_ref, wb_ref, x_ref, gate_ref, o_ref):
    y = jnp.dot(a_ref[...], wa_ref[...], preferred_element_type=F32)
    y = y + jnp.dot(b_ref[...], wb_ref[...], preferred_element_type=F32)
    o_ref[...] = x_ref[...] + gate_ref[...] * y


def proj_residual(oa, ob, w_all, wi, x, mod, layer, gate_col, tm=1024, tn=512):
    t, ka = oa.shape
    kb = ob.shape[1]
    assert ka == kb
    return pl.pallas_call(
        _proj_res_kernel,
        grid=(t // tm, D_MODEL // tn),
        in_specs=[pl.BlockSpec((tm, ka), lambda i, j: (i, 0)),
                  pl.BlockSpec((tm, kb), lambda i, j: (i, 0)),
                  pl.BlockSpec((None, ka, tn), lambda i, j: (wi, 0, j)),
                  pl.BlockSpec((None, kb, tn), lambda i, j: (wi, 1, j)),
                  pl.BlockSpec((tm, tn), lambda i, j: (i, j)),
                  _gate_spec(layer, gate_col, tm, tn)],
        out_specs=pl.BlockSpec((tm, tn), lambda i, j: (i, j)),
        out_shape=jax.ShapeDtypeStruct((t, D_MODEL), F32),
        compiler_params=_cp("arbitrary", "arbitrary"),
        name="proj_residual",
    )(oa, ob, w_all, w_all, x, mod)


def _gqa_prep_kernel(p_ref, gain_ref, cos_ref, sin_ref, o_ref):
    h = pl.program_id(1)
    nqk = GQA_Q_HEADS + GQA_KV_HEADS

    @pl.when(h < nqk)
    def _():
        x = p_ref[...]
        scale = jnp.where(h < GQA_Q_HEADS, ATT_SCALE * LOG2E, 1.0).astype(F32)
        y = x * lax.rsqrt(jnp.mean(x * x, axis=-1, keepdims=True) + RMS_EPS) * (gain_ref[...] * scale)
        lane = lax.broadcasted_iota(jnp.int32, y.shape, 1)
        half = ROPE_AXIS_DIM // 2
        partner = jnp.where((lane % ROPE_AXIS_DIM) < half,
                            pltpu.roll(y, HEAD_DIM - half, 1), pltpu.roll(y, half, 1))
        o_ref[...] = (y * cos_ref[...] + partner * sin_ref[...]).astype(BF16)

    @pl.when(h >= nqk)
    def _():
        o_ref[...] = p_ref[...].astype(BF16)


def gqa_prep(p, gains, cos_t, sin_t, tm=512):
    t = p.shape[0]
    nh = GQA_Q_HEADS + 2 * GQA_KV_HEADS
    col0 = 3 * NA_HEADS
    lat_tiles = T_LAT // tm
    per_seq = SEQ // tm

    def tab_idx(i, h):
        return (jnp.where(i < lat_tiles, i % per_seq, per_seq), 0)

    return pl.pallas_call(
        _gqa_prep_kernel,
        grid=(t // tm, nh),
        in_specs=[pl.BlockSpec((tm, HEAD_DIM), lambda i, h: (i, col0 + h)),
                  pl.BlockSpec((None, 1, HEAD_DIM),
                               lambda i, h: (jnp.where(h < GQA_Q_HEADS, 0, 1), 0, 0)),
                  pl.BlockSpec((tm, HEAD_DIM), tab_idx),
                  pl.BlockSpec((tm, HEAD_DIM), tab_idx)],
        out_specs=pl.BlockSpec((tm, HEAD_DIM), lambda i, h: (i, h)),
        out_shape=jax.ShapeDtypeStruct((t, nh * HEAD_DIM), BF16),
        compiler_params=_cp("arbitrary", "arbitrary"),
        name="gqa_prep",
    )(p, gains, cos_t, sin_t)


def rope_tables(tm=512):
    t = jnp.arange(SEQ)
    row = (t // GRID_W).astype(F32)
    col = (t % GRID_W).astype(F32)
    inv = ROPE_THETA ** (-jnp.arange(0, ROPE_AXIS_DIM, 2, dtype=F32) / ROPE_AXIS_DIM)
    ar = row[:, None] * inv[None]
    ac = col[:, None] * inv[None]
    cos_t = jnp.concatenate([jnp.cos(ar), jnp.cos(ar), jnp.cos(ac), jnp.cos(ac)], axis=-1)
    sin_t = jnp.concatenate([-jnp.sin(ar), jnp.sin(ar), -jnp.sin(ac), jnp.sin(ac)], axis=-1)
    cos_t = jnp.concatenate([cos_t, jnp.ones((tm, HEAD_DIM), F32)], axis=0)
    sin_t = jnp.concatenate([sin_t, jnp.zeros((tm, HEAD_DIM), F32)], axis=0)
    return cos_t, sin_t


LOG2E = 1.4426950408889634
ATT_KCHUNK = 512


def _softmax_chunk(q, k, v, bias, m, l, acc):
    s = lax.dot_general(q, k.astype(BF16), (((1,), (1,)), ((), ())), preferred_element_type=F32)
    if bias is not None:
        s = s + bias
    m_new = jnp.maximum(m, jnp.max(s, axis=-1, keepdims=True))
    alpha = jnp.exp2(m - m_new)
    p = jnp.exp2(s - m_new)
    l = alpha * l + jnp.sum(p, axis=-1, keepdims=True)
    acc = alpha * acc + jnp.dot(p.astype(BF16), v.astype(BF16), preferred_element_type=F32)
    return m_new, l, acc


def _attn_kernel(*refs, nseg, group, scale):
    q_ref = refs[0]
    k_refs = refs[1:1 + nseg]
    v_refs = refs[1 + nseg:1 + 2 * nseg]
    o_ref = refs[1 + 2 * nseg]
    tq = q_ref.shape[0]
    q = jnp.concatenate([q_ref[:, g * HEAD_DIM:(g + 1) * HEAD_DIM] for g in range(group)], axis=0)
    if scale != 1.0:
        q = q.astype(F32) * scale
    q = q.astype(BF16)
    m = jnp.full((q.shape[0], 1), -jnp.inf, F32)
    l = jnp.zeros((q.shape[0], 1), F32)
    acc = jnp.zeros((q.shape[0], HEAD_DIM), F32)
    for k_ref, v_ref in zip(k_refs, v_refs):
        nk = k_ref.shape[0]
        for c0 in range(0, nk, ATT_KCHUNK):
            cs = min(ATT_KCHUNK, nk - c0)
            m, l, acc = _softmax_chunk(q, k_ref[c0:c0 + cs, :], v_ref[c0:c0 + cs, :], None, m, l, acc)
    o = acc / l
    for g in range(group):
        o_ref[:, g * HEAD_DIM:(g + 1) * HEAD_DIM] = o[g * tq:(g + 1) * tq].astype(o_ref.dtype)


def attention(q_arr, q_idx, kv_arrs, k_idx, v_idx, kv_rows, grid, tq, group, scale, out_rows, out_cols, o_idx, name):
    nseg = len(kv_rows)
    in_specs = [pl.BlockSpec((tq, group * HEAD_DIM), q_idx)]
    in_specs += [pl.BlockSpec((kv_rows[i], HEAD_DIM), k_idx[i]) for i in range(nseg)]
    in_specs += [pl.BlockSpec((kv_rows[i], HEAD_DIM), v_idx[i]) for i in range(nseg)]
    return pl.pallas_call(
        functools.partial(_attn_kernel, nseg=nseg, group=group, scale=scale),
        grid=grid,
        in_specs=in_specs,
        out_specs=pl.BlockSpec((tq, group * HEAD_DIM), o_idx),
        out_shape=jax.ShapeDtypeStruct((out_rows, out_cols), BF16),
        compiler_params=_cp(*(["arbitrary"] * len(grid))),
        name=name,
    )(q_arr, *kv_arrs, *kv_arrs)


NA_QROWS = 8
NA_KROWS = 16
NA_TQ = NA_QROWS * GRID_W
NA_TK = NA_KROWS * GRID_W
GRID_H = SEQ // GRID_W


def _na_kernel(q_ref, k_ref, v_ref, kc_ref, vc_ref, bias_ref, o_ref):
    rb = pl.program_id(2)
    w0 = jnp.clip(NA_QROWS * rb - (NA_KROWS - NA_QROWS) // 2, 0, GRID_H - NA_KROWS) * GRID_W
    w0 = pl.multiple_of(w0, 4 * GRID_W)
    q = (q_ref[...] * (ATT_SCALE * LOG2E)).astype(BF16)
    m = jnp.full((NA_TQ, 1), -jnp.inf, F32)
    l = jnp.zeros((NA_TQ, 1), F32)
    acc = jnp.zeros((NA_TQ, HEAD_DIM), F32)
    m, l, acc = _softmax_chunk(q, kc_ref[...], vc_ref[...], None, m, l, acc)
    for c0 in range(0, NA_TK, ATT_KCHUNK):
        kw = k_ref[pl.ds(w0 + c0, ATT_KCHUNK), :]
        vw = v_ref[pl.ds(w0 + c0, ATT_KCHUNK), :]
        m, l, acc = _softmax_chunk(q, kw, vw, bias_ref[:, c0:c0 + ATT_KCHUNK], m, l, acc)
    o_ref[...] = (acc / l).astype(o_ref.dtype)


def na_bias_table(rpb):
    aw = jnp.arange(GRID_W)
    col_start = jnp.clip(aw - NA_COLS // 2, 0, GRID_W - NA_COLS)
    col_ok = (aw[None, :] >= col_start[:, None]) & (aw[None, :] < col_start[:, None] + NA_COLS)
    off_c = jnp.clip(aw[None, :] - aw[:, None] + (NA_COLS - 1), 0, 2 * NA_COLS - 2)
    nh, nr, nc = rpb.shape
    pick = (off_c.reshape(1, -1) == jnp.arange(nc)[:, None]).astype(F32)
    tiles = jnp.dot(rpb.astype(F32).reshape(nh * nr, nc), pick, precision=lax.Precision.HIGHEST)
    tiles = jnp.where(col_ok.reshape(1, -1), tiles * LOG2E, NEG_INF).reshape(nh, nr, GRID_W, GRID_W)
    blocked = jnp.full((nh, 1, GRID_W, GRID_W), NEG_INF, F32)
    tiles = jnp.concatenate([tiles, blocked], axis=1)
    nblk = GRID_H // NA_QROWS
    out = []
    for rb in (0, nblk // 2, nblk - 1):
        w0 = min(max(NA_QROWS * rb - (NA_KROWS - NA_QROWS) // 2, 0), GRID_H - NA_KROWS)
        picks = []
        for qr in range(NA_QROWS * rb, NA_QROWS * (rb + 1)):
            rs = min(max(qr - NA_ROWS // 2, 0), GRID_H - NA_ROWS)
            for kr in range(w0, w0 + NA_KROWS):
                picks.append(kr - qr + NA_ROWS - 1 if rs <= kr < rs + NA_ROWS else nr)
        b = jnp.stack([tiles[:, k] for k in picks], axis=1)
        b = b.reshape(nh, NA_QROWS, NA_KROWS, GRID_W, GRID_W).transpose(0, 1, 3, 2, 4)
        out.append(b.reshape(nh, NA_TQ, NA_TK))
    return jnp.stack(out, axis=1)


def na_attention(p, bias):
    nblk = GRID_H // NA_QROWS
    ctx_blk0 = T_LAT // CTX_LEN

    def pat(rb):
        return jnp.where(rb == 0, 0, jnp.where(rb == nblk - 1, 2, 1))

    return pl.pallas_call(
        _na_kernel,
        grid=(NA_HEADS, BATCH, nblk),
        in_specs=[pl.BlockSpec((NA_TQ, HEAD_DIM), lambda h, b, r: (b * nblk + r, h)),
                  pl.BlockSpec((SEQ, HEAD_DIM), lambda h, b, r: (b, NA_HEADS + h)),
                  pl.BlockSpec((SEQ, HEAD_DIM), lambda h, b, r: (b, 2 * NA_HEADS + h)),
                  pl.BlockSpec((CTX_LEN, HEAD_DIM), lambda h, b, r: (ctx_blk0 + b, NA_HEADS + h)),
                  pl.BlockSpec((CTX_LEN, HEAD_DIM), lambda h, b, r: (ctx_blk0 + b, 2 * NA_HEADS + h)),
                  pl.BlockSpec((None, None, NA_TQ, NA_TK), lambda h, b, r: (h, pat(r), 0, 0))],
        out_specs=pl.BlockSpec((NA_TQ, HEAD_DIM), lambda h, b, r: (b * nblk + r, h)),
        out_shape=jax.ShapeDtypeStruct((T_LAT, NA_HEADS * HEAD_DIM), BF16),
        compiler_params=_cp("arbitrary", "arbitrary", "arbitrary"),
        name="na_attention",
    )(p, p, p, p, p, bias)


def s5_weights(lam_re, lam_im, log_dt, b_re, b_im, c_re, c_im):
    hi = lax.Precision.HIGHEST
    L = S5_CHUNK
    lr = jnp.minimum(lam_re.astype(F32), -1e-4)
    li = lam_im.astype(F32)
    dt = jnp.exp(log_dt.astype(F32))[..., None]
    mag = jnp.exp(lr * dt)
    ar = mag * jnp.cos(li * dt)
    ai = mag * jnp.sin(li * dt)
    den = lr * lr + li * li
    xr = ar - 1.0
    fr = (xr * lr + ai * li) / den
    fi = (ai * lr - xr * li) / den
    br = b_re.astype(F32)
    bi = b_im.astype(F32)
    bbr = fr[..., None] * br - fi[..., None] * bi
    bbi = fr[..., None] * bi + fi[..., None] * br
    k = jnp.arange(L + 1, dtype=F32)[:, None, None, None]
    pm = jnp.exp(lr * dt * k)
    pr = pm * jnp.cos(li * dt * k)
    pi = pm * jnp.sin(li * dt * k)
    e_r = pr[..., None] * bbr - pi[..., None] * bbi
    e_i = pr[..., None] * bbi + pi[..., None] * bbr
    cr = c_re.astype(F32)
    ci = c_im.astype(F32)
    kern = (jnp.einsum('dgop,kdgpi->kdgoi', cr, e_r[:L], precision=hi)
            - jnp.einsum('dgop,kdgpi->kdgoi', ci, e_i[:L], precision=hi))
    s_i = jnp.arange(L)
    ng = lr.shape[1]
    lag = s_i[None, :, None] - s_i[:, None, None]
    place_f = (lag == s_i[None, None, :]).astype(F32)
    place = jnp.stack([place_f, place_f[::-1, ::-1]], axis=-1)
    toep = jnp.einsum('stkd,kdgoi->gsito', place, kern, precision=hi)
    w_toep_sum = toep.reshape(ng, S5_CW, S5_CW)
    st_r = e_r[L - 1 - s_i].transpose(1, 2, 0, 4, 3).reshape(2, ng, S5_CW, S5_STATE)
    st_i = e_i[L - 1 - s_i].transpose(1, 2, 0, 4, 3).reshape(2, ng, S5_CW, S5_STATE)
    w_state = jnp.concatenate([st_r, st_i], axis=-1)
    w_state_sw = jnp.concatenate([st_i, st_r], axis=-1)
    qr = pr[1:, :, :, None, :]
    qi = pi[1:, :, :, None, :]
    d_r = cr[None] * qr - ci[None] * qi
    d_i = cr[None] * qi + ci[None] * qr
    wo_r = d_r.transpose(1, 2, 4, 0, 3).reshape(2, ng, S5_STATE, S5_CW)
    wo_i = (-d_i).transpose(1, 2, 4, 0, 3).reshape(2, ng, S5_STATE, S5_CW)
    w_out = jnp.concatenate([wo_r, wo_i], axis=2)
    flip_rows = lambda w: w.reshape(ng, L, S5_GROUP, w.shape[-1])[:, ::-1].reshape(w.shape)
    flip_cols = lambda w: w.reshape(ng, w.shape[1], L, S5_GROUP)[:, :, ::-1].reshape(w.shape)
    wst = jnp.concatenate([w_state[0], flip_rows(w_state[1]),
                           w_state_sw[0], flip_rows(w_state_sw[1])], axis=-1).astype(BF16)
    wy = jnp.concatenate([w_toep_sum, w_out[0], flip_cols(w_out[1])], axis=1).astype(BF16)
    a1 = jnp.concatenate([pr[L], pr[L]], axis=-1)
    a2 = jnp.concatenate([-pi[L], pi[L]], axis=-1)
    per_row = lambda a: jnp.repeat(a.transpose(1, 0, 2), BATCH, axis=1)
    return wst, wy, per_row(a1), per_row(a2)


S5_CTX_CHUNKS = CTX_LEN // S5_CHUNK
S5_PACK = HEAD_DIM // S5_GROUP
S5_PACKS = S5_GROUPS // S5_PACK
S5_BROWS = S5_NCHUNK


def s5_lane_permutation():
    tl, g, hh = jnp.meshgrid(jnp.arange(S5_CHUNK), jnp.arange(S5_PACK), jnp.arange(S5_GROUP), indexing="ij")
    dst = (g * S5_CW + tl * S5_GROUP + hh).reshape(-1)
    return (dst[:, None] == jnp.arange(S5_PACK * S5_CW)[None, :]).astype(BF16)


S5_SB = 2


def _s5_pack_kernel(lat_ref, ctx_ref, p_ref, o_ref):
    cols = []
    for tl in range(S5_CHUNK):
        rows = []
        for b in range(S5_SB):
            rows.append(ctx_ref[pl.ds(b * CTX_LEN + tl, S5_CTX_CHUNKS, stride=S5_CHUNK), :])
            rows.append(lat_ref[pl.ds(b * SEQ + tl, SEQ // S5_CHUNK, stride=S5_CHUNK), :])
        cols.append(jnp.concatenate(rows, axis=0))
    xcat = jnp.concatenate(cols, axis=1).astype(BF16)
    o_ref[...] = jnp.dot(xcat, p_ref[...], preferred_element_type=F32).astype(BF16)


def s5_pack(u, perm):
    ctx0 = T_LAT // (S5_SB * CTX_LEN)
    w = S5_PACK * S5_CW
    return pl.pallas_call(
        _s5_pack_kernel,
        grid=(S5_PACKS, BATCH // S5_SB),
        in_specs=[pl.BlockSpec((S5_SB * SEQ, HEAD_DIM), lambda k, b: (b, k)),
                  pl.BlockSpec((S5_SB * CTX_LEN, HEAD_DIM), lambda k, b: (ctx0 + b, k)),
                  pl.BlockSpec((w, w), lambda k, b: (0, 0))],
        out_specs=pl.BlockSpec((None, S5_SB * S5_BROWS, w), lambda k, b: (k, b, 0)),
        out_shape=jax.ShapeDtypeStruct((S5_PACKS, BATCH * S5_BROWS, w), BF16),
        compiler_params=_cp("arbitrary", "arbitrary"),
        name="s5_pack",
    )(u, u, perm)


def _s5_kernel(u_ref, wst_ref, wy_ref, a1_ref, a2_ref, y_ref, s_ref, ssw_ref, hf_ref, hr_ref):
    gb = wst_ref.shape[0]
    nc = S5_NCHUNK
    sw = 2 * S5_STATE
    for j in range(gb):
        ug = u_ref[:, j * S5_CW:(j + 1) * S5_CW]
        s4 = jnp.dot(ug, wst_ref[j], preferred_element_type=F32)
        for b in range(BATCH):
            rows = s4[b * nc:(b + 1) * nc]
            s_ref[:, j, b, :] = rows[:, 0:sw]
            s_ref[:, j, BATCH + b, :] = rows[:, sw:2 * sw]
            ssw_ref[:, j, b, :] = rows[:, 2 * sw:3 * sw]
            ssw_ref[:, j, BATCH + b, :] = rows[:, 3 * sw:4 * sw]
    a1 = a1_ref[...]
    a2 = a2_ref[...]
    fwd = lax.broadcasted_iota(jnp.int32, (gb, 2 * BATCH, sw), 1) < BATCH

    def step(i, carry):
        h, hs = carry
        ri = jnp.where(i < S5_CTX_CHUNKS, S5_CTX_CHUNKS - 1 - i, nc - 1 + S5_CTX_CHUNKS - i)
        hf_ref[i] = h
        hr_ref[ri] = h
        s = jnp.where(fwd, s_ref[i], s_ref[ri])
        ssw = jnp.where(fwd, ssw_ref[i], ssw_ref[ri])
        return a1 * h + a2 * hs + s, a1 * hs - a2 * h + ssw

    z = jnp.zeros((gb, 2 * BATCH, sw), F32)
    lax.fori_loop(0, nc, step, (z, z), unroll=4)
    for j in range(gb):
        ug = u_ref[:, j * S5_CW:(j + 1) * S5_CW]
        hf = jnp.concatenate([hf_ref[:, j, b, :] for b in range(BATCH)], axis=0)
        hr = jnp.concatenate([hr_ref[:, j, BATCH + b, :] for b in range(BATCH)], axis=0)
        lhs = jnp.concatenate([ug, hf.astype(BF16), hr.astype(BF16)], axis=1)
        y_ref[:, j * S5_CW:(j + 1) * S5_CW] = jnp.dot(lhs, wy_ref[j], preferred_element_type=F32)


def s5_core(u_packed, wst, wy, a1, a2):
    gb = S5_GB
    per_pack = S5_PACK // gb
    rows = u_packed.shape[1]
    sw = 2 * S5_STATE
    blk = lambda *shape: pl.BlockSpec((gb,) + shape, lambda i: (i,) + (0,) * len(shape))
    io = pl.BlockSpec((None, rows, gb * S5_CW), lambda i: (i // per_pack, 0, i % per_pack))
    state = pltpu.VMEM((S5_NCHUNK, gb, 2 * BATCH, sw), F32)
    return pl.pallas_call(
        _s5_kernel,
        grid=(S5_GROUPS // gb,),
        in_specs=[io, blk(S5_CW, 4 * sw), blk(S5_CW + 2 * sw, S5_CW), blk(2 * BATCH, sw), blk(2 * BATCH, sw)],
        out_specs=io,
        out_shape=jax.ShapeDtypeStruct(u_packed.shape, F32),
        scratch_shapes=[state, state, state, state],
        compiler_params=_cp("arbitrary"),
        name="s5_core",
    )(u_packed, wst, wy, a1, a2)


def _s5_unpack_kernel(y_ref, pt_ref, lat_ref, ctx_ref):
    y = y_ref[...]
    hi = y.astype(BF16)
    lo = (y - hi.astype(F32)).astype(BF16)
    yn = (jnp.dot(hi, pt_ref[...], preferred_element_type=F32)
          + jnp.dot(lo, pt_ref[...], preferred_element_type=F32))
    for tl in range(S5_CHUNK):
        piece = yn[:, tl * HEAD_DIM:(tl + 1) * HEAD_DIM]
        for b in range(S5_SB):
            r0 = b * S5_BROWS
            ctx_ref[pl.ds(b * CTX_LEN + tl, S5_CTX_CHUNKS, stride=S5_CHUNK), :] = piece[r0:r0 + S5_CTX_CHUNKS]
            lat_ref[pl.ds(b * SEQ + tl, SEQ // S5_CHUNK, stride=S5_CHUNK), :] = (
                piece[r0 + S5_CTX_CHUNKS:r0 + S5_BROWS])


def s5_unpack(y_packed, perm_t):
    w = S5_PACK * S5_CW
    return pl.pallas_call(
        _s5_unpack_kernel,
        grid=(S5_PACKS, BATCH // S5_SB),
        in_specs=[pl.BlockSpec((None, S5_SB * S5_BROWS, w), lambda k, b: (k, b, 0)),
                  pl.BlockSpec((w, w), lambda k, b: (0, 0))],
        out_specs=[pl.BlockSpec((S5_SB * SEQ, HEAD_DIM), lambda k, b: (b, k)),
                   pl.BlockSpec((S5_SB * CTX_LEN, HEAD_DIM), lambda k, b: (b, k))],
        out_shape=[jax.ShapeDtypeStruct((T_LAT, D_MODEL), F32), jax.ShapeDtypeStruct((T_CTX, D_MODEL), F32)],
        compiler_params=_cp("arbitrary", "arbitrary"),
        name="s5_unpack",
    )(y_packed, perm_t)


def _glu_kernel(u_ref, y_ref, d_ref, wa_ref, wg_ref, x_ref, gate_ref, o_ref, h_ref):
    @pl.when(pl.program_id(1) == 0)
    def _():
        y = u_ref[...] * d_ref[...] + y_ref[...]
        h_ref[...] = jax.nn.gelu(y).astype(BF16)

    h = h_ref[...]
    za = jnp.dot(h, wa_ref[...], preferred_element_type=F32)
    zg = jnp.dot(h, wg_ref[...], preferred_element_type=F32)
    o_ref[...] = x_ref[...] + gate_ref[...] * (za * jax.nn.sigmoid(zg))


def glu_residual(u, y, d_all, w_all, wi, x, mod, layer, gate_col, tm=1024, tn=256):
    t = u.shape[0]
    nj = D_MODEL // tn
    row = pl.BlockSpec((tm, D_MODEL), lambda i, j: (i, 0))
    return pl.pallas_call(
        _glu_kernel,
        grid=(t // tm, nj),
        in_specs=[row, row,
                  pl.BlockSpec((None, 1, D_MODEL), lambda i, j: (wi, 0, 0)),
                  pl.BlockSpec((None, D_MODEL, tn), lambda i, j: (wi, 0, j)),
                  pl.BlockSpec((None, D_MODEL, tn), lambda i, j: (wi, 0, nj + j)),
                  pl.BlockSpec((tm, tn), lambda i, j: (i, j)),
                  _gate_spec(layer, gate_col, tm, tn)],
        out_specs=pl.BlockSpec((tm, tn), lambda i, j: (i, j)),
        out_shape=jax.ShapeDtypeStruct((t, D_MODEL), F32),
        scratch_shapes=[pltpu.VMEM((tm, D_MODEL), BF16)],
        compiler_params=_cp("arbitrary", "arbitrary"),
        name="glu_residual",
    )(u, y, d_all.reshape(-1, 1, D_MODEL), w_all, w_all, x, mod)


ROUTE_LANES = 128
MOE_ROW_TILES = D_MODEL // 2 // HEAD_DIM


def _route_kernel(x_ref, g_ref, sh_ref, sc_ref, wr_ref, br_ref, h_ref, r_ref, cnt_ref, carry_ref):
    @pl.when(pl.program_id(0) == 0)
    def _():
        carry_ref[...] = jnp.zeros_like(carry_ref)

    h = _norm_mod(x_ref[...], g_ref[...], sh_ref[...], sc_ref[...])
    bits = lax.bitcast_convert_type(h.astype(BF16).astype(F32), jnp.uint32)
    half = h.shape[1] // 2
    packed = (bits[:, :half] >> 16) | (bits[:, half:] & jnp.uint32(0xFFFF0000))
    for j in range(half // HEAD_DIM):
        h_ref[:, j, :] = packed[:, j * HEAD_DIM:(j + 1) * HEAD_DIM]
    lg = jnp.dot(h, wr_ref[...], precision=lax.Precision.HIGHEST, preferred_element_type=F32) + br_ref[...]
    lane = lax.broadcasted_iota(jnp.int32, lg.shape, 1)
    ninf = -jnp.inf
    coarse = lane < MOE_GROUPS
    lc = jnp.where(coarse, lg, ninf)
    mc = jnp.max(lc, axis=-1, keepdims=True)
    g_sel = jnp.min(jnp.where(lc == mc, lane, ROUTE_LANES), axis=-1, keepdims=True)
    p_sel = 1.0 / jnp.sum(jnp.where(coarse, jnp.exp(lc - mc), 0.0), axis=-1, keepdims=True)
    lo = MOE_GROUPS + MOE_EXPERTS_PER_GROUP * g_sel
    lf = jnp.where((lane >= lo) & (lane < lo + MOE_EXPERTS_PER_GROUP), lg, ninf)
    v0 = jnp.max(lf, axis=-1, keepdims=True)
    i0 = jnp.min(jnp.where(lf == v0, lane, ROUTE_LANES), axis=-1, keepdims=True)
    lf2 = jnp.where(lane == i0, ninf, lf)
    v1 = jnp.max(lf2, axis=-1, keepdims=True)
    i1 = jnp.min(jnp.where(lf2 == v1, lane, ROUTE_LANES), axis=-1, keepdims=True)
    e1 = jnp.exp(v1 - v0)
    w0 = p_sel / (1.0 + e1)
    w1 = w0 * e1
    tm = lg.shape[0]
    lower = (lax.broadcasted_iota(jnp.int32, (tm, tm), 1)
             < lax.broadcasted_iota(jnp.int32, (tm, tm), 0)).astype(BF16)
    carry = carry_ref[...]
    hot0 = (lane == i0).astype(F32)
    before0 = carry + jnp.dot(lower, hot0.astype(BF16), preferred_element_type=F32)
    rank0 = jnp.sum(hot0 * before0, axis=-1, keepdims=True)
    carry = carry + jnp.sum(hot0, axis=0, keepdims=True)
    hot1 = (lane == i1).astype(F32)
    before1 = carry + jnp.dot(lower, hot1.astype(BF16), preferred_element_type=F32)
    rank1 = jnp.sum(hot1 * before1, axis=-1, keepdims=True)
    carry = carry + jnp.sum(hot1, axis=0, keepdims=True)
    carry_ref[...] = carry
    cnt_ref[...] = carry
    cols = [(i0 - MOE_GROUPS).astype(F32), (i1 - MOE_GROUPS).astype(F32), w0, w1, rank0, rank1]
    r = jnp.zeros_like(lg)
    for c, val in enumerate(cols):
        r = jnp.where(lane == c, val, r)
    r_ref[...] = r


ROUTE_E0, ROUTE_E1, ROUTE_W0, ROUTE_W1, ROUTE_R0, ROUTE_R1 = range(6)


def moe_route(x, g_all, mod, layer, sh_col, sc_col, w_route, b_route, tm=512):
    t = x.shape[0]
    mspec = lambda col: pl.BlockSpec((None, 1, D_MODEL),
                                     lambda i: (layer * MOD_ROWS + _mod_row(i, tm), 0, col))
    return pl.pallas_call(
        _route_kernel,
        grid=(t // tm,),
        in_specs=[pl.BlockSpec((tm, D_MODEL), lambda i: (i, 0)),
                  pl.BlockSpec((None, 1, D_MODEL), lambda i: (layer, 0, 0)),
                  mspec(sh_col), mspec(sc_col),
                  pl.BlockSpec((D_MODEL, ROUTE_LANES), lambda i: (0, 0)),
                  pl.BlockSpec((1, ROUTE_LANES), lambda i: (0, 0))],
        out_specs=[pl.BlockSpec((tm, MOE_ROW_TILES, HEAD_DIM), lambda i: (i, 0, 0)),
                   pl.BlockSpec((tm, ROUTE_LANES), lambda i: (i, 0)),
                   pl.BlockSpec((1, ROUTE_LANES), lambda i: (0, 0))],
        out_shape=[jax.ShapeDtypeStruct((t, MOE_ROW_TILES, HEAD_DIM), jnp.uint32),
                   jax.ShapeDtypeStruct((t, ROUTE_LANES), F32),
                   jax.ShapeDtypeStruct((1, ROUTE_LANES), F32)],
        scratch_shapes=[pltpu.VMEM((1, ROUTE_LANES), F32)],
        compiler_params=_cp("arbitrary"),
        name="moe_route",
    )(x, g_all.reshape(DEPTH, 1, D_MODEL), mod, mod, w_route, b_route)


def moe_plan(route, counts):
    cnt = counts[0, MOE_GROUPS:MOE_GROUPS + MOE_EXPERTS].astype(jnp.int32)
    padded = ((cnt + MOE_TILE - 1) // MOE_TILE) * MOE_TILE
    pad_end = jnp.cumsum(padded)
    pad_off = pad_end - padded
    e = route[:, ROUTE_E0:ROUTE_E1 + 1].astype(jnp.int32)
    rank = route[:, ROUTE_R0:ROUTE_R1 + 1].astype(jnp.int32)
    hot = e[:, :, None] == jnp.arange(MOE_EXPERTS, dtype=jnp.int32)[None, None, :]
    dest = jnp.sum(jnp.where(hot, pad_off[None, None, :], 0), axis=-1) + rank
    n_used = (pad_end[-1] // MOE_TILE).astype(jnp.int32).reshape(1)
    tile_start = jnp.arange(MOE_NT, dtype=jnp.int32) * MOE_TILE
    tile_expert = jnp.sum((pad_end[None, :] <= tile_start[:, None]).astype(jnp.int32), axis=1)
    tile_expert = jnp.minimum(tile_expert, MOE_EXPERTS - 1).astype(jnp.int32)
    tok = jnp.zeros((MOE_NT * MOE_TILE,), jnp.int32).at[dest.reshape(-1)].set(
        jnp.arange(dest.size, dtype=jnp.int32) // 2)
    return dest, tok, tile_expert, n_used


def _dispatch_kernel(nu_ref, tok_ref, tokn_ref, h_hbm, o_ref, buf, sem):
    i = pl.program_id(0)
    nu = nu_ref[0]

    def row_copy(t, slot, r):
        return pltpu.make_async_copy(h_hbm.at[pl.ds(t, 1)], buf.at[slot, pl.ds(r, 1)], sem.at[slot])

    def issue(tref, slot):
        def body(r, c):
            row_copy(tref[0, 0, r], slot, r).start()
            return c
        lax.fori_loop(0, MOE_TILE, body, 0, unroll=8)

    @pl.when(i == 0)
    def _():
        issue(tok_ref, 0)

    @pl.when(i + 1 < nu)
    def _():
        issue(tokn_ref, (i + 1) % 2)

    @pl.when(i < nu)
    def _():
        slot = i % 2

        def drain(r, c):
            row_copy(0, slot, r).wait()
            return c
        lax.fori_loop(0, MOE_TILE, drain, 0, unroll=8)
        o_ref[...] = buf[slot]

    @pl.when(i >= nu)
    def _():
        o_ref[...] = jnp.zeros_like(o_ref)


def moe_dispatch(h, tok, n_used):
    smem_tile = lambda off: pl.BlockSpec(
        (1, 1, MOE_TILE), lambda i, nu: (jnp.minimum(i + off, MOE_NT - 1), 0, 0), memory_space=pltpu.SMEM)
    grid_spec = pltpu.PrefetchScalarGridSpec(
        num_scalar_prefetch=1,
        grid=(MOE_NT,),
        in_specs=[smem_tile(0), smem_tile(1), pl.BlockSpec(memory_space=pl.ANY)],
        out_specs=pl.BlockSpec((MOE_TILE,) + h.shape[1:], lambda i, nu: (i, 0, 0)),
        scratch_shapes=[pltpu.VMEM((2, MOE_TILE) + h.shape[1:], h.dtype), pltpu.SemaphoreType.DMA((2,))])
    tok3 = tok.reshape(MOE_NT, 1, MOE_TILE)
    return pl.pallas_call(
        _dispatch_kernel,
        grid_spec=grid_spec,
        out_shape=jax.ShapeDtypeStruct((MOE_NT * MOE_TILE,) + h.shape[1:], h.dtype),
        compiler_params=_cp("arbitrary"),
        name="moe_dispatch",
    )(n_used, tok3, tok3, h)


def _expert_kernel(te_ref, nu_ref, hs_ref, wg_ref, wu_ref, wd_ref, o_ref, wgb, wub, wdb):
    i = pl.program_id(0)

    @pl.when(i < nu_ref[0])
    def _():
        @pl.when((i == 0) | (te_ref[i] != te_ref[jnp.maximum(i - 1, 0)]))
        def _():
            wgb[...] = wg_ref[...].astype(BF16)
            wub[...] = wu_ref[...].astype(BF16)
            wdb[...] = wd_ref[...].astype(BF16)

        w = jnp.concatenate([hs_ref[:, j, :] for j in range(MOE_ROW_TILES)], axis=1)
        lo = lax.bitcast_convert_type(w << 16, F32).astype(BF16)
        hi = lax.bitcast_convert_type(w & jnp.uint32(0xFFFF0000), F32).astype(BF16)
        h = jnp.concatenate([lo, hi], axis=1)
        g = jnp.dot(h, wgb[...], preferred_element_type=F32)
        u = jnp.dot(h, wub[...], preferred_element_type=F32)
        hid = (jax.nn.silu(g) * u).astype(BF16)
        o_ref[...] = jnp.dot(hid, wdb[...], preferred_element_type=F32)

    @pl.when(i >= nu_ref[0])
    def _():
        o_ref[...] = jnp.zeros_like(o_ref)


def moe_experts(hs, tile_expert, n_used, w_gate, w_up, w_down, layer):
    def wspec(shape):
        return pl.BlockSpec((None, None, None) + shape,
                            lambda i, te, nu: (layer, te[i] // MOE_EXPERTS_PER_GROUP,
                                               te[i] % MOE_EXPERTS_PER_GROUP, 0, 0))

    grid_spec = pltpu.PrefetchScalarGridSpec(
        num_scalar_prefetch=2,
        grid=(MOE_NT,),
        in_specs=[pl.BlockSpec((MOE_TILE, MOE_ROW_TILES, HEAD_DIM),
                               lambda i, te, nu: (jnp.minimum(i, nu[0] - 1), 0, 0)),
                  wspec((D_MODEL, MOE_FFN)), wspec((D_MODEL, MOE_FFN)), wspec((MOE_FFN, D_MODEL))],
        out_specs=pl.BlockSpec((MOE_TILE, D_MODEL), lambda i, te, nu: (i, 0)),
        scratch_shapes=[pltpu.VMEM((D_MODEL, MOE_FFN), BF16),
                        pltpu.VMEM((D_MODEL, MOE_FFN), BF16),
                        pltpu.VMEM((MOE_FFN, D_MODEL), BF16)])
    return pl.pallas_call(
        _expert_kernel,
        grid_spec=grid_spec,
        out_shape=jax.ShapeDtypeStruct((MOE_NT * MOE_TILE, D_MODEL), F32),
        compiler_params=_cp("arbitrary"),
        name="moe_experts",
    )(tile_expert, n_used, hs, w_gate, w_up, w_down)


def _combine_kernel(dest_ref, destn_ref, ys_hbm, r_ref, x_ref, gate_ref, o_ref, buf, sem):
    i = pl.program_id(0)
    n = pl.num_programs(0)
    tm = x_ref.shape[0]

    def row_copy(d, slot, k, r):
        return pltpu.make_async_copy(ys_hbm.at[pl.ds(d, 1), :], buf.at[slot, k, pl.ds(r, 1), :], sem.at[slot])

    def issue(dref, slot):
        def body(r, c):
            row_copy(dref[0, 0, 2 * r], slot, 0, r).start()
            row_copy(dref[0, 0, 2 * r + 1], slot, 1, r).start()
            return c
        lax.fori_loop(0, tm, body, 0, unroll=8)

    @pl.when(i == 0)
    def _():
        issue(dest_ref, 0)

    @pl.when(i + 1 < n)
    def _():
        issue(destn_ref, (i + 1) % 2)

    slot = i % 2

    def drain(r, c):
        row_copy(0, slot, 0, r).wait()
        row_copy(0, slot, 1, r).wait()
        return c
    lax.fori_loop(0, tm, drain, 0, unroll=8)
    r = r_ref[...]
    y = r[:, ROUTE_W0:ROUTE_W0 + 1] * buf[slot, 0] + r[:, ROUTE_W1:ROUTE_W1 + 1] * buf[slot, 1]
    o_ref[...] = x_ref[...] + gate_ref[...] * y


def moe_combine(ys, dest, route, x, mod, layer, gate_col, tm=256):
    t = x.shape[0]
    row = pl.BlockSpec((tm, D_MODEL), lambda i: (i, 0))
    nblk = t // tm
    smem = lambda off: pl.BlockSpec((1, 1, 2 * tm), lambda i: (jnp.minimum(i + off, nblk - 1), 0, 0),
                                    memory_space=pltpu.SMEM)
    dest3 = dest.reshape(nblk, 1, 2 * tm)
    return pl.pallas_call(
        _combine_kernel,
        grid=(nblk,),
        in_specs=[smem(0), smem(1),
                  pl.BlockSpec(memory_space=pl.ANY),
                  pl.BlockSpec((tm, ROUTE_LANES), lambda i: (i, 0)),
                  row,
                  pl.BlockSpec((None, 1, D_MODEL),
                               lambda i: (layer * MOD_ROWS + _mod_row(i, tm), 0, gate_col))],
        out_specs=row,
        out_shape=jax.ShapeDtypeStruct((t, D_MODEL), F32),
        scratch_shapes=[pltpu.VMEM((2, 2, tm, D_MODEL), F32), pltpu.SemaphoreType.DMA((2,))],
        compiler_params=_cp("arbitrary"),
        name="moe_combine",
    )(dest3, dest3, ys, route, x, mod)


def _final_norm_kernel(x_ref, g_ref, o_ref):
    x = x_ref[...]
    o_ref[...] = x * lax.rsqrt(jnp.mean(x * x, axis=-1, keepdims=True) + RMS_EPS) * g_ref[...]


def final_norm(x, g, rows, tm=512):
    row = pl.BlockSpec((tm, D_MODEL), lambda i: (i, 0))
    return pl.pallas_call(
        _final_norm_kernel,
        grid=(rows // tm,),
        in_specs=[row, pl.BlockSpec((1, D_MODEL), lambda i: (0, 0))],
        out_specs=row,
        out_shape=jax.ShapeDtypeStruct((rows, D_MODEL), F32),
        compiler_params=_cp("arbitrary"),
        name="final_norm",
    )(x, g.reshape(1, D_MODEL))


def attention_layer(xs, mod, layer, norm1_g, w_in, w_out, rpb, q_gain, k_gain, cos_t, sin_t):
    li = layer // 2
    p = norm_mod_matmul(xs, norm1_g, mod, w_in, layer, li, 0, 1)
    gains = jnp.stack([q_gain[li], k_gain[li]]).reshape(2, 1, HEAD_DIM)
    qkv = gqa_prep(p, gains, cos_t, sin_t)
    kcol, vcol = GQA_Q_HEADS, GQA_Q_HEADS + GQA_KV_HEADS
    grp = GQA_Q_HEADS // GQA_KV_HEADS
    ctx0 = T_LAT // CTX_LEN

    oa = na_attention(p, na_bias_table(rpb[li]))
    tq = 256
    nq = SEQ // tq
    ob = attention(
        qkv, lambda b, k, q: (b * nq + q, k),
        [qkv, qkv],
        [lambda b, k, q: (b, kcol + k), lambda b, k, q: (ctx0 + b, kcol + k)],
        [lambda b, k, q: (b, vcol + k), lambda b, k, q: (ctx0 + b, vcol + k)],
        [SEQ, CTX_LEN], (BATCH, GQA_KV_HEADS, nq), tq, grp, 1.0,
        T_LAT, GQA_Q_HEADS * HEAD_DIM, lambda b, k, q: (b * nq + q, k), "gqa_latent")
    oac = attention(
        p, lambda b, h: (ctx0 + b, h),
        [p],
        [lambda b, h: (ctx0 + b, NA_HEADS + h)],
        [lambda b, h: (ctx0 + b, 2 * NA_HEADS + h)],
        [CTX_LEN], (BATCH, NA_HEADS), CTX_LEN, 1, ATT_SCALE * LOG2E,
        T_CTX, NA_HEADS * HEAD_DIM, lambda b, h: (b, h), "na_context")
    obc = attention(
        qkv, lambda b, k: (ctx0 + b, k),
        [qkv],
        [lambda b, k: (ctx0 + b, kcol + k)],
        [lambda b, k: (ctx0 + b, vcol + k)],
        [CTX_LEN], (BATCH, GQA_KV_HEADS), CTX_LEN, grp, 1.0,
        T_CTX, GQA_Q_HEADS * HEAD_DIM, lambda b, k: (b, k), "gqa_context")
    oa_all = jnp.concatenate([oa, oac], axis=0)
    ob_all = jnp.concatenate([ob, obc], axis=0)
    return proj_residual(oa_all, ob_all, w_out, li, xs, mod, layer, 2)


def s5_layer(xs, mod, layer, norm1_g, w_in, lam_re, lam_im, log_dt, b_re, b_im, c_re, c_im, d_skip, w_glu):
    li = layer // 2
    u = norm_mod_matmul(xs, norm1_g, mod, w_in, layer, li, 0, 1)
    wst, wy, a1, a2 = s5_weights(lam_re[li], lam_im[li], log_dt[li], b_re[li], b_im[li], c_re[li], c_im[li])
    perm = s5_lane_permutation()
    y_lat, y_ctx = s5_unpack(s5_core(s5_pack(u, perm), wst, wy, a1, a2), perm.T)
    y = jnp.concatenate([y_lat, y_ctx], axis=0)
    return glu_residual(u, y, d_skip, w_glu, li, xs, mod, layer, 2)


def moe_layer(xs, mod, layer, norm2_g, w_coarse, b_coarse, w_fine, b_fine, w_gate, w_up, w_down):
    wf = w_fine[layer].transpose(1, 0, 2).reshape(D_MODEL, MOE_EXPERTS)
    w_route = jnp.concatenate([w_coarse[layer], wf], axis=1).astype(F32)
    w_route = jnp.pad(w_route, ((0, 0), (0, ROUTE_LANES - w_route.shape[1])))
    b_route = jnp.concatenate([b_coarse[layer], b_fine[layer].reshape(-1)]).astype(F32)
    b_route = jnp.pad(b_route, (0, ROUTE_LANES - b_route.shape[0])).reshape(1, ROUTE_LANES)
    h, route, counts = moe_route(xs, norm2_g, mod, layer, 3, 4, w_route, b_route)
    dest, tok, tile_expert, n_used = moe_plan(route, counts)
    hs = moe_dispatch(h, tok, n_used)
    ys = moe_experts(hs, tile_expert, n_used, w_gate, w_up, w_down, layer)
    return moe_combine(ys, dest, route, xs, mod, layer, 5)


def kernel(x, c, ctx, c_ctx, ada_w, ada_b, norm1_g, norm2_g, final_g, attn_w_in, attn_w_out, na_rpb, q_gain, k_gain, s5_w_in, s5_lam_re, s5_lam_im, s5_log_dt, s5_b_re, s5_b_im, s5_c_re, s5_c_im, s5_d, s5_w_glu, moe_w_coarse, moe_b_coarse, moe_w_fine, moe_b_fine, moe_w_gate, moe_w_up, moe_w_down):
    xs = jnp.concatenate([x.reshape(T_LAT, D_MODEL), ctx.reshape(T_CTX, D_MODEL)], axis=0)
    c8 = jnp.concatenate([c, c_ctx[None, :], jnp.zeros((MOD_ROWS - BATCH - 1, D_MODEL), F32)], axis=0)
    mod = ada_mod(c8, ada_w, ada_b).reshape(DEPTH * MOD_ROWS, 1, 6 * D_MODEL)
    cos_t, sin_t = rope_tables()
    attn_w_in, attn_w_out, s5_w_in, s5_w_glu = (w.astype(BF16) for w in (attn_w_in, attn_w_out, s5_w_in, s5_w_glu))
    for layer in range(DEPTH):
        if layer % 2 == 0:
            xs = attention_layer(xs, mod, layer, norm1_g, attn_w_in, attn_w_out, na_rpb, q_gain, k_gain,
                                 cos_t, sin_t)
        else:
            xs = s5_layer(xs, mod, layer, norm1_g, s5_w_in, s5_lam_re, s5_lam_im, s5_log_dt,
                          s5_b_re, s5_b_im, s5_c_re, s5_c_im, s5_d, s5_w_glu)
        xs = moe_layer(xs, mod, layer, norm2_g, moe_w_coarse, moe_b_coarse, moe_w_fine, moe_b_fine,
                       moe_w_gate, moe_w_up, moe_w_down)
    return final_norm(xs, final_g, T_LAT).reshape(BATCH, SEQ, D_MODEL)
```

```python
import functools
import math

import jax
import jax.numpy as jnp
from jax import lax
from jax.experimental import pallas as pl
from jax.experimental.pallas import tpu as pltpu

F32 = jnp.float32
BF16 = jnp.bfloat16

D_MODEL = 2048
BATCH = 4
SEQ = 4096
DEPTH = 4
GRID_W = 64
CTX_LEN = 256
HEAD_DIM = 128
NA_HEADS = 8
NA_ROWS = 8
NA_COLS = 16
GQA_Q_HEADS = 8
GQA_KV_HEADS = 2
ROPE_THETA = 10000.0
ROPE_AXIS_DIM = HEAD_DIM // 2
ATTN_IN_COLS = (3 * NA_HEADS + GQA_Q_HEADS + 2 * GQA_KV_HEADS) * HEAD_DIM
S5_GROUP = 16
S5_GROUPS = D_MODEL // S5_GROUP
S5_STATE = 64
MOE_GROUPS = 4
MOE_EXPERTS_PER_GROUP = 8
MOE_EXPERTS = MOE_GROUPS * MOE_EXPERTS_PER_GROUP
MOE_FFN = D_MODEL // 4
RMS_EPS = 1e-6
NEG_INF = -1e30

T_LAT = BATCH * SEQ
T_CTX = BATCH * CTX_LEN
T_ALL = T_LAT + T_CTX
MOD_ROWS = 8
ATT_SCALE = HEAD_DIM ** -0.5

S5_CHUNK = 16
S5_SEQ = CTX_LEN + SEQ
S5_NCHUNK = S5_SEQ // S5_CHUNK
S5_CW = S5_CHUNK * S5_GROUP
S5_GB = 4

MOE_TILE = 512
MOE_SLOTS = T_ALL * 2
MOE_NT = MOE_SLOTS // MOE_TILE + MOE_EXPERTS

VMEM_LIMIT = 56 * 1024 * 1024


def _cp(*sem):
    return pltpu.CompilerParams(dimension_semantics=sem, vmem_limit_bytes=VMEM_LIMIT)


def _mod_row(i, tm):
    return jnp.minimum((i * tm) // SEQ, BATCH)


def _ada_kernel(c_ref, w_ref, b_ref, o_ref):
    s = jax.nn.silu(c_ref[...])
    o_ref[...] = jnp.dot(s, w_ref[...], precision=lax.Precision.HIGHEST,
                         preferred_element_type=F32) + b_ref[...]


def ada_mod(c8, ada_w, ada_b):
    tn = 1024
    n = ada_w.shape[-1]
    return pl.pallas_call(
        _ada_kernel,
        grid=(DEPTH, n // tn),
        in_specs=[pl.BlockSpec((MOD_ROWS, D_MODEL), lambda l, j: (0, 0)),
                  pl.BlockSpec((None, D_MODEL, tn), lambda l, j: (l, 0, j)),
                  pl.BlockSpec((None, 1, tn), lambda l, j: (l, 0, j))],
        out_specs=pl.BlockSpec((None, MOD_ROWS, tn), lambda l, j: (l, 0, j)),
        out_shape=jax.ShapeDtypeStruct((DEPTH, MOD_ROWS, n), F32),
        compiler_params=_cp("arbitrary", "arbitrary"),
        name="ada_mod",
    )(c8, ada_w, ada_b.reshape(DEPTH, 1, n))


def _norm_mod(x, g, sh, sc):
    y = x * lax.rsqrt(jnp.mean(x * x, axis=-1, keepdims=True) + RMS_EPS) * g
    return y * (1.0 + sc) + sh


def _mod_spec(layer, col, tm):
    return pl.BlockSpec((None, 1, D_MODEL),
                        lambda i, j: (layer * MOD_ROWS + _mod_row(i, tm), 0, col))


def _gate_spec(layer, col, tm, tn):
    return pl.BlockSpec((None, 1, tn),
                        lambda i, j: (layer * MOD_ROWS + _mod_row(i, tm), 0, col * (D_MODEL // tn) + j))


def _nm_mm_kernel(x_ref, g_ref, sh_ref, sc_ref, w_ref, o_ref, h_ref):
    @pl.when(pl.program_id(1) == 0)
    def _():
        h_ref[...] = _norm_mod(x_ref[...], g_ref[...], sh_ref[...], sc_ref[...]).astype(BF16)

    o_ref[...] = jnp.dot(h_ref[...], w_ref[...],
                         preferred_element_type=F32).astype(o_ref.dtype)


def norm_mod_matmul(x, g_all, mod, w_all, layer, wi, sh_col, sc_col, tm=1024, tn=512):
    t = x.shape[0]
    n = w_all.shape[-1]
    return pl.pallas_call(
        _nm_mm_kernel,
        grid=(t // tm, n // tn),
        in_specs=[pl.BlockSpec((tm, D_MODEL), lambda i, j: (i, 0)),
                  pl.BlockSpec((None, 1, D_MODEL), lambda i, j: (layer, 0, 0)),
                  _mod_spec(layer, sh_col, tm),
                  _mod_spec(layer, sc_col, tm),
                  pl.BlockSpec((None, D_MODEL, tn), lambda i, j: (wi, 0, j))],
        out_specs=pl.BlockSpec((tm, tn), lambda i, j: (i, j)),
        out_shape=jax.ShapeDtypeStruct((t, n), F32),
        scratch_shapes=[pltpu.VMEM((tm, D_MODEL), BF16)],
        compiler_params=_cp("arbitrary", "arbitrary"),
        name="norm_mod_matmul",
    )(x, g_all.reshape(DEPTH, 1, D_MODEL), mod, mod, w_all)


def _proj_res_kernel(a_ref, b_ref, wa_ref, wb_ref, x_ref, gate_ref, o_ref):
    y = jnp.dot(a_ref[...], wa_ref[...], preferred_element_type=F32)
    y = y + jnp.dot(b_ref[...], wb_ref[...], preferred_element_type=F32)
    o_ref[...] = x_ref[...] + gate_ref[...] * y


def proj_residual(oa, ob, w_all, wi, x, mod, layer, gate_col, tm=1024, tn=512):
    t, ka = oa.shape
    kb = ob.shape[1]
    assert ka == kb
    return pl.pallas_call(
        _proj_res_kernel,
        grid=(t // tm, D_MODEL // tn),
        in_specs=[pl.BlockSpec((tm, ka), lambda i, j: (i, 0)),
                  pl.BlockSpec((tm, kb), lambda i, j: (i, 0)),
                  pl.BlockSpec((None, ka, tn), lambda i, j: (wi, 0, j)),
                  pl.BlockSpec((None, kb, tn), lambda i, j: (wi, 1, j)),
                  pl.BlockSpec((tm, tn), lambda i, j: (i, j)),
                  _gate_spec(layer, gate_col, tm, tn)],
        out_specs=pl.BlockSpec((tm, tn), lambda i, j: (i, j)),
        out_shape=jax.ShapeDtypeStruct((t, D_MODEL), F32),
        compiler_params=_cp("arbitrary", "arbitrary"),
        name="proj_residual",
    )(oa, ob, w_all, w_all, x, mod)


def _gqa_prep_kernel(p_ref, gain_ref, cos_ref, sin_ref, o_ref):
    nqk = GQA_Q_HEADS + GQA_KV_HEADS
    nh = nqk + GQA_KV_HEADS
    cos = cos_ref[...]
    sin = sin_ref[...]
    lane = lax.broadcasted_iota(jnp.int32, cos.shape, 1)
    half = ROPE_AXIS_DIM // 2
    first_half = (lane % ROPE_AXIS_DIM) < half
    for h in range(nh):
        cols = slice(h * HEAD_DIM, (h + 1) * HEAD_DIM)
        x = p_ref[:, cols]
        if h < nqk:
            is_q = h < GQA_Q_HEADS
            gain = gain_ref[0 if is_q else 1] * (ATT_SCALE * LOG2E if is_q else 1.0)
            y = x * lax.rsqrt(jnp.mean(x * x, axis=-1, keepdims=True) + RMS_EPS) * gain
            partner = jnp.where(first_half, pltpu.roll(y, HEAD_DIM - half, 1), pltpu.roll(y, half, 1))
            x = y * cos + partner * sin
        o_ref[:, cols] = x.astype(BF16)


def gqa_prep(p, gains, cos_t, sin_t, tm=512):
    t = p.shape[0]
    nh = GQA_Q_HEADS + 2 * GQA_KV_HEADS
    w = nh * HEAD_DIM
    assert (3 * NA_HEADS * HEAD_DIM) % w == 0
    col_blk = 3 * NA_HEADS * HEAD_DIM // w
    lat_tiles = T_LAT // tm
    per_seq = SEQ // tm

    def tab_idx(i):
        return (jnp.where(i < lat_tiles, i % per_seq, per_seq), 0)

    return pl.pallas_call(
        _gqa_prep_kernel,
        grid=(t // tm,),
        in_specs=[pl.BlockSpec((tm, w), lambda i: (i, col_blk)),
                  pl.BlockSpec((2, 1, HEAD_DIM), lambda i: (0, 0, 0)),
                  pl.BlockSpec((tm, HEAD_DIM), tab_idx),
                  pl.BlockSpec((tm, HEAD_DIM), tab_idx)],
        out_specs=pl.BlockSpec((tm, w), lambda i: (i, 0)),
        out_shape=jax.ShapeDtypeStruct((t, w), BF16),
        compiler_params=_cp("arbitrary"),
        name="gqa_prep",
    )(p, gains, cos_t, sin_t)


def rope_tables(tm=512):
    t = jnp.arange(SEQ)
    row = (t // GRID_W).astype(F32)
    col = (t % GRID_W).astype(F32)
    inv = ROPE_THETA ** (-jnp.arange(0, ROPE_AXIS_DIM, 2, dtype=F32) / ROPE_AXIS_DIM)
    ar = row[:, None] * inv[None]
    ac = col[:, None] * inv[None]
    cos_t = jnp.concatenate([jnp.cos(ar), jnp.cos(ar), jnp.cos(ac), jnp.cos(ac)], axis=-1)
    sin_t = jnp.concatenate([-jnp.sin(ar), jnp.sin(ar), -jnp.sin(ac), jnp.sin(ac)], axis=-1)
    cos_t = jnp.concatenate([cos_t, jnp.ones((tm, HEAD_DIM), F32)], axis=0)
    sin_t = jnp.concatenate([sin_t, jnp.zeros((tm, HEAD_DIM), F32)], axis=0)
    return cos_t, sin_t


LOG2E = 1.4426950408889634
ATT_KCHUNK = 512


def _softmax_chunk(q, k, v, bias, m, l, acc):
    s = lax.dot_general(q, k.astype(BF16), (((1,), (1,)), ((), ())), preferred_element_type=F32)
    if bias is not None:
        s = s + bias
    m_new = jnp.maximum(m, jnp.max(s, axis=-1, keepdims=True))
    alpha = jnp.exp2(m - m_new)
    p = jnp.exp2(s - m_new)
    l = alpha * l + jnp.sum(p, axis=-1, keepdims=True)
    acc = alpha * acc + jnp.dot(p.astype(BF16), v.astype(BF16), preferred_element_type=F32)
    return m_new, l, acc


def _attn_kernel(*refs, nseg, group, scale):
    q_ref = refs[0]
    k_refs = refs[1:1 + nseg]
    v_refs = refs[1 + nseg:1 + 2 * nseg]
    o_ref = refs[1 + 2 * nseg]
    tq = q_ref.shape[0]
    q = jnp.concatenate([q_ref[:, g * HEAD_DIM:(g + 1) * HEAD_DIM] for g in range(group)], axis=0)
    if scale != 1.0:
        q = q.astype(F32) * scale
    q = q.astype(BF16)
    m = jnp.full((q.shape[0], 1), -jnp.inf, F32)
    l = jnp.zeros((q.shape[0], 1), F32)
    acc = jnp.zeros((q.shape[0], HEAD_DIM), F32)
    for k_ref, v_ref in zip(k_refs, v_refs):
        nk = k_ref.shape[0]
        for c0 in range(0, nk, ATT_KCHUNK):
            cs = min(ATT_KCHUNK, nk - c0)
            m, l, acc = _softmax_chunk(q, k_ref[c0:c0 + cs, :], v_ref[c0:c0 + cs, :], None, m, l, acc)
    o = acc / l
    for g in range(group):
        o_ref[:, g * HEAD_DIM:(g + 1) * HEAD_DIM] = o[g * tq:(g + 1) * tq].astype(o_ref.dtype)


def attention(q_arr, q_idx, kv_arrs, k_idx, v_idx, kv_rows, grid, tq, group, scale, out_rows, out_cols, o_idx, name):
    nseg = len(kv_rows)
    in_specs = [pl.BlockSpec((tq, group * HEAD_DIM), q_idx)]
    in_specs += [pl.BlockSpec((kv_rows[i], HEAD_DIM), k_idx[i]) for i in range(nseg)]
    in_specs += [pl.BlockSpec((kv_rows[i], HEAD_DIM), v_idx[i]) for i in range(nseg)]
    return pl.pallas_call(
        functools.partial(_attn_kernel, nseg=nseg, group=group, scale=scale),
        grid=grid,
        in_specs=in_specs,
        out_specs=pl.BlockSpec((tq, group * HEAD_DIM), o_idx),
        out_shape=jax.ShapeDtypeStruct((out_rows, out_cols), BF16),
        compiler_params=_cp(*(["arbitrary"] * len(grid))),
        name=name,
    )(q_arr, *kv_arrs, *kv_arrs)


NA_QROWS = 8
NA_KROWS = 16
NA_TQ = NA_QROWS * GRID_W
NA_TK = NA_KROWS * GRID_W
GRID_H = SEQ // GRID_W


def _na_kernel(q_ref, k_ref, v_ref, kc_ref, vc_ref, bias_ref, o_ref):
    rb = pl.program_id(2)
    w0 = jnp.clip(NA_QROWS * rb - (NA_KROWS - NA_QROWS) // 2, 0, GRID_H - NA_KROWS) * GRID_W
    w0 = pl.multiple_of(w0, 4 * GRID_W)
    q = (q_ref[...] * (ATT_SCALE * LOG2E)).astype(BF16)
    m = jnp.full((NA_TQ, 1), -jnp.inf, F32)
    l = jnp.zeros((NA_TQ, 1), F32)
    acc = jnp.zeros((NA_TQ, HEAD_DIM), F32)
    m, l, acc = _softmax_chunk(q, kc_ref[...], vc_ref[...], None, m, l, acc)
    for c0 in range(0, NA_TK, ATT_KCHUNK):
        kw = k_ref[pl.ds(w0 + c0, ATT_KCHUNK), :]
        vw = v_ref[pl.ds(w0 + c0, ATT_KCHUNK), :]
        m, l, acc = _softmax_chunk(q, kw, vw, bias_ref[:, c0:c0 + ATT_KCHUNK], m, l, acc)
    o_ref[...] = (acc / l).astype(o_ref.dtype)


def na_bias_table(rpb):
    aw = jnp.arange(GRID_W)
    col_start = jnp.clip(aw - NA_COLS // 2, 0, GRID_W - NA_COLS)
    col_ok = (aw[None, :] >= col_start[:, None]) & (aw[None, :] < col_start[:, None] + NA_COLS)
    off_c = jnp.clip(aw[None, :] - aw[:, None] + (NA_COLS - 1), 0, 2 * NA_COLS - 2)
    nh, nr, nc = rpb.shape
    pick = (off_c.reshape(1, -1) == jnp.arange(nc)[:, None]).astype(F32)
    tiles = jnp.dot(rpb.astype(F32).reshape(nh * nr, nc), pick, precision=lax.Precision.HIGHEST)
    tiles = jnp.where(col_ok.reshape(1, -1), tiles * LOG2E, NEG_INF).reshape(nh, nr, GRID_W, GRID_W)
    blocked = jnp.full((nh, 1, GRID_W, GRID_W), NEG_INF, F32)
    tiles = jnp.concatenate([tiles, blocked], axis=1)
    nblk = GRID_H // NA_QROWS
    out = []
    for rb in (0, nblk // 2, nblk - 1):
        w0 = min(max(NA_QROWS * rb - (NA_KROWS - NA_QROWS) // 2, 0), GRID_H - NA_KROWS)
        picks = []
        for qr in range(NA_QROWS * rb, NA_QROWS * (rb + 1)):
            rs = min(max(qr - NA_ROWS // 2, 0), GRID_H - NA_ROWS)
            for kr in range(w0, w0 + NA_KROWS):
                picks.append(kr - qr + NA_ROWS - 1 if rs <= kr < rs + NA_ROWS else nr)
        b = jnp.stack([tiles[:, k] for k in picks], axis=1)
        b = b.reshape(nh, NA_QROWS, NA_KROWS, GRID_W, GRID_W).transpose(0, 1, 3, 2, 4)
        out.append(b.reshape(nh, NA_TQ, NA_TK))
    return jnp.stack(out, axis=1)


def na_attention(p, bias):
    nblk = GRID_H // NA_QROWS
    ctx_blk0 = T_LAT // CTX_LEN

    def pat(rb):
        return jnp.where(rb == 0, 0, jnp.where(rb == nblk - 1, 2, 1))

    return pl.pallas_call(
        _na_kernel,
        grid=(NA_HEADS, BATCH, nblk),
        in_specs=[pl.BlockSpec((NA_TQ, HEAD_DIM), lambda h, b, r: (b * nblk + r, h)),
                  pl.BlockSpec((SEQ, HEAD_DIM), lambda h, b, r: (b, NA_HEADS + h)),
                  pl.BlockSpec((SEQ, HEAD_DIM), lambda h, b, r: (b, 2 * NA_HEADS + h)),
                  pl.BlockSpec((CTX_LEN, HEAD_DIM), lambda h, b, r: (ctx_blk0 + b, NA_HEADS + h)),
                  pl.BlockSpec((CTX_LEN, HEAD_DIM), lambda h, b, r: (ctx_blk0 + b, 2 * NA_HEADS + h)),
                  pl.BlockSpec((None, None, NA_TQ, NA_TK), lambda h, b, r: (h, pat(r), 0, 0))],
        out_specs=pl.BlockSpec((NA_TQ, HEAD_DIM), lambda h, b, r: (b * nblk + r, h)),
        out_shape=jax.ShapeDtypeStruct((T_LAT, NA_HEADS * HEAD_DIM), BF16),
        compiler_params=_cp("arbitrary", "arbitrary", "arbitrary"),
        name="na_attention",
    )(p, p, p, p, p, bias)


def s5_weights(lam_re, lam_im, log_dt, b_re, b_im, c_re, c_im):
    hi = lax.Precision.HIGHEST
    L = S5_CHUNK
    lr = jnp.minimum(lam_re.astype(F32), -1e-4)
    li = lam_im.astype(F32)
    dt = jnp.exp(log_dt.astype(F32))[..., None]
    mag = jnp.exp(lr * dt)
    ar = mag * jnp.cos(li * dt)
    ai = mag * jnp.sin(li * dt)
    den = lr * lr + li * li
    xr = ar - 1.0
    fr = (xr * lr + ai * li) / den
    fi = (ai * lr - xr * li) / den
    br = b_re.astype(F32)
    bi = b_im.astype(F32)
    bbr = fr[..., None] * br - fi[..., None] * bi
    bbi = fr[..., None] * bi + fi[..., None] * br
    k = jnp.arange(L + 1, dtype=F32)[:, None, None, None]
    pm = jnp.exp(lr * dt * k)
    pr = pm * jnp.cos(li * dt * k)
    pi = pm * jnp.sin(li * dt * k)
    e_r = pr[..., None] * bbr - pi[..., None] * bbi
    e_i = pr[..., None] * bbi + pi[..., None] * bbr
    cr = c_re.astype(F32)
    ci = c_im.astype(F32)
    kern = (jnp.einsum('dgop,kdgpi->kdgoi', cr, e_r[:L], precision=hi)
            - jnp.einsum('dgop,kdgpi->kdgoi', ci, e_i[:L], precision=hi))
    s_i = jnp.arange(L)
    ng = lr.shape[1]
    lag = s_i[None, :, None] - s_i[:, None, None]
    place_f = (lag == s_i[None, None, :]).astype(F32)
    place = jnp.stack([place_f, place_f[::-1, ::-1]], axis=-1)
    toep = jnp.einsum('stkd,kdgoi->gsito', place, kern, precision=hi)
    w_toep_sum = toep.reshape(ng, S5_CW, S5_CW)
    st_r = e_r[L - 1 - s_i].transpose(1, 2, 0, 4, 3).reshape(2, ng, S5_CW, S5_STATE)
    st_i = e_i[L - 1 - s_i].transpose(1, 2, 0, 4, 3).reshape(2, ng, S5_CW, S5_STATE)
    w_state = jnp.concatenate([st_r, st_i], axis=-1)
    w_state_sw = jnp.concatenate([st_i, st_r], axis=-1)
    qr = pr[1:, :, :, None, :]
    qi = pi[1:, :, :, None, :]
    d_r = cr[None] * qr - ci[None] * qi
    d_i = cr[None] * qi + ci[None] * qr
    wo_r = d_r.transpose(1, 2, 4, 0, 3).reshape(2, ng, S5_STATE, S5_CW)
    wo_i = (-d_i).transpose(1, 2, 4, 0, 3).reshape(2, ng, S5_STATE, S5_CW)
    w_out = jnp.concatenate([wo_r, wo_i], axis=2)
    flip_rows = lambda w: w.reshape(ng, L, S5_GROUP, w.shape[-1])[:, ::-1].reshape(w.shape)
    flip_cols = lambda w: w.reshape(ng, w.shape[1], L, S5_GROUP)[:, :, ::-1].reshape(w.shape)
    wst = jnp.concatenate([w_state[0], flip_rows(w_state[1]),
                           w_state_sw[0], flip_rows(w_state_sw[1])], axis=-1).astype(BF16)
    wy = jnp.concatenate([w_toep_sum, w_out[0], flip_cols(w_out[1])], axis=1).astype(BF16)
    a1 = jnp.concatenate([pr[L], pr[L]], axis=-1)
    a2 = jnp.concatenate([-pi[L], pi[L]], axis=-1)
    per_row = lambda a: jnp.repeat(a.transpose(1, 0, 2), BATCH, axis=1)
    return wst, wy, per_row(a1), per_row(a2)


S5_CTX_CHUNKS = CTX_LEN // S5_CHUNK
S5_PACK = HEAD_DIM // S5_GROUP
S5_PACKS = S5_GROUPS // S5_PACK
S5_BROWS = S5_NCHUNK


def s5_lane_permutation():
    tl, g, hh = jnp.meshgrid(jnp.arange(S5_CHUNK), jnp.arange(S5_PACK), jnp.arange(S5_GROUP), indexing="ij")
    dst = (g * S5_CW + tl * S5_GROUP + hh).reshape(-1)
    return (dst[:, None] == jnp.arange(S5_PACK * S5_CW)[None, :]).astype(BF16)


S5_SB = 2


def _s5_pack_kernel(lat_ref, ctx_ref, p_ref, o_ref):
    cols = []
    for tl in range(S5_CHUNK):
        rows = []
        for b in range(S5_SB):
            rows.append(ctx_ref[pl.ds(b * CTX_LEN + tl, S5_CTX_CHUNKS, stride=S5_CHUNK), :])
            rows.append(lat_ref[pl.ds(b * SEQ + tl, SEQ // S5_CHUNK, stride=S5_CHUNK), :])
        cols.append(jnp.concatenate(rows, axis=0))
    xcat = jnp.concatenate(cols, axis=1).astype(BF16)
    o_ref[...] = jnp.dot(xcat, p_ref[...], preferred_element_type=F32).astype(BF16)


def s5_pack(u, perm):
    ctx0 = T_LAT // (S5_SB * CTX_LEN)
    w = S5_PACK * S5_CW
    return pl.pallas_call(
        _s5_pack_kernel,
        grid=(S5_PACKS, BATCH // S5_SB),
        in_specs=[pl.BlockSpec((S5_SB * SEQ, HEAD_DIM), lambda k, b: (b, k)),
                  pl.BlockSpec((S5_SB * CTX_LEN, HEAD_DIM), lambda k, b: (ctx0 + b, k)),
                  pl.BlockSpec((w, w), lambda k, b: (0, 0))],
        out_specs=pl.BlockSpec((None, S5_SB * S5_BROWS, w), lambda k, b: (k, b, 0)),
        out_shape=jax.ShapeDtypeStruct((S5_PACKS, BATCH * S5_BROWS, w), BF16),
        compiler_params=_cp("arbitrary", "arbitrary"),
        name="s5_pack",
    )(u, u, perm)


def _s5_kernel(u_ref, wst_ref, wy_ref, a1_ref, a2_ref, y_ref, s_ref, ssw_ref, hf_ref, hr_ref):
    gb = wst_ref.shape[0]
    nc = S5_NCHUNK
    sw = 2 * S5_STATE
    for j in range(gb):
        ug = u_ref[:, j * S5_CW:(j + 1) * S5_CW]
        s4 = jnp.dot(ug, wst_ref[j], preferred_element_type=F32)
        for b in range(BATCH):
            rows = s4[b * nc:(b + 1) * nc]
            s_ref[:, j, b, :] = rows[:, 0:sw]
            s_ref[:, j, BATCH + b, :] = rows[:, sw:2 * sw]
            ssw_ref[:, j, b, :] = rows[:, 2 * sw:3 * sw]
            ssw_ref[:, j, BATCH + b, :] = rows[:, 3 * sw:4 * sw]
    a1 = a1_ref[...]
    a2 = a2_ref[...]
    fwd = lax.broadcasted_iota(jnp.int32, (gb, 2 * BATCH, sw), 1) < BATCH

    def step(i, carry):
        h, hs = carry
        ri = jnp.where(i < S5_CTX_CHUNKS, S5_CTX_CHUNKS - 1 - i, nc - 1 + S5_CTX_CHUNKS - i)
        hf_ref[i] = h
        hr_ref[ri] = h
        s = jnp.where(fwd, s_ref[i], s_ref[ri])
        ssw = jnp.where(fwd, ssw_ref[i], ssw_ref[ri])
        return a1 * h + a2 * hs + s, a1 * hs - a2 * h + ssw

    z = jnp.zeros((gb, 2 * BATCH, sw), F32)
    lax.fori_loop(0, nc, step, (z, z), unroll=4)
    for j in range(gb):
        ug = u_ref[:, j * S5_CW:(j + 1) * S5_CW]
        hf = jnp.concatenate([hf_ref[:, j, b, :] for b in range(BATCH)], axis=0)
        hr = jnp.concatenate([hr_ref[:, j, BATCH + b, :] for b in range(BATCH)], axis=0)
        lhs = jnp.concatenate([ug, hf.astype(BF16), hr.astype(BF16)], axis=1)
        y_ref[:, j * S5_CW:(j + 1) * S5_CW] = jnp.dot(lhs, wy_ref[j], preferred_element_type=F32)


def s5_core(u_packed, wst, wy, a1, a2):
    gb = S5_GB
    per_pack = S5_PACK // gb
    rows = u_packed.shape[1]
    sw = 2 * S5_STATE
    blk = lambda *shape: pl.BlockSpec((gb,) + shape, lambda i: (i,) + (0,) * len(shape))
    io = pl.BlockSpec((None, rows, gb * S5_CW), lambda i: (i // per_pack, 0, i % per_pack))
    state = pltpu.VMEM((S5_NCHUNK, gb, 2 * BATCH, sw), F32)
    return pl.pallas_call(
        _s5_kernel,
        grid=(S5_GROUPS // gb,),
        in_specs=[io, blk(S5_CW, 4 * sw), blk(S5_CW + 2 * sw, S5_CW), blk(2 * BATCH, sw), blk(2 * BATCH, sw)],
        out_specs=io,
        out_shape=jax.ShapeDtypeStruct(u_packed.shape, F32),
        scratch_shapes=[state, state, state, state],
        compiler_params=_cp("arbitrary"),
        name="s5_core",
    )(u_packed, wst, wy, a1, a2)


def _s5_unpack_kernel(y_ref, pt_ref, lat_ref, ctx_ref):
    y = y_ref[...]
    hi = y.astype(BF16)
    lo = (y - hi.astype(F32)).astype(BF16)
    yn = (jnp.dot(hi, pt_ref[...], preferred_element_type=F32)
          + jnp.dot(lo, pt_ref[...], preferred_element_type=F32))
    for tl in range(S5_CHUNK):
        piece = yn[:, tl * HEAD_DIM:(tl + 1) * HEAD_DIM]
        for b in range(S5_SB):
            r0 = b * S5_BROWS
            ctx_ref[pl.ds(b * CTX_LEN + tl, S5_CTX_CHUNKS, stride=S5_CHUNK), :] = piece[r0:r0 + S5_CTX_CHUNKS]
            lat_ref[pl.ds(b * SEQ + tl, SEQ // S5_CHUNK, stride=S5_CHUNK), :] = (
                piece[r0 + S5_CTX_CHUNKS:r0 + S5_BROWS])


def s5_unpack(y_packed, perm_t):
    w = S5_PACK * S5_CW
    return pl.pallas_call(
        _s5_unpack_kernel,
        grid=(S5_PACKS, BATCH // S5_SB),
        in_specs=[pl.BlockSpec((None, S5_SB * S5_BROWS, w), lambda k, b: (k, b, 0)),
                  pl.BlockSpec((w, w), lambda k, b: (0, 0))],
        out_specs=[pl.BlockSpec((S5_SB * SEQ, HEAD_DIM), lambda k, b: (b, k)),
                   pl.BlockSpec((S5_SB * CTX_LEN, HEAD_DIM), lambda k, b: (b, k))],
        out_shape=[jax.ShapeDtypeStruct((T_LAT, D_MODEL), F32), jax.ShapeDtypeStruct((T_CTX, D_MODEL), F32)],
        compiler_params=_cp("arbitrary", "arbitrary"),
        name="s5_unpack",
    )(y_packed, perm_t)


def _glu_kernel(u_ref, y_ref, d_ref, wa_ref, wg_ref, x_ref, gate_ref, o_ref, h_ref):
    @pl.when(pl.program_id(1) == 0)
    def _():
        y = u_ref[...] * d_ref[...] + y_ref[...]
        h_ref[...] = jax.nn.gelu(y).astype(BF16)

    h = h_ref[...]
    za = jnp.dot(h, wa_ref[...], preferred_element_type=F32)
    zg = jnp.dot(h, wg_ref[...], preferred_element_type=F32)
    o_ref[...] = x_ref[...] + gate_ref[...] * (za * jax.nn.sigmoid(zg))


def glu_residual(u, y, d_all, w_all, wi, x, mod, layer, gate_col, tm=1024, tn=256):
    t = u.shape[0]
    nj = D_MODEL // tn
    row = pl.BlockSpec((tm, D_MODEL), lambda i, j: (i, 0))
    return pl.pallas_call(
        _glu_kernel,
        grid=(t // tm, nj),
        in_specs=[row, row,
                  pl.BlockSpec((None, 1, D_MODEL), lambda i, j: (wi, 0, 0)),
                  pl.BlockSpec((None, D_MODEL, tn), lambda i, j: (wi, 0, j)),
                  pl.BlockSpec((None, D_MODEL, tn), lambda i, j: (wi, 0, nj + j)),
                  pl.BlockSpec((tm, tn), lambda i, j: (i, j)),
                  _gate_spec(layer, gate_col, tm, tn)],
        out_specs=pl.BlockSpec((tm, tn), lambda i, j: (i, j)),
        out_shape=jax.ShapeDtypeStruct((t, D_MODEL), F32),
        scratch_shapes=[pltpu.VMEM((tm, D_MODEL), BF16)],
        compiler_params=_cp("arbitrary", "arbitrary"),
        name="glu_residual",
    )(u, y, d_all.reshape(-1, 1, D_MODEL), w_all, w_all, x, mod)


ROUTE_LANES = 128
MOE_ROW_TILES = D_MODEL // 2 // HEAD_DIM


def _route_kernel(x_ref, g_ref, sh_ref, sc_ref, wr_ref, br_ref, h_ref, r_ref, cnt_ref, carry_ref):
    @pl.when(pl.program_id(0) == 0)
    def _():
        carry_ref[...] = jnp.zeros_like(carry_ref)

    h = _norm_mod(x_ref[...], g_ref[...], sh_ref[...], sc_ref[...])
    bits = lax.bitcast_convert_type(h.astype(BF16).astype(F32), jnp.uint32)
    half = h.shape[1] // 2
    packed = (bits[:, :half] >> 16) | (bits[:, half:] & jnp.uint32(0xFFFF0000))
    for j in range(half // HEAD_DIM):
        h_ref[:, j, :] = packed[:, j * HEAD_DIM:(j + 1) * HEAD_DIM]
    lg = jnp.dot(h, wr_ref[...], precision=lax.Precision.HIGHEST, preferred_element_type=F32) + br_ref[...]
    lane = lax.broadcasted_iota(jnp.int32, lg.shape, 1)
    ninf = -jnp.inf
    coarse = lane < MOE_GROUPS
    lc = jnp.where(coarse, lg, ninf)
    mc = jnp.max(lc, axis=-1, keepdims=True)
    g_sel = jnp.min(jnp.where(lc == mc, lane, ROUTE_LANES), axis=-1, keepdims=True)
    p_sel = 1.0 / jnp.sum(jnp.where(coarse, jnp.exp(lc - mc), 0.0), axis=-1, keepdims=True)
    lo = MOE_GROUPS + MOE_EXPERTS_PER_GROUP * g_sel
    lf = jnp.where((lane >= lo) & (lane < lo + MOE_EXPERTS_PER_GROUP), lg, ninf)
    v0 = jnp.max(lf, axis=-1, keepdims=True)
    i0 = jnp.min(jnp.where(lf == v0, lane, ROUTE_LANES), axis=-1, keepdims=True)
    lf2 = jnp.where(lane == i0, ninf, lf)
    v1 = jnp.max(lf2, axis=-1, keepdims=True)
    i1 = jnp.min(jnp.where(lf2 == v1, lane, ROUTE_LANES), axis=-1, keepdims=True)
    e1 = jnp.exp(v1 - v0)
    w0 = p_sel / (1.0 + e1)
    w1 = w0 * e1
    tm = lg.shape[0]
    lower = (lax.broadcasted_iota(jnp.int32, (tm, tm), 1)
             < lax.broadcasted_iota(jnp.int32, (tm, tm), 0)).astype(BF16)
    carry = carry_ref[...]
    hot0 = (lane == i0).astype(F32)
    before0 = carry + jnp.dot(lower, hot0.astype(BF16), preferred_element_type=F32)
    rank0 = jnp.sum(hot0 * before0, axis=-1, keepdims=True)
    carry = carry + jnp.sum(hot0, axis=0, keepdims=True)
    hot1 = (lane == i1).astype(F32)
    before1 = carry + jnp.dot(lower, hot1.astype(BF16), preferred_element_type=F32)
    rank1 = jnp.sum(hot1 * before1, axis=-1, keepdims=True)
    carry = carry + jnp.sum(hot1, axis=0, keepdims=True)
    carry_ref[...] = carry
    cnt_ref[...] = carry
    cols = [(i0 - MOE_GROUPS).astype(F32), (i1 - MOE_GROUPS).astype(F32), w0, w1, rank0, rank1]
    r = jnp.zeros_like(lg)
    for c, val in enumerate(cols):
        r = jnp.where(lane == c, val, r)
    r_ref[...] = r


ROUTE_E0, ROUTE_E1, ROUTE_W0, ROUTE_W1, ROUTE_R0, ROUTE_R1 = range(6)


def moe_route(x, g_all, mod, layer, sh_col, sc_col, w_route, b_route, tm=512):
    t = x.shape[0]
    mspec = lambda col: pl.BlockSpec((None, 1, D_MODEL),
                                     lambda i: (layer * MOD_ROWS + _mod_row(i, tm), 0, col))
    return pl.pallas_call(
        _route_kernel,
        grid=(t // tm,),
        in_specs=[pl.BlockSpec((tm, D_MODEL), lambda i: (i, 0)),
                  pl.BlockSpec((None, 1, D_MODEL), lambda i: (layer, 0, 0)),
                  mspec(sh_col), mspec(sc_col),
                  pl.BlockSpec((D_MODEL, ROUTE_LANES), lambda i: (0, 0)),
                  pl.BlockSpec((1, ROUTE_LANES), lambda i: (0, 0))],
        out_specs=[pl.BlockSpec((tm, MOE_ROW_TILES, HEAD_DIM), lambda i: (i, 0, 0)),
                   pl.BlockSpec((tm, ROUTE_LANES), lambda i: (i, 0)),
                   pl.BlockSpec((1, ROUTE_LANES), lambda i: (0, 0))],
        out_shape=[jax.ShapeDtypeStruct((t, MOE_ROW_TILES, HEAD_DIM), jnp.uint32),
                   jax.ShapeDtypeStruct((t, ROUTE_LANES), F32),
                   jax.ShapeDtypeStruct((1, ROUTE_LANES), F32)],
        scratch_shapes=[pltpu.VMEM((1, ROUTE_LANES), F32)],
        compiler_params=_cp("arbitrary"),
        name="moe_route",
    )(x, g_all.reshape(DEPTH, 1, D_MODEL), mod, mod, w_route, b_route)


def moe_plan(route, counts):
    cnt = counts[0, MOE_GROUPS:MOE_GROUPS + MOE_EXPERTS].astype(jnp.int32)
    padded = ((cnt + MOE_TILE - 1) // MOE_TILE) * MOE_TILE
    pad_end = jnp.cumsum(padded)
    pad_off = pad_end - padded
    e = route[:, ROUTE_E0:ROUTE_E1 + 1].astype(jnp.int32)
    rank = route[:, ROUTE_R0:ROUTE_R1 + 1].astype(jnp.int32)
    hot = e[:, :, None] == jnp.arange(MOE_EXPERTS, dtype=jnp.int32)[None, None, :]
    dest = jnp.sum(jnp.where(hot, pad_off[None, None, :], 0), axis=-1) + rank
    n_used = (pad_end[-1] // MOE_TILE).astype(jnp.int32).reshape(1)
    tile_start = jnp.arange(MOE_NT, dtype=jnp.int32) * MOE_TILE
    tile_expert = jnp.sum((pad_end[None, :] <= tile_start[:, None]).astype(jnp.int32), axis=1)
    tile_expert = jnp.minimum(tile_expert, MOE_EXPERTS - 1).astype(jnp.int32)
    tok = jnp.zeros((MOE_NT * MOE_TILE,), jnp.int32).at[dest.reshape(-1)].set(
        jnp.arange(dest.size, dtype=jnp.int32) // 2)
    return dest, tok, tile_expert, n_used


def _dispatch_kernel(nu_ref, tok_ref, tokn_ref, h_hbm, o_ref, buf, sem):
    i = pl.program_id(0)
    nu = nu_ref[0]

    def row_copy(t, slot, r):
        return pltpu.make_async_copy(h_hbm.at[pl.ds(t, 1)], buf.at[slot, pl.ds(r, 1)], sem.at[slot])

    def issue(tref, slot):
        def body(r2, c):
            row_copy(tref[0, 0, 2 * r2], slot, 2 * r2).start(priority=0)
            row_copy(tref[0, 0, 2 * r2 + 1], slot, 2 * r2 + 1).start(priority=1)
            return c
        lax.fori_loop(0, MOE_TILE // 2, body, 0, unroll=4)

    @pl.when(i == 0)
    def _():
        issue(tok_ref, 0)

    @pl.when(i + 1 < nu)
    def _():
        issue(tokn_ref, (i + 1) % 2)

    @pl.when(i < nu)
    def _():
        slot = i % 2

        def drain(r, c):
            row_copy(0, slot, r).wait()
            return c
        lax.fori_loop(0, MOE_TILE, drain, 0, unroll=8)
        o_ref[...] = buf[slot]

    @pl.when(i >= nu)
    def _():
        o_ref[...] = jnp.zeros_like(o_ref)


def moe_dispatch(h, tok, n_used):
    smem_tile = lambda off: pl.BlockSpec(
        (1, 1, MOE_TILE), lambda i, nu: (jnp.minimum(i + off, MOE_NT - 1), 0, 0), memory_space=pltpu.SMEM)
    grid_spec = pltpu.PrefetchScalarGridSpec(
        num_scalar_prefetch=1,
        grid=(MOE_NT,),
        in_specs=[smem_tile(0), smem_tile(1), pl.BlockSpec(memory_space=pl.ANY)],
        out_specs=pl.BlockSpec((MOE_TILE,) + h.shape[1:], lambda i, nu: (i, 0, 0)),
        scratch_shapes=[pltpu.VMEM((2, MOE_TILE) + h.shape[1:], h.dtype), pltpu.SemaphoreType.DMA((2,))])
    tok3 = tok.reshape(MOE_NT, 1, MOE_TILE)
    return pl.pallas_call(
        _dispatch_kernel,
        grid_spec=grid_spec,
        out_shape=jax.ShapeDtypeStruct((MOE_NT * MOE_TILE,) + h.shape[1:], h.dtype),
        compiler_params=_cp("arbitrary"),
        name="moe_dispatch",
    )(n_used, tok3, tok3, h)


def _expert_kernel(te_ref, nu_ref, hs_ref, wg_ref, wu_ref, wd_ref, o_ref, wgb, wub, wdb):
    i = pl.program_id(0)

    @pl.when(i < nu_ref[0])
    def _():
        @pl.when((i == 0) | (te_ref[i] != te_ref[jnp.maximum(i - 1, 0)]))
        def _():
            wgb[...] = wg_ref[...].astype(BF16)
            wub[...] = wu_ref[...].astype(BF16)
            wdb[...] = wd_ref[...].astype(BF16)

        w = jnp.concatenate([hs_ref[:, j, :] for j in range(MOE_ROW_TILES)], axis=1)
        lo = lax.bitcast_convert_type(w << 16, F32).astype(BF16)
        hi = lax.bitcast_convert_type(w & jnp.uint32(0xFFFF0000), F32).astype(BF16)
        h = jnp.concatenate([lo, hi], axis=1)
        g = jnp.dot(h, wgb[...], preferred_element_type=F32)
        u = jnp.dot(h, wub[...], preferred_element_type=F32)
        hid = (jax.nn.silu(g) * u).astype(BF16)
        o_ref[...] = jnp.dot(hid, wdb[...], preferred_element_type=F32)

    @pl.when(i >= nu_ref[0])
    def _():
        o_ref[...] = jnp.zeros_like(o_ref)


def moe_experts(hs, tile_expert, n_used, w_gate, w_up, w_down, layer):
    def wspec(shape):
        return pl.BlockSpec((None, None, None) + shape,
                            lambda i, te, nu: (layer, te[i] // MOE_EXPERTS_PER_GROUP,
                                               te[i] % MOE_EXPERTS_PER_GROUP, 0, 0))

    grid_spec = pltpu.PrefetchScalarGridSpec(
        num_scalar_prefetch=2,
        grid=(MOE_NT,),
        in_specs=[pl.BlockSpec((MOE_TILE, MOE_ROW_TILES, HEAD_DIM),
                               lambda i, te, nu: (jnp.minimum(i, nu[0] - 1), 0, 0)),
                  wspec((D_MODEL, MOE_FFN)), wspec((D_MODEL, MOE_FFN)), wspec((MOE_FFN, D_MODEL))],
        out_specs=pl.BlockSpec((MOE_TILE, D_MODEL), lambda i, te, nu: (i, 0)),
        scratch_shapes=[pltpu.VMEM((D_MODEL, MOE_FFN), BF16),
                        pltpu.VMEM((D_MODEL, MOE_FFN), BF16),
                        pltpu.VMEM((MOE_FFN, D_MODEL), BF16)])
    return pl.pallas_call(
        _expert_kernel,
        grid_spec=grid_spec,
        out_shape=jax.ShapeDtypeStruct((MOE_NT * MOE_TILE, D_MODEL), F32),
        compiler_params=_cp("arbitrary"),
        name="moe_experts",
    )(tile_expert, n_used, hs, w_gate, w_up, w_down)


def _combine_kernel(dest_ref, destn_ref, ys_hbm, r_ref, x_ref, gate_ref, o_ref, buf, sem):
    i = pl.program_id(0)
    n = pl.num_programs(0)
    tm = x_ref.shape[0]

    def row_copy(d, slot, k, r):
        return pltpu.make_async_copy(ys_hbm.at[pl.ds(d, 1), :], buf.at[slot, k, pl.ds(r, 1), :], sem.at[slot])

    def issue(dref, slot):
        def body(r, c):
            row_copy(dref[0, 0, 2 * r], slot, 0, r).start(priority=0)
            row_copy(dref[0, 0, 2 * r + 1], slot, 1, r).start(priority=1)
            return c
        lax.fori_loop(0, tm, body, 0, unroll=8)

    @pl.when(i == 0)
    def _():
        issue(dest_ref, 0)

    @pl.when(i + 1 < n)
    def _():
        issue(destn_ref, (i + 1) % 2)

    slot = i % 2

    def drain(r, c):
        row_copy(0, slot, 0, r).wait()
        row_copy(0, slot, 1, r).wait()
        return c
    lax.fori_loop(0, tm, drain, 0, unroll=8)
    r = r_ref[...]
    y = r[:, ROUTE_W0:ROUTE_W0 + 1] * buf[slot, 0] + r[:, ROUTE_W1:ROUTE_W1 + 1] * buf[slot, 1]
    o_ref[...] = x_ref[...] + gate_ref[...] * y


def moe_combine(ys, dest, route, x, mod, layer, gate_col, tm=256):
    t = x.shape[0]
    row = pl.BlockSpec((tm, D_MODEL), lambda i: (i, 0))
    nblk = t // tm
    smem = lambda off: pl.BlockSpec((1, 1, 2 * tm), lambda i: (jnp.minimum(i + off, nblk - 1), 0, 0),
                                    memory_space=pltpu.SMEM)
    dest3 = dest.reshape(nblk, 1, 2 * tm)
    return pl.pallas_call(
        _combine_kernel,
        grid=(nblk,),
        in_specs=[smem(0), smem(1),
                  pl.BlockSpec(memory_space=pl.ANY),
                  pl.BlockSpec((tm, ROUTE_LANES), lambda i: (i, 0)),
                  row,
                  pl.BlockSpec((None, 1, D_MODEL),
                               lambda i: (layer * MOD_ROWS + _mod_row(i, tm), 0, gate_col))],
        out_specs=row,
        out_shape=jax.ShapeDtypeStruct((t, D_MODEL), F32),
        scratch_shapes=[pltpu.VMEM((2, 2, tm, D_MODEL), F32), pltpu.SemaphoreType.DMA((2,))],
        compiler_params=_cp("arbitrary"),
        name="moe_combine",
    )(dest3, dest3, ys, route, x, mod)


def _final_norm_kernel(x_ref, g_ref, o_ref):
    x = x_ref[...]
    o_ref[...] = x * lax.rsqrt(jnp.mean(x * x, axis=-1, keepdims=True) + RMS_EPS) * g_ref[...]


def final_norm(x, g, rows, tm=512):
    row = pl.BlockSpec((tm, D_MODEL), lambda i: (i, 0))
    return pl.pallas_call(
        _final_norm_kernel,
        grid=(rows // tm,),
        in_specs=[row, pl.BlockSpec((1, D_MODEL), lambda i: (0, 0))],
        out_specs=row,
        out_shape=jax.ShapeDtypeStruct((rows, D_MODEL), F32),
        compiler_params=_cp("arbitrary"),
        name="final_norm",
    )(x, g.reshape(1, D_MODEL))


def attention_layer(xs, mod, layer, norm1_g, w_in, w_out, na_bias, q_gain, k_gain, cos_t, sin_t):
    li = layer // 2
    p = norm_mod_matmul(xs, norm1_g, mod, w_in, layer, li, 0, 1)
    gains = jnp.stack([q_gain[li], k_gain[li]]).reshape(2, 1, HEAD_DIM)
    qkv = gqa_prep(p, gains, cos_t, sin_t)
    kcol, vcol = GQA_Q_HEADS, GQA_Q_HEADS + GQA_KV_HEADS
    grp = GQA_Q_HEADS // GQA_KV_HEADS
    ctx0 = T_LAT // CTX_LEN

    oa = na_attention(p, na_bias[li])
    tq = 256
    nq = SEQ // tq
    ob = attention(
        qkv, lambda b, k, q: (b * nq + q, k),
        [qkv, qkv],
        [lambda b, k, q: (b, kcol + k), lambda b, k, q: (ctx0 + b, kcol + k)],
        [lambda b, k, q: (b, vcol + k), lambda b, k, q: (ctx0 + b, vcol + k)],
        [SEQ, CTX_LEN], (BATCH, GQA_KV_HEADS, nq), tq, grp, 1.0,
        T_LAT, GQA_Q_HEADS * HEAD_DIM, lambda b, k, q: (b * nq + q, k), "gqa_latent")
    oac = attention(
        p, lambda b, h: (ctx0 + b, h),
        [p],
        [lambda b, h: (ctx0 + b, NA_HEADS + h)],
        [lambda b, h: (ctx0 + b, 2 * NA_HEADS + h)],
        [CTX_LEN], (BATCH, NA_HEADS), CTX_LEN, 1, ATT_SCALE * LOG2E,
        T_CTX, NA_HEADS * HEAD_DIM, lambda b, h: (b, h), "na_context")
    obc = attention(
        qkv, lambda b, k: (ctx0 + b, k),
        [qkv],
        [lambda b, k: (ctx0 + b, kcol + k)],
        [lambda b, k: (ctx0 + b, vcol + k)],
        [CTX_LEN], (BATCH, GQA_KV_HEADS), CTX_LEN, grp, 1.0,
        T_CTX, GQA_Q_HEADS * HEAD_DIM, lambda b, k: (b, k), "gqa_context")
    oa_all = jnp.concatenate([oa, oac], axis=0)
    ob_all = jnp.concatenate([ob, obc], axis=0)
    return proj_residual(oa_all, ob_all, w_out, li, xs, mod, layer, 2)


def s5_layer(xs, mod, layer, norm1_g, w_in, chunk_ops, d_skip, w_glu):
    li = layer // 2
    u = norm_mod_matmul(xs, norm1_g, mod, w_in, layer, li, 0, 1)
    wst, wy, a1, a2 = (w[li] for w in chunk_ops)
    perm = s5_lane_permutation()
    y_lat, y_ctx = s5_unpack(s5_core(s5_pack(u, perm), wst, wy, a1, a2), perm.T)
    y = jnp.concatenate([y_lat, y_ctx], axis=0)
    return glu_residual(u, y, d_skip, w_glu, li, xs, mod, layer, 2)


def moe_layer(xs, mod, layer, norm2_g, w_coarse, b_coarse, w_fine, b_fine, w_gate, w_up, w_down):
    wf = w_fine[layer].transpose(1, 0, 2).reshape(D_MODEL, MOE_EXPERTS)
    w_route = jnp.concatenate([w_coarse[layer], wf], axis=1).astype(F32)
    w_route = jnp.pad(w_route, ((0, 0), (0, ROUTE_LANES - w_route.shape[1])))
    b_route = jnp.concatenate([b_coarse[layer], b_fine[layer].reshape(-1)]).astype(F32)
    b_route = jnp.pad(b_route, (0, ROUTE_LANES - b_route.shape[0])).reshape(1, ROUTE_LANES)
    h, route, counts = moe_route(xs, norm2_g, mod, layer, 3, 4, w_route, b_route)
    dest, tok, tile_expert, n_used = moe_plan(route, counts)
    hs = moe_dispatch(h, tok, n_used)
    ys = moe_experts(hs, tile_expert, n_used, w_gate, w_up, w_down, layer)
    return moe_combine(ys, dest, route, xs, mod, layer, 5)


def kernel(x, c, ctx, c_ctx, ada_w, ada_b, norm1_g, norm2_g, final_g, attn_w_in, attn_w_out, na_rpb, q_gain, k_gain, s5_w_in, s5_lam_re, s5_lam_im, s5_log_dt, s5_b_re, s5_b_im, s5_c_re, s5_c_im, s5_d, s5_w_glu, moe_w_coarse, moe_b_coarse, moe_w_fine, moe_b_fine, moe_w_gate, moe_w_up, moe_w_down):
    xs = jnp.concatenate([x.reshape(T_LAT, D_MODEL), ctx.reshape(T_CTX, D_MODEL)], axis=0)
    c8 = jnp.concatenate([c, c_ctx[None, :], jnp.zeros((MOD_ROWS - BATCH - 1, D_MODEL), F32)], axis=0)
    mod = ada_mod(c8, ada_w, ada_b).reshape(DEPTH * MOD_ROWS, 1, 6 * D_MODEL)
    cos_t, sin_t = rope_tables()
    attn_w_in, attn_w_out, s5_w_in, s5_w_glu = (w.astype(BF16) for w in (attn_w_in, attn_w_out, s5_w_in, s5_w_glu))
    chunk_ops = jax.vmap(s5_weights)(s5_lam_re, s5_lam_im, s5_log_dt, s5_b_re, s5_b_im, s5_c_re, s5_c_im)
    na_bias = jax.vmap(na_bias_table)(na_rpb)
    for layer in range(DEPTH):
        if layer % 2 == 0:
            xs = attention_layer(xs, mod, layer, norm1_g, attn_w_in, attn_w_out, na_bias, q_gain, k_gain,
                                 cos_t, sin_t)
        else:
            xs = s5_layer(xs, mod, layer, norm1_g, s5_w_in, chunk_ops, s5_d, s5_w_glu)
        xs = moe_layer(xs, mod, layer, norm2_g, moe_w_coarse, moe_b_coarse, moe_w_fine, moe_b_fine,
                       moe_w_gate, moe_w_up, moe_w_down)
    return final_norm(xs, final_g, T_LAT).reshape(BATCH, SEQ, D_MODEL)
```

```python
import functools
import math

import jax
import jax.numpy as jnp
from jax import lax
from jax.experimental import pallas as pl
from jax.experimental.pallas import tpu as pltpu

F32 = jnp.float32
BF16 = jnp.bfloat16

D_MODEL = 2048
BATCH = 4
SEQ = 4096
DEPTH = 4
GRID_W = 64
CTX_LEN = 256
HEAD_DIM = 128
NA_HEADS = 8
NA_ROWS = 8
NA_COLS = 16
GQA_Q_HEADS = 8
GQA_KV_HEADS = 2
ROPE_THETA = 10000.0
ROPE_AXIS_DIM = HEAD_DIM // 2
ATTN_IN_COLS = (3 * NA_HEADS + GQA_Q_HEADS + 2 * GQA_KV_HEADS) * HEAD_DIM
S5_GROUP = 16
S5_GROUPS = D_MODEL // S5_GROUP
S5_STATE = 64
MOE_GROUPS = 4
MOE_EXPERTS_PER_GROUP = 8
MOE_EXPERTS = MOE_GROUPS * MOE_EXPERTS_PER_GROUP
MOE_FFN = D_MODEL // 4
RMS_EPS = 1e-6
NEG_INF = -1e30

T_LAT = BATCH * SEQ
T_CTX = BATCH * CTX_LEN
T_ALL = T_LAT + T_CTX
MOD_ROWS = 8
ATT_SCALE = HEAD_DIM ** -0.5

S5_CHUNK = 16
S5_SEQ = CTX_LEN + SEQ
S5_NCHUNK = S5_SEQ // S5_CHUNK
S5_CW = S5_CHUNK * S5_GROUP
S5_GB = 4

MOE_TILE = 512
MOE_SLOTS = T_ALL * 2
MOE_NT = MOE_SLOTS // MOE_TILE + MOE_EXPERTS

VMEM_LIMIT = 56 * 1024 * 1024


def _cp(*sem):
    return pltpu.CompilerParams(dimension_semantics=sem, vmem_limit_bytes=VMEM_LIMIT)


def _mod_row(i, tm):
    return jnp.minimum((i * tm) // SEQ, BATCH)


def _ada_kernel(c_ref, w_ref, b_ref, o_ref):
    s = jax.nn.silu(c_ref[...])
    o_ref[...] = jnp.dot(s, w_ref[...], precision=lax.Precision.HIGHEST,
                         preferred_element_type=F32) + b_ref[...]


def ada_mod(c8, ada_w, ada_b):
    tn = 1024
    n = ada_w.shape[-1]
    return pl.pallas_call(
        _ada_kernel,
        grid=(DEPTH, n // tn),
        in_specs=[pl.BlockSpec((MOD_ROWS, D_MODEL), lambda l, j: (0, 0)),
                  pl.BlockSpec((None, D_MODEL, tn), lambda l, j: (l, 0, j)),
                  pl.BlockSpec((None, 1, tn), lambda l, j: (l, 0, j))],
        out_specs=pl.BlockSpec((None, MOD_ROWS, tn), lambda l, j: (l, 0, j)),
        out_shape=jax.ShapeDtypeStruct((DEPTH, MOD_ROWS, n), F32),
        compiler_params=_cp("arbitrary", "arbitrary"),
        name="ada_mod",
    )(c8, ada_w, ada_b.reshape(DEPTH, 1, n))


def _norm_mod(x, g, sh, sc):
    y = x * lax.rsqrt(jnp.mean(x * x, axis=-1, keepdims=True) + RMS_EPS) * g
    return y * (1.0 + sc) + sh


def _mod_spec(layer, col, tm):
    return pl.BlockSpec((None, 1, D_MODEL),
                        lambda i, j: (layer * MOD_ROWS + _mod_row(i, tm), 0, col))


def _gate_spec(layer, col, tm, tn):
    return pl.BlockSpec((None, 1, tn),
                        lambda i, j: (layer * MOD_ROWS + _mod_row(i, tm), 0, col * (D_MODEL // tn) + j))


def _nm_mm_kernel(x_ref, g_ref, sh_ref, sc_ref, w_ref, o_ref, h_ref):
    @pl.when(pl.program_id(1) == 0)
    def _():
        h_ref[...] = _norm_mod(x_ref[...], g_ref[...], sh_ref[...], sc_ref[...]).astype(BF16)

    o_ref[...] = jnp.dot(h_ref[...], w_ref[...],
                         preferred_element_type=F32).astype(o_ref.dtype)


def norm_mod_matmul(x, g_all, mod, w_all, layer, wi, sh_col, sc_col, tm=1024, tn=512):
    t = x.shape[0]
    n = w_all.shape[-1]
    return pl.pallas_call(
        _nm_mm_kernel,
        grid=(t // tm, n // tn),
        in_specs=[pl.BlockSpec((tm, D_MODEL), lambda i, j: (i, 0)),
                  pl.BlockSpec((None, 1, D_MODEL), lambda i, j: (layer, 0, 0)),
                  _mod_spec(layer, sh_col, tm),
                  _mod_spec(layer, sc_col, tm),
                  pl.BlockSpec((None, D_MODEL, tn), lambda i, j: (wi, 0, j))],
        out_specs=pl.BlockSpec((tm, tn), lambda i, j: (i, j)),
        out_shape=jax.ShapeDtypeStruct((t, n), F32),
        scratch_shapes=[pltpu.VMEM((tm, D_MODEL), BF16)],
        compiler_params=_cp("arbitrary", "arbitrary"),
        name="norm_mod_matmul",
    )(x, g_all.reshape(DEPTH, 1, D_MODEL), mod, mod, w_all)


def _proj_res_kernel(a_ref, b_ref, wa_ref, wb_ref, x_ref, gate_ref, o_ref):
    y = jnp.dot(a_ref[...], wa_ref[...], preferred_element_type=F32)
    y = y + jnp.dot(b_ref[...], wb_ref[...], preferred_element_type=F32)
    o_ref[...] = x_ref[...] + gate_ref[...] * y


def proj_residual(oa, ob, w_all, wi, x, mod, layer, gate_col, tm=1024, tn=512):
    t, ka = oa.shape
    kb = ob.shape[1]
    assert ka == kb
    return pl.pallas_call(
        _proj_res_kernel,
        grid=(t // tm, D_MODEL // tn),
        in_specs=[pl.BlockSpec((tm, ka), lambda i, j: (i, 0)),
                  pl.BlockSpec((tm, kb), lambda i, j: (i, 0)),
                  pl.BlockSpec((None, ka, tn), lambda i, j: (wi, 0, j)),
                  pl.BlockSpec((None, kb, tn), lambda i, j: (wi, 1, j)),
                  pl.BlockSpec((tm, tn), lambda i, j: (i, j)),
                  _gate_spec(layer, gate_col, tm, tn)],
        out_specs=pl.BlockSpec((tm, tn), lambda i, j: (i, j)),
        out_shape=jax.ShapeDtypeStruct((t, D_MODEL), F32),
        compiler_params=_cp("arbitrary", "arbitrary"),
        name="proj_residual",
    )(oa, ob, w_all, w_all, x, mod)


def _gqa_prep_kernel(p_ref, gain_ref, cos_ref, sin_ref, o_ref):
    nqk = GQA_Q_HEADS + GQA_KV_HEADS
    nh = nqk + GQA_KV_HEADS
    cos = cos_ref[...]
    sin = sin_ref[...]
    lane = lax.broadcasted_iota(jnp.int32, cos.shape, 1)
    half = ROPE_AXIS_DIM // 2
    first_half = (lane % ROPE_AXIS_DIM) < half
    for h in range(nh):
        cols = slice(h * HEAD_DIM, (h + 1) * HEAD_DIM)
        x = p_ref[:, cols]
        if h < nqk:
            is_q = h < GQA_Q_HEADS
            gain = gain_ref[0 if is_q else 1] * (ATT_SCALE * LOG2E if is_q else 1.0)
            y = x * lax.rsqrt(jnp.mean(x * x, axis=-1, keepdims=True) + RMS_EPS) * gain
            partner = jnp.where(first_half, pltpu.roll(y, HEAD_DIM - half, 1), pltpu.roll(y, half, 1))
            x = y * cos + partner * sin
        o_ref[:, cols] = x.astype(BF16)


def gqa_prep(p, gains, cos_t, sin_t, tm=512):
    t = p.shape[0]
    nh = GQA_Q_HEADS + 2 * GQA_KV_HEADS
    w = nh * HEAD_DIM
    assert (3 * NA_HEADS * HEAD_DIM) % w == 0
    col_blk = 3 * NA_HEADS * HEAD_DIM // w
    lat_tiles = T_LAT // tm
    per_seq = SEQ // tm

    def tab_idx(i):
        return (jnp.where(i < lat_tiles, i % per_seq, per_seq), 0)

    return pl.pallas_call(
        _gqa_prep_kernel,
        grid=(t // tm,),
        in_specs=[pl.BlockSpec((tm, w), lambda i: (i, col_blk)),
                  pl.BlockSpec((2, 1, HEAD_DIM), lambda i: (0, 0, 0)),
                  pl.BlockSpec((tm, HEAD_DIM), tab_idx),
                  pl.BlockSpec((tm, HEAD_DIM), tab_idx)],
        out_specs=pl.BlockSpec((tm, w), lambda i: (i, 0)),
        out_shape=jax.ShapeDtypeStruct((t, w), BF16),
        compiler_params=_cp("arbitrary"),
        name="gqa_prep",
    )(p, gains, cos_t, sin_t)


def rope_tables(tm=512):
    t = jnp.arange(SEQ)
    row = (t // GRID_W).astype(F32)
    col = (t % GRID_W).astype(F32)
    inv = ROPE_THETA ** (-jnp.arange(0, ROPE_AXIS_DIM, 2, dtype=F32) / ROPE_AXIS_DIM)
    ar = row[:, None] * inv[None]
    ac = col[:, None] * inv[None]
    cos_t = jnp.concatenate([jnp.cos(ar), jnp.cos(ar), jnp.cos(ac), jnp.cos(ac)], axis=-1)
    sin_t = jnp.concatenate([-jnp.sin(ar), jnp.sin(ar), -jnp.sin(ac), jnp.sin(ac)], axis=-1)
    cos_t = jnp.concatenate([cos_t, jnp.ones((tm, HEAD_DIM), F32)], axis=0)
    sin_t = jnp.concatenate([sin_t, jnp.zeros((tm, HEAD_DIM), F32)], axis=0)
    return cos_t, sin_t


LOG2E = 1.4426950408889634
ATT_KCHUNK = 512


def _softmax_chunk(q, k, v, bias, m, l, acc):
    s = lax.dot_general(q, k.astype(BF16), (((1,), (1,)), ((), ())), preferred_element_type=F32)
    if bias is not None:
        s = s + bias
    m_new = jnp.maximum(m, jnp.max(s, axis=-1, keepdims=True))
    alpha = jnp.exp2(m - m_new)
    p = jnp.exp2(s - m_new)
    l = alpha * l + jnp.sum(p, axis=-1, keepdims=True)
    acc = alpha * acc + jnp.dot(p.astype(BF16), v.astype(BF16), preferred_element_type=F32)
    return m_new, l, acc


def _attn_kernel(*refs, nseg, group, scale):
    q_ref = refs[0]
    k_refs = refs[1:1 + nseg]
    v_refs = refs[1 + nseg:1 + 2 * nseg]
    o_ref = refs[1 + 2 * nseg]
    tq = q_ref.shape[0]
    q = jnp.concatenate([q_ref[:, g * HEAD_DIM:(g + 1) * HEAD_DIM] for g in range(group)], axis=0)
    if scale != 1.0:
        q = q.astype(F32) * scale
    q = q.astype(BF16)
    m = jnp.full((q.shape[0], 1), -jnp.inf, F32)
    l = jnp.zeros((q.shape[0], 1), F32)
    acc = jnp.zeros((q.shape[0], HEAD_DIM), F32)
    for k_ref, v_ref in zip(k_refs, v_refs):
        nk = k_ref.shape[0]
        for c0 in range(0, nk, ATT_KCHUNK):
            cs = min(ATT_KCHUNK, nk - c0)
            m, l, acc = _softmax_chunk(q, k_ref[c0:c0 + cs, :], v_ref[c0:c0 + cs, :], None, m, l, acc)
    o = acc / l
    for g in range(group):
        o_ref[:, g * HEAD_DIM:(g + 1) * HEAD_DIM] = o[g * tq:(g + 1) * tq].astype(o_ref.dtype)


def attention(q_arr, q_idx, kv_arrs, k_idx, v_idx, kv_rows, grid, tq, group, scale, out_rows, out_cols, o_idx, name):
    nseg = len(kv_rows)
    in_specs = [pl.BlockSpec((tq, group * HEAD_DIM), q_idx)]
    in_specs += [pl.BlockSpec((kv_rows[i], HEAD_DIM), k_idx[i]) for i in range(nseg)]
    in_specs += [pl.BlockSpec((kv_rows[i], HEAD_DIM), v_idx[i]) for i in range(nseg)]
    return pl.pallas_call(
        functools.partial(_attn_kernel, nseg=nseg, group=group, scale=scale),
        grid=grid,
        in_specs=in_specs,
        out_specs=pl.BlockSpec((tq, group * HEAD_DIM), o_idx),
        out_shape=jax.ShapeDtypeStruct((out_rows, out_cols), BF16),
        compiler_params=_cp(*(["arbitrary"] * len(grid))),
        name=name,
    )(q_arr, *kv_arrs, *kv_arrs)


NA_QROWS = 8
NA_KROWS = 16
NA_TQ = NA_QROWS * GRID_W
NA_TK = NA_KROWS * GRID_W
GRID_H = SEQ // GRID_W


def _na_kernel(q_ref, k_ref, v_ref, kc_ref, vc_ref, bias_ref, o_ref):
    rb = pl.program_id(2)
    w0 = jnp.clip(NA_QROWS * rb - (NA_KROWS - NA_QROWS) // 2, 0, GRID_H - NA_KROWS) * GRID_W
    w0 = pl.multiple_of(w0, 4 * GRID_W)
    q = (q_ref[...] * (ATT_SCALE * LOG2E)).astype(BF16)
    m = jnp.full((NA_TQ, 1), -jnp.inf, F32)
    l = jnp.zeros((NA_TQ, 1), F32)
    acc = jnp.zeros((NA_TQ, HEAD_DIM), F32)
    m, l, acc = _softmax_chunk(q, kc_ref[...], vc_ref[...], None, m, l, acc)
    for c0 in range(0, NA_TK, ATT_KCHUNK):
        kw = k_ref[pl.ds(w0 + c0, ATT_KCHUNK), :]
        vw = v_ref[pl.ds(w0 + c0, ATT_KCHUNK), :]
        m, l, acc = _softmax_chunk(q, kw, vw, bias_ref[:, c0:c0 + ATT_KCHUNK], m, l, acc)
    o_ref[...] = (acc / l).astype(o_ref.dtype)


def na_bias_table(rpb):
    aw = jnp.arange(GRID_W)
    col_start = jnp.clip(aw - NA_COLS // 2, 0, GRID_W - NA_COLS)
    col_ok = (aw[None, :] >= col_start[:, None]) & (aw[None, :] < col_start[:, None] + NA_COLS)
    off_c = jnp.clip(aw[None, :] - aw[:, None] + (NA_COLS - 1), 0, 2 * NA_COLS - 2)
    nh, nr, nc = rpb.shape
    pick = (off_c.reshape(1, -1) == jnp.arange(nc)[:, None]).astype(F32)
    tiles = jnp.dot(rpb.astype(F32).reshape(nh * nr, nc), pick, precision=lax.Precision.HIGHEST)
    tiles = jnp.where(col_ok.reshape(1, -1), tiles * LOG2E, NEG_INF).reshape(nh, nr, GRID_W, GRID_W)
    blocked = jnp.full((nh, 1, GRID_W, GRID_W), NEG_INF, F32)
    tiles = jnp.concatenate([tiles, blocked], axis=1)
    nblk = GRID_H // NA_QROWS
    out = []
    for rb in (0, nblk // 2, nblk - 1):
        w0 = min(max(NA_QROWS * rb - (NA_KROWS - NA_QROWS) // 2, 0), GRID_H - NA_KROWS)
        picks = []
        for qr in range(NA_QROWS * rb, NA_QROWS * (rb + 1)):
            rs = min(max(qr - NA_ROWS // 2, 0), GRID_H - NA_ROWS)
            for kr in range(w0, w0 + NA_KROWS):
                picks.append(kr - qr + NA_ROWS - 1 if rs <= kr < rs + NA_ROWS else nr)
        b = jnp.stack([tiles[:, k] for k in picks], axis=1)
        b = b.reshape(nh, NA_QROWS, NA_KROWS, GRID_W, GRID_W).transpose(0, 1, 3, 2, 4)
        out.append(b.reshape(nh, NA_TQ, NA_TK))
    return jnp.stack(out, axis=1)


def na_attention(p, bias):
    nblk = GRID_H // NA_QROWS
    ctx_blk0 = T_LAT // CTX_LEN

    def pat(rb):
        return jnp.where(rb == 0, 0, jnp.where(rb == nblk - 1, 2, 1))

    return pl.pallas_call(
        _na_kernel,
        grid=(NA_HEADS, BATCH, nblk),
        in_specs=[pl.BlockSpec((NA_TQ, HEAD_DIM), lambda h, b, r: (b * nblk + r, h)),
                  pl.BlockSpec((SEQ, HEAD_DIM), lambda h, b, r: (b, NA_HEADS + h)),
                  pl.BlockSpec((SEQ, HEAD_DIM), lambda h, b, r: (b, 2 * NA_HEADS + h)),
                  pl.BlockSpec((CTX_LEN, HEAD_DIM), lambda h, b, r: (ctx_blk0 + b, NA_HEADS + h)),
                  pl.BlockSpec((CTX_LEN, HEAD_DIM), lambda h, b, r: (ctx_blk0 + b, 2 * NA_HEADS + h)),
                  pl.BlockSpec((None, None, NA_TQ, NA_TK), lambda h, b, r: (h, pat(r), 0, 0))],
        out_specs=pl.BlockSpec((NA_TQ, HEAD_DIM), lambda h, b, r: (b * nblk + r, h)),
        out_shape=jax.ShapeDtypeStruct((T_LAT, NA_HEADS * HEAD_DIM), BF16),
        compiler_params=_cp("arbitrary", "arbitrary", "arbitrary"),
        name="na_attention",
    )(p, p, p, p, p, bias)


def s5_weights(lam_re, lam_im, log_dt, b_re, b_im, c_re, c_im):
    hi = lax.Precision.HIGHEST
    L = S5_CHUNK
    lr = jnp.minimum(lam_re.astype(F32), -1e-4)
    li = lam_im.astype(F32)
    dt = jnp.exp(log_dt.astype(F32))[..., None]
    mag = jnp.exp(lr * dt)
    ar = mag * jnp.cos(li * dt)
    ai = mag * jnp.sin(li * dt)
    den = lr * lr + li * li
    xr = ar - 1.0
    fr = (xr * lr + ai * li) / den
    fi = (ai * lr - xr * li) / den
    br = b_re.astype(F32)
    bi = b_im.astype(F32)
    bbr = fr[..., None] * br - fi[..., None] * bi
    bbi = fr[..., None] * bi + fi[..., None] * br
    k = jnp.arange(L + 1, dtype=F32)[:, None, None, None]
    pm = jnp.exp(lr * dt * k)
    pr = pm * jnp.cos(li * dt * k)
    pi = pm * jnp.sin(li * dt * k)
    e_r = pr[..., None] * bbr - pi[..., None] * bbi
    e_i = pr[..., None] * bbi + pi[..., None] * bbr
    cr = c_re.astype(F32)
    ci = c_im.astype(F32)
    kern = (jnp.einsum('dgop,kdgpi->kdgoi', cr, e_r[:L], precision=hi)
            - jnp.einsum('dgop,kdgpi->kdgoi', ci, e_i[:L], precision=hi))
    s_i = jnp.arange(L)
    ng = lr.shape[1]
    lag = s_i[None, :, None] - s_i[:, None, None]
    place_f = (lag == s_i[None, None, :]).astype(F32)
    place = jnp.stack([place_f, place_f[::-1, ::-1]], axis=-1)
    toep = jnp.einsum('stkd,kdgoi->gsito', place, kern, precision=hi)
    w_toep_sum = toep.reshape(ng, S5_CW, S5_CW)
    st_r = e_r[L - 1 - s_i].transpose(1, 2, 0, 4, 3).reshape(2, ng, S5_CW, S5_STATE)
    st_i = e_i[L - 1 - s_i].transpose(1, 2, 0, 4, 3).reshape(2, ng, S5_CW, S5_STATE)
    w_state = jnp.concatenate([st_r, st_i], axis=-1)
    w_state_sw = jnp.concatenate([st_i, st_r], axis=-1)
    qr = pr[1:, :, :, None, :]
    qi = pi[1:, :, :, None, :]
    d_r = cr[None] * qr - ci[None] * qi
    d_i = cr[None] * qi + ci[None] * qr
    wo_r = d_r.transpose(1, 2, 4, 0, 3).reshape(2, ng, S5_STATE, S5_CW)
    wo_i = (-d_i).transpose(1, 2, 4, 0, 3).reshape(2, ng, S5_STATE, S5_CW)
    w_out = jnp.concatenate([wo_r, wo_i], axis=2)
    flip_rows = lambda w: w.reshape(ng, L, S5_GROUP, w.shape[-1])[:, ::-1].reshape(w.shape)
    flip_cols = lambda w: w.reshape(ng, w.shape[1], L, S5_GROUP)[:, :, ::-1].reshape(w.shape)
    wst = jnp.concatenate([w_state[0], flip_rows(w_state[1]),
                           w_state_sw[0], flip_rows(w_state_sw[1])], axis=-1).astype(BF16)
    wy = jnp.concatenate([w_toep_sum, w_out[0], flip_cols(w_out[1])], axis=1).astype(BF16)
    a1 = jnp.concatenate([pr[L], pr[L]], axis=-1)
    a2 = jnp.concatenate([-pi[L], pi[L]], axis=-1)
    per_row = lambda a: jnp.repeat(a.transpose(1, 0, 2), BATCH, axis=1)
    return wst, wy, per_row(a1), per_row(a2)


S5_CTX_CHUNKS = CTX_LEN // S5_CHUNK
S5_PACK = HEAD_DIM // S5_GROUP
S5_PACKS = S5_GROUPS // S5_PACK
S5_BROWS = S5_NCHUNK


def s5_lane_permutation():
    tl, g, hh = jnp.meshgrid(jnp.arange(S5_CHUNK), jnp.arange(S5_PACK), jnp.arange(S5_GROUP), indexing="ij")
    dst = (g * S5_CW + tl * S5_GROUP + hh).reshape(-1)
    return (dst[:, None] == jnp.arange(S5_PACK * S5_CW)[None, :]).astype(BF16)


S5_SB = 2


def _s5_pack_kernel(lat_ref, ctx_ref, p_ref, o_ref):
    cols = []
    for tl in range(S5_CHUNK):
        rows = []
        for b in range(S5_SB):
            rows.append(ctx_ref[pl.ds(b * CTX_LEN + tl, S5_CTX_CHUNKS, stride=S5_CHUNK), :])
            rows.append(lat_ref[pl.ds(b * SEQ + tl, SEQ // S5_CHUNK, stride=S5_CHUNK), :])
        cols.append(jnp.concatenate(rows, axis=0))
    xcat = jnp.concatenate(cols, axis=1).astype(BF16)
    o_ref[...] = jnp.dot(xcat, p_ref[...], preferred_element_type=F32).astype(BF16)


def s5_pack(u, perm):
    ctx0 = T_LAT // (S5_SB * CTX_LEN)
    w = S5_PACK * S5_CW
    return pl.pallas_call(
        _s5_pack_kernel,
        grid=(S5_PACKS, BATCH // S5_SB),
        in_specs=[pl.BlockSpec((S5_SB * SEQ, HEAD_DIM), lambda k, b: (b, k)),
                  pl.BlockSpec((S5_SB * CTX_LEN, HEAD_DIM), lambda k, b: (ctx0 + b, k)),
                  pl.BlockSpec((w, w), lambda k, b: (0, 0))],
        out_specs=pl.BlockSpec((None, S5_SB * S5_BROWS, w), lambda k, b: (k, b, 0)),
        out_shape=jax.ShapeDtypeStruct((S5_PACKS, BATCH * S5_BROWS, w), BF16),
        compiler_params=_cp("arbitrary", "arbitrary"),
        name="s5_pack",
    )(u, u, perm)


def _s5_kernel(u_ref, wst_ref, wy_ref, a1_ref, a2_ref, y_ref, s_ref, ssw_ref, hf_ref, hr_ref):
    gb = wst_ref.shape[0]
    nc = S5_NCHUNK
    sw = 2 * S5_STATE
    for j in range(gb):
        ug = u_ref[:, j * S5_CW:(j + 1) * S5_CW]
        s4 = jnp.dot(ug, wst_ref[j], preferred_element_type=F32)
        for b in range(BATCH):
            rows = s4[b * nc:(b + 1) * nc]
            s_ref[:, j, b, :] = rows[:, 0:sw]
            s_ref[:, j, BATCH + b, :] = rows[:, sw:2 * sw]
            ssw_ref[:, j, b, :] = rows[:, 2 * sw:3 * sw]
            ssw_ref[:, j, BATCH + b, :] = rows[:, 3 * sw:4 * sw]
    a1 = a1_ref[...]
    a2 = a2_ref[...]
    fwd = lax.broadcasted_iota(jnp.int32, (gb, 2 * BATCH, sw), 1) < BATCH

    def step(i, carry):
        h, hs = carry
        ri = jnp.where(i < S5_CTX_CHUNKS, S5_CTX_CHUNKS - 1 - i, nc - 1 + S5_CTX_CHUNKS - i)
        hf_ref[i] = h
        hr_ref[ri] = h
        s = jnp.where(fwd, s_ref[i], s_ref[ri])
        ssw = jnp.where(fwd, ssw_ref[i], ssw_ref[ri])
        return a1 * h + a2 * hs + s, a1 * hs - a2 * h + ssw

    z = jnp.zeros((gb, 2 * BATCH, sw), F32)
    lax.fori_loop(0, nc, step, (z, z), unroll=4)
    for j in range(gb):
        ug = u_ref[:, j * S5_CW:(j + 1) * S5_CW]
        hf = jnp.concatenate([hf_ref[:, j, b, :] for b in range(BATCH)], axis=0)
        hr = jnp.concatenate([hr_ref[:, j, BATCH + b, :] for b in range(BATCH)], axis=0)
        lhs = jnp.concatenate([ug, hf.astype(BF16), hr.astype(BF16)], axis=1)
        y_ref[:, j * S5_CW:(j + 1) * S5_CW] = jnp.dot(lhs, wy_ref[j], preferred_element_type=F32)


def s5_core(u_packed, wst, wy, a1, a2):
    gb = S5_GB
    per_pack = S5_PACK // gb
    rows = u_packed.shape[1]
    sw = 2 * S5_STATE
    blk = lambda *shape: pl.BlockSpec((gb,) + shape, lambda i: (i,) + (0,) * len(shape))
    io = pl.BlockSpec((None, rows, gb * S5_CW), lambda i: (i // per_pack, 0, i % per_pack))
    state = pltpu.VMEM((S5_NCHUNK, gb, 2 * BATCH, sw), F32)
    return pl.pallas_call(
        _s5_kernel,
        grid=(S5_GROUPS // gb,),
        in_specs=[io, blk(S5_CW, 4 * sw), blk(S5_CW + 2 * sw, S5_CW), blk(2 * BATCH, sw), blk(2 * BATCH, sw)],
        out_specs=io,
        out_shape=jax.ShapeDtypeStruct(u_packed.shape, F32),
        scratch_shapes=[state, state, state, state],
        compiler_params=_cp("arbitrary"),
        name="s5_core",
    )(u_packed, wst, wy, a1, a2)


def _s5_unpack_kernel(y_ref, pt_ref, lat_ref, ctx_ref):
    y = y_ref[...]
    hi = y.astype(BF16)
    lo = (y - hi.astype(F32)).astype(BF16)
    yn = (jnp.dot(hi, pt_ref[...], preferred_element_type=F32)
          + jnp.dot(lo, pt_ref[...], preferred_element_type=F32))
    for tl in range(S5_CHUNK):
        piece = yn[:, tl * HEAD_DIM:(tl + 1) * HEAD_DIM]
        for b in range(S5_SB):
            r0 = b * S5_BROWS
            ctx_ref[pl.ds(b * CTX_LEN + tl, S5_CTX_CHUNKS, stride=S5_CHUNK), :] = piece[r0:r0 + S5_CTX_CHUNKS]
            lat_ref[pl.ds(b * SEQ + tl, SEQ // S5_CHUNK, stride=S5_CHUNK), :] = (
                piece[r0 + S5_CTX_CHUNKS:r0 + S5_BROWS])


def s5_unpack(y_packed, perm_t):
    w = S5_PACK * S5_CW
    return pl.pallas_call(
        _s5_unpack_kernel,
        grid=(S5_PACKS, BATCH // S5_SB),
        in_specs=[pl.BlockSpec((None, S5_SB * S5_BROWS, w), lambda k, b: (k, b, 0)),
                  pl.BlockSpec((w, w), lambda k, b: (0, 0))],
        out_specs=[pl.BlockSpec((S5_SB * SEQ, HEAD_DIM), lambda k, b: (b, k)),
                   pl.BlockSpec((S5_SB * CTX_LEN, HEAD_DIM), lambda k, b: (b, k))],
        out_shape=[jax.ShapeDtypeStruct((T_LAT, D_MODEL), F32), jax.ShapeDtypeStruct((T_CTX, D_MODEL), F32)],
        compiler_params=_cp("arbitrary", "arbitrary"),
        name="s5_unpack",
    )(y_packed, perm_t)


def _glu_kernel(u_ref, y_ref, d_ref, wa_ref, wg_ref, x_ref, gate_ref, o_ref, h_ref):
    @pl.when(pl.program_id(1) == 0)
    def _():
        y = u_ref[...] * d_ref[...] + y_ref[...]
        h_ref[...] = jax.nn.gelu(y).astype(BF16)

    h = h_ref[...]
    za = jnp.dot(h, wa_ref[...], preferred_element_type=F32)
    zg = jnp.dot(h, wg_ref[...], preferred_element_type=F32)
    o_ref[...] = x_ref[...] + gate_ref[...] * (za * jax.nn.sigmoid(zg))


def glu_residual(u, y, d_all, w_all, wi, x, mod, layer, gate_col, tm=1024, tn=256):
    t = u.shape[0]
    nj = D_MODEL // tn
    row = pl.BlockSpec((tm, D_MODEL), lambda i, j: (i, 0))
    return pl.pallas_call(
        _glu_kernel,
        grid=(t // tm, nj),
        in_specs=[row, row,
                  pl.BlockSpec((None, 1, D_MODEL), lambda i, j: (wi, 0, 0)),
                  pl.BlockSpec((None, D_MODEL, tn), lambda i, j: (wi, 0, j)),
                  pl.BlockSpec((None, D_MODEL, tn), lambda i, j: (wi, 0, nj + j)),
                  pl.BlockSpec((tm, tn), lambda i, j: (i, j)),
                  _gate_spec(layer, gate_col, tm, tn)],
        out_specs=pl.BlockSpec((tm, tn), lambda i, j: (i, j)),
        out_shape=jax.ShapeDtypeStruct((t, D_MODEL), F32),
        scratch_shapes=[pltpu.VMEM((tm, D_MODEL), BF16)],
        compiler_params=_cp("arbitrary", "arbitrary"),
        name="glu_residual",
    )(u, y, d_all.reshape(-1, 1, D_MODEL), w_all, w_all, x, mod)


ROUTE_LANES = 128
MOE_ROW_TILES = D_MODEL // 2 // HEAD_DIM


def _route_kernel(x_ref, g_ref, sh_ref, sc_ref, wr_ref, br_ref, h_ref, r_ref, cnt_ref, carry_ref):
    @pl.when(pl.program_id(0) == 0)
    def _():
        carry_ref[...] = jnp.zeros_like(carry_ref)

    h = _norm_mod(x_ref[...], g_ref[...], sh_ref[...], sc_ref[...])
    bits = lax.bitcast_convert_type(h.astype(BF16).astype(F32), jnp.uint32)
    half = h.shape[1] // 2
    packed = (bits[:, :half] >> 16) | (bits[:, half:] & jnp.uint32(0xFFFF0000))
    for j in range(half // HEAD_DIM):
        h_ref[:, j, :] = packed[:, j * HEAD_DIM:(j + 1) * HEAD_DIM]
    lg = jnp.dot(h, wr_ref[...], precision=lax.Precision.HIGHEST, preferred_element_type=F32) + br_ref[...]
    lane = lax.broadcasted_iota(jnp.int32, lg.shape, 1)
    ninf = -jnp.inf
    coarse = lane < MOE_GROUPS
    lc = jnp.where(coarse, lg, ninf)
    mc = jnp.max(lc, axis=-1, keepdims=True)
    g_sel = jnp.min(jnp.where(lc == mc, lane, ROUTE_LANES), axis=-1, keepdims=True)
    p_sel = 1.0 / jnp.sum(jnp.where(coarse, jnp.exp(lc - mc), 0.0), axis=-1, keepdims=True)
    lo = MOE_GROUPS + MOE_EXPERTS_PER_GROUP * g_sel
    lf = jnp.where((lane >= lo) & (lane < lo + MOE_EXPERTS_PER_GROUP), lg, ninf)
    v0 = jnp.max(lf, axis=-1, keepdims=True)
    i0 = jnp.min(jnp.where(lf == v0, lane, ROUTE_LANES), axis=-1, keepdims=True)
    lf2 = jnp.where(lane == i0, ninf, lf)
    v1 = jnp.max(lf2, axis=-1, keepdims=True)
    i1 = jnp.min(jnp.where(lf2 == v1, lane, ROUTE_LANES), axis=-1, keepdims=True)
    e1 = jnp.exp(v1 - v0)
    w0 = p_sel / (1.0 + e1)
    w1 = w0 * e1
    tm = lg.shape[0]
    lower = (lax.broadcasted_iota(jnp.int32, (tm, tm), 1)
             < lax.broadcasted_iota(jnp.int32, (tm, tm), 0)).astype(BF16)
    carry = carry_ref[...]
    hot0 = (lane == i0).astype(F32)
    before0 = carry + jnp.dot(lower, hot0.astype(BF16), preferred_element_type=F32)
    rank0 = jnp.sum(hot0 * before0, axis=-1, keepdims=True)
    carry = carry + jnp.sum(hot0, axis=0, keepdims=True)
    hot1 = (lane == i1).astype(F32)
    before1 = carry + jnp.dot(lower, hot1.astype(BF16), preferred_element_type=F32)
    rank1 = jnp.sum(hot1 * before1, axis=-1, keepdims=True)
    carry = carry + jnp.sum(hot1, axis=0, keepdims=True)
    carry_ref[...] = carry
    cnt_ref[...] = carry
    cols = [(i0 - MOE_GROUPS).astype(F32), (i1 - MOE_GROUPS).astype(F32), w0, w1, rank0, rank1]
    r = jnp.zeros_like(lg)
    for c, val in enumerate(cols):
        r = jnp.where(lane == c, val, r)
    r_ref[...] = r


ROUTE_E0, ROUTE_E1, ROUTE_W0, ROUTE_W1, ROUTE_R0, ROUTE_R1 = range(6)


def moe_route(x, g_all, mod, layer, sh_col, sc_col, w_route, b_route, tm=512):
    t = x.shape[0]
    mspec = lambda col: pl.BlockSpec((None, 1, D_MODEL),
                                     lambda i: (layer * MOD_ROWS + _mod_row(i, tm), 0, col))
    return pl.pallas_call(
        _route_kernel,
        grid=(t // tm,),
        in_specs=[pl.BlockSpec((tm, D_MODEL), lambda i: (i, 0)),
                  pl.BlockSpec((None, 1, D_MODEL), lambda i: (layer, 0, 0)),
                  mspec(sh_col), mspec(sc_col),
                  pl.BlockSpec((D_MODEL, ROUTE_LANES), lambda i: (0, 0)),
                  pl.BlockSpec((1, ROUTE_LANES), lambda i: (0, 0))],
        out_specs=[pl.BlockSpec((tm, MOE_ROW_TILES, HEAD_DIM), lambda i: (i, 0, 0)),
                   pl.BlockSpec((tm, ROUTE_LANES), lambda i: (i, 0)),
                   pl.BlockSpec((1, ROUTE_LANES), lambda i: (0, 0))],
        out_shape=[jax.ShapeDtypeStruct((t, MOE_ROW_TILES, HEAD_DIM), jnp.uint32),
                   jax.ShapeDtypeStruct((t, ROUTE_LANES), F32),
                   jax.ShapeDtypeStruct((1, ROUTE_LANES), F32)],
        scratch_shapes=[pltpu.VMEM((1, ROUTE_LANES), F32)],
        compiler_params=_cp("arbitrary"),
        name="moe_route",
    )(x, g_all.reshape(DEPTH, 1, D_MODEL), mod, mod, w_route, b_route)


def moe_plan(route, counts):
    cnt = counts[0, MOE_GROUPS:MOE_GROUPS + MOE_EXPERTS].astype(jnp.int32)
    padded = ((cnt + MOE_TILE - 1) // MOE_TILE) * MOE_TILE
    pad_end = jnp.cumsum(padded)
    pad_off = pad_end - padded
    e = route[:, ROUTE_E0:ROUTE_E1 + 1].astype(jnp.int32)
    rank = route[:, ROUTE_R0:ROUTE_R1 + 1].astype(jnp.int32)
    hot = e[:, :, None] == jnp.arange(MOE_EXPERTS, dtype=jnp.int32)[None, None, :]
    dest = jnp.sum(jnp.where(hot, pad_off[None, None, :], 0), axis=-1) + rank
    n_used = (pad_end[-1] // MOE_TILE).astype(jnp.int32).reshape(1)
    tile_start = jnp.arange(MOE_NT, dtype=jnp.int32) * MOE_TILE
    tile_expert = jnp.sum((pad_end[None, :] <= tile_start[:, None]).astype(jnp.int32), axis=1)
    tile_expert = jnp.minimum(tile_expert, MOE_EXPERTS - 1).astype(jnp.int32)
    return dest, tile_expert, n_used


def _dispatch_kernel(dest_ref, h_ref, zero_hbm, hs_hbm, sem):
    del zero_hbm
    tm = h_ref.shape[0]

    def row_copy(r, d):
        return pltpu.make_async_copy(h_ref.at[pl.ds(r, 1)], hs_hbm.at[pl.ds(d, 1)], sem)

    def issue(r, c):
        row_copy(r, dest_ref[0, 0, 2 * r]).start(priority=0)
        row_copy(r, dest_ref[0, 0, 2 * r + 1]).start(priority=1)
        return c
    lax.fori_loop(0, tm, issue, 0, unroll=8)

    def drain(r, c):
        row_copy(r, 0).wait()
        row_copy(r, 0).wait()
        return c
    lax.fori_loop(0, tm, drain, 0, unroll=8)


def moe_dispatch(h, dest, tm=512):
    t = h.shape[0]
    zeros = jnp.zeros((MOE_NT * MOE_TILE,) + h.shape[1:], h.dtype)
    return pl.pallas_call(
        _dispatch_kernel,
        grid=(t // tm,),
        in_specs=[pl.BlockSpec((1, 1, 2 * tm), lambda i: (i, 0, 0), memory_space=pltpu.SMEM),
                  pl.BlockSpec((tm,) + h.shape[1:], lambda i: (i, 0, 0)),
                  pl.BlockSpec(memory_space=pl.ANY)],
        out_specs=pl.BlockSpec(memory_space=pl.ANY),
        out_shape=jax.ShapeDtypeStruct(zeros.shape, h.dtype),
        scratch_shapes=[pltpu.SemaphoreType.DMA],
        input_output_aliases={2: 0},
        compiler_params=_cp("arbitrary"),
        name="moe_dispatch",
    )(dest.reshape(t // tm, 1, 2 * tm), h, zeros)


def _expert_kernel(te_ref, nu_ref, hs_ref, wg_ref, wu_ref, wd_ref, o_ref, wgb, wub, wdb):
    i = pl.program_id(0)

    @pl.when(i < nu_ref[0])
    def _():
        @pl.when((i == 0) | (te_ref[i] != te_ref[jnp.maximum(i - 1, 0)]))
        def _():
            wgb[...] = wg_ref[...].astype(BF16)
            wub[...] = wu_ref[...].astype(BF16)
            wdb[...] = wd_ref[...].astype(BF16)

        w = jnp.concatenate([hs_ref[:, j, :] for j in range(MOE_ROW_TILES)], axis=1)
        lo = lax.bitcast_convert_type(w << 16, F32).astype(BF16)
        hi = lax.bitcast_convert_type(w & jnp.uint32(0xFFFF0000), F32).astype(BF16)
        h = jnp.concatenate([lo, hi], axis=1)
        g = jnp.dot(h, wgb[...], preferred_element_type=F32)
        u = jnp.dot(h, wub[...], preferred_element_type=F32)
        hid = (jax.nn.silu(g) * u).astype(BF16)
        o_ref[...] = jnp.dot(hid, wdb[...], preferred_element_type=F32)

    @pl.when(i >= nu_ref[0])
    def _():
        o_ref[...] = jnp.zeros_like(o_ref)


def moe_experts(hs, tile_expert, n_used, w_gate, w_up, w_down, layer):
    def wspec(shape):
        return pl.BlockSpec((None, None, None) + shape,
                            lambda i, te, nu: (layer, te[i] // MOE_EXPERTS_PER_GROUP,
                                               te[i] % MOE_EXPERTS_PER_GROUP, 0, 0))

    grid_spec = pltpu.PrefetchScalarGridSpec(
        num_scalar_prefetch=2,
        grid=(MOE_NT,),
        in_specs=[pl.BlockSpec((MOE_TILE, MOE_ROW_TILES, HEAD_DIM),
                               lambda i, te, nu: (jnp.minimum(i, nu[0] - 1), 0, 0)),
                  wspec((D_MODEL, MOE_FFN)), wspec((D_MODEL, MOE_FFN)), wspec((MOE_FFN, D_MODEL))],
        out_specs=pl.BlockSpec((MOE_TILE, D_MODEL), lambda i, te, nu: (i, 0)),
        scratch_shapes=[pltpu.VMEM((D_MODEL, MOE_FFN), BF16),
                        pltpu.VMEM((D_MODEL, MOE_FFN), BF16),
                        pltpu.VMEM((MOE_FFN, D_MODEL), BF16)])
    return pl.pallas_call(
        _expert_kernel,
        grid_spec=grid_spec,
        out_shape=jax.ShapeDtypeStruct((MOE_NT * MOE_TILE, D_MODEL), F32),
        compiler_params=_cp("arbitrary"),
        name="moe_experts",
    )(tile_expert, n_used, hs, w_gate, w_up, w_down)


def _combine_kernel(dest_ref, destn_ref, ys_hbm, r_ref, x_ref, gate_ref, o_ref, buf, sem):
    i = pl.program_id(0)
    n = pl.num_programs(0)
    tm = x_ref.shape[0]

    def row_copy(d, slot, k, r):
        return pltpu.make_async_copy(ys_hbm.at[pl.ds(d, 1), :], buf.at[slot, k, pl.ds(r, 1), :], sem.at[slot])

    def issue(dref, slot):
        def body(r, c):
            row_copy(dref[0, 0, 2 * r], slot, 0, r).start(priority=0)
            row_copy(dref[0, 0, 2 * r + 1], slot, 1, r).start(priority=1)
            return c
        lax.fori_loop(0, tm, body, 0, unroll=8)

    @pl.when(i == 0)
    def _():
        issue(dest_ref, 0)

    @pl.when(i + 1 < n)
    def _():
        issue(destn_ref, (i + 1) % 2)

    slot = i % 2

    def drain(r, c):
        row_copy(0, slot, 0, r).wait()
        row_copy(0, slot, 1, r).wait()
        return c
    lax.fori_loop(0, tm, drain, 0, unroll=8)
    r = r_ref[...]
    y = r[:, ROUTE_W0:ROUTE_W0 + 1] * buf[slot, 0] + r[:, ROUTE_W1:ROUTE_W1 + 1] * buf[slot, 1]
    o_ref[...] = x_ref[...] + gate_ref[...] * y


def moe_combine(ys, dest, route, x, mod, layer, gate_col, tm=256):
    t = x.shape[0]
    row = pl.BlockSpec((tm, D_MODEL), lambda i: (i, 0))
    nblk = t // tm
    smem = lambda off: pl.BlockSpec((1, 1, 2 * tm), lambda i: (jnp.minimum(i + off, nblk - 1), 0, 0),
                                    memory_space=pltpu.SMEM)
    dest3 = dest.reshape(nblk, 1, 2 * tm)
    return pl.pallas_call(
        _combine_kernel,
        grid=(nblk,),
        in_specs=[smem(0), smem(1),
                  pl.BlockSpec(memory_space=pl.ANY),
                  pl.BlockSpec((tm, ROUTE_LANES), lambda i: (i, 0)),
                  row,
                  pl.BlockSpec((None, 1, D_MODEL),
                               lambda i: (layer * MOD_ROWS + _mod_row(i, tm), 0, gate_col))],
        out_specs=row,
        out_shape=jax.ShapeDtypeStruct((t, D_MODEL), F32),
        scratch_shapes=[pltpu.VMEM((2, 2, tm, D_MODEL), F32), pltpu.SemaphoreType.DMA((2,))],
        compiler_params=_cp("arbitrary"),
        name="moe_combine",
    )(dest3, dest3, ys, route, x, mod)


def _final_norm_kernel(x_ref, g_ref, o_ref):
    x = x_ref[...]
    o_ref[...] = x * lax.rsqrt(jnp.mean(x * x, axis=-1, keepdims=True) + RMS_EPS) * g_ref[...]


def final_norm(x, g, rows, tm=512):
    row = pl.BlockSpec((tm, D_MODEL), lambda i: (i, 0))
    return pl.pallas_call(
        _final_norm_kernel,
        grid=(rows // tm,),
        in_specs=[row, pl.BlockSpec((1, D_MODEL), lambda i: (0, 0))],
        out_specs=row,
        out_shape=jax.ShapeDtypeStruct((rows, D_MODEL), F32),
        compiler_params=_cp("arbitrary"),
        name="final_norm",
    )(x, g.reshape(1, D_MODEL))


def attention_layer(xs, mod, layer, norm1_g, w_in, w_out, na_bias, q_gain, k_gain, cos_t, sin_t):
    li = layer // 2
    p = norm_mod_matmul(xs, norm1_g, mod, w_in, layer, li, 0, 1)
    gains = jnp.stack([q_gain[li], k_gain[li]]).reshape(2, 1, HEAD_DIM)
    qkv = gqa_prep(p, gains, cos_t, sin_t)
    kcol, vcol = GQA_Q_HEADS, GQA_Q_HEADS + GQA_KV_HEADS
    grp = GQA_Q_HEADS // GQA_KV_HEADS
    ctx0 = T_LAT // CTX_LEN

    oa = na_attention(p, na_bias[li])
    tq = 256
    nq = SEQ // tq
    ob = attention(
        qkv, lambda b, k, q: (b * nq + q, k),
        [qkv, qkv],
        [lambda b, k, q: (b, kcol + k), lambda b, k, q: (ctx0 + b, kcol + k)],
        [lambda b, k, q: (b, vcol + k), lambda b, k, q: (ctx0 + b, vcol + k)],
        [SEQ, CTX_LEN], (BATCH, GQA_KV_HEADS, nq), tq, grp, 1.0,
        T_LAT, GQA_Q_HEADS * HEAD_DIM, lambda b, k, q: (b * nq + q, k), "gqa_latent")
    oac = attention(
        p, lambda b, h: (ctx0 + b, h),
        [p],
        [lambda b, h: (ctx0 + b, NA_HEADS + h)],
        [lambda b, h: (ctx0 + b, 2 * NA_HEADS + h)],
        [CTX_LEN], (BATCH, NA_HEADS), CTX_LEN, 1, ATT_SCALE * LOG2E,
        T_CTX, NA_HEADS * HEAD_DIM, lambda b, h: (b, h), "na_context")
    obc = attention(
        qkv, lambda b, k: (ctx0 + b, k),
        [qkv],
        [lambda b, k: (ctx0 + b, kcol + k)],
        [lambda b, k: (ctx0 + b, vcol + k)],
        [CTX_LEN], (BATCH, GQA_KV_HEADS), CTX_LEN, grp, 1.0,
        T_CTX, GQA_Q_HEADS * HEAD_DIM, lambda b, k: (b, k), "gqa_context")
    oa_all = jnp.concatenate([oa, oac], axis=0)
    ob_all = jnp.concatenate([ob, obc], axis=0)
    return proj_residual(oa_all, ob_all, w_out, li, xs, mod, layer, 2)


def s5_layer(xs, mod, layer, norm1_g, w_in, chunk_ops, d_skip, w_glu):
    li = layer // 2
    u = norm_mod_matmul(xs, norm1_g, mod, w_in, layer, li, 0, 1)
    wst, wy, a1, a2 = (w[li] for w in chunk_ops)
    perm = s5_lane_permutation()
    y_lat, y_ctx = s5_unpack(s5_core(s5_pack(u, perm), wst, wy, a1, a2), perm.T)
    y = jnp.concatenate([y_lat, y_ctx], axis=0)
    return glu_residual(u, y, d_skip, w_glu, li, xs, mod, layer, 2)


def moe_layer(xs, mod, layer, norm2_g, w_coarse, b_coarse, w_fine, b_fine, w_gate, w_up, w_down):
    wf = w_fine[layer].transpose(1, 0, 2).reshape(D_MODEL, MOE_EXPERTS)
    w_route = jnp.concatenate([w_coarse[layer], wf], axis=1).astype(F32)
    w_route = jnp.pad(w_route, ((0, 0), (0, ROUTE_LANES - w_route.shape[1])))
    b_route = jnp.concatenate([b_coarse[layer], b_fine[layer].reshape(-1)]).astype(F32)
    b_route = jnp.pad(b_route, (0, ROUTE_LANES - b_route.shape[0])).reshape(1, ROUTE_LANES)
    h, route, counts = moe_route(xs, norm2_g, mod, layer, 3, 4, w_route, b_route)
    dest, tile_expert, n_used = moe_plan(route, counts)
    hs = moe_dispatch(h, dest)
    ys = moe_experts(hs, tile_expert, n_used, w_gate, w_up, w_down, layer)
    return moe_combine(ys, dest, route, xs, mod, layer, 5)


def kernel(x, c, ctx, c_ctx, ada_w, ada_b, norm1_g, norm2_g, final_g, attn_w_in, attn_w_out, na_rpb, q_gain, k_gain, s5_w_in, s5_lam_re, s5_lam_im, s5_log_dt, s5_b_re, s5_b_im, s5_c_re, s5_c_im, s5_d, s5_w_glu, moe_w_coarse, moe_b_coarse, moe_w_fine, moe_b_fine, moe_w_gate, moe_w_up, moe_w_down):
    xs = jnp.concatenate([x.reshape(T_LAT, D_MODEL), ctx.reshape(T_CTX, D_MODEL)], axis=0)
    c8 = jnp.concatenate([c, c_ctx[None, :], jnp.zeros((MOD_ROWS - BATCH - 1, D_MODEL), F32)], axis=0)
    mod = ada_mod(c8, ada_w, ada_b).reshape(DEPTH * MOD_ROWS, 1, 6 * D_MODEL)
    cos_t, sin_t = rope_tables()
    attn_w_in, attn_w_out, s5_w_in, s5_w_glu = (w.astype(BF16) for w in (attn_w_in, attn_w_out, s5_w_in, s5_w_glu))
    chunk_ops = jax.vmap(s5_weights)(s5_lam_re, s5_lam_im, s5_log_dt, s5_b_re, s5_b_im, s5_c_re, s5_c_im)
    na_bias = jax.vmap(na_bias_table)(na_rpb)
    for layer in range(DEPTH):
        if layer % 2 == 0:
            xs = attention_layer(xs, mod, layer, norm1_g, attn_w_in, attn_w_out, na_bias, q_gain, k_gain,
                                 cos_t, sin_t)
        else:
            xs = s5_layer(xs, mod, layer, norm1_g, s5_w_in, chunk_ops, s5_d, s5_w_glu)
        xs = moe_layer(xs, mod, layer, norm2_g, moe_w_coarse, moe_b_coarse, moe_w_fine, moe_b_fine,
                       moe_w_gate, moe_w_up, moe_w_down)
    return final_norm(xs, final_g, T_LAT).reshape(BATCH, SEQ, D_MODEL)
```

```python
import functools
import math

import jax
import jax.numpy as jnp
from jax import lax
from jax.experimental import pallas as pl
from jax.experimental.pallas import tpu as pltpu

F32 = jnp.float32
BF16 = jnp.bfloat16

D_MODEL = 2048
BATCH = 4
SEQ = 4096
DEPTH = 4
GRID_W = 64
CTX_LEN = 256
HEAD_DIM = 128
NA_HEADS = 8
NA_ROWS = 8
NA_COLS = 16
GQA_Q_HEADS = 8
GQA_KV_HEADS = 2
ROPE_THETA = 10000.0
ROPE_AXIS_DIM = HEAD_DIM // 2
ATTN_IN_COLS = (3 * NA_HEADS + GQA_Q_HEADS + 2 * GQA_KV_HEADS) * HEAD_DIM
S5_GROUP = 16
S5_GROUPS = D_MODEL // S5_GROUP
S5_STATE = 64
MOE_GROUPS = 4
MOE_EXPERTS_PER_GROUP = 8
MOE_EXPERTS = MOE_GROUPS * MOE_EXPERTS_PER_GROUP
MOE_FFN = D_MODEL // 4
RMS_EPS = 1e-6
NEG_INF = -1e30

T_LAT = BATCH * SEQ
T_CTX = BATCH * CTX_LEN
T_ALL = T_LAT + T_CTX
MOD_ROWS = 8
ATT_SCALE = HEAD_DIM ** -0.5

S5_CHUNK = 16
S5_SEQ = CTX_LEN + SEQ
S5_NCHUNK = S5_SEQ // S5_CHUNK
S5_CW = S5_CHUNK * S5_GROUP
S5_GB = 4

MOE_TILE = 256
MOE_SLOTS = T_ALL * 2
MOE_NT = MOE_SLOTS // MOE_TILE + MOE_EXPERTS

VMEM_LIMIT = 56 * 1024 * 1024


def _cp(*sem):
    return pltpu.CompilerParams(dimension_semantics=sem, vmem_limit_bytes=VMEM_LIMIT)


def _mod_row(i, tm):
    return jnp.minimum((i * tm) // SEQ, BATCH)


def _ada_kernel(c_ref, w_ref, b_ref, o_ref):
    s = jax.nn.silu(c_ref[...])
    o_ref[...] = jnp.dot(s, w_ref[...], precision=lax.Precision.HIGHEST,
                         preferred_element_type=F32) + b_ref[...]


def ada_mod(c8, ada_w, ada_b):
    tn = 1024
    n = ada_w.shape[-1]
    return pl.pallas_call(
        _ada_kernel,
        grid=(DEPTH, n // tn),
        in_specs=[pl.BlockSpec((MOD_ROWS, D_MODEL), lambda l, j: (0, 0)),
                  pl.BlockSpec((None, D_MODEL, tn), lambda l, j: (l, 0, j)),
                  pl.BlockSpec((None, 1, tn), lambda l, j: (l, 0, j))],
        out_specs=pl.BlockSpec((None, MOD_ROWS, tn), lambda l, j: (l, 0, j)),
        out_shape=jax.ShapeDtypeStruct((DEPTH, MOD_ROWS, n), F32),
        compiler_params=_cp("arbitrary", "arbitrary"),
        name="ada_mod",
    )(c8, ada_w, ada_b.reshape(DEPTH, 1, n))


def _norm_mod(x, g, sh, sc):
    y = x * lax.rsqrt(jnp.mean(x * x, axis=-1, keepdims=True) + RMS_EPS) * g
    return y * (1.0 + sc) + sh


def _mod_spec(layer, col, tm):
    return pl.BlockSpec((None, 1, D_MODEL),
                        lambda i, j: (layer * MOD_ROWS + _mod_row(i, tm), 0, col))


def _gate_spec(layer, col, tm, tn):
    return pl.BlockSpec((None, 1, tn),
                        lambda i, j: (layer * MOD_ROWS + _mod_row(i, tm), 0, col * (D_MODEL // tn) + j))


def _nm_mm_kernel(x_ref, g_ref, sh_ref, sc_ref, w_ref, o_ref, h_ref):
    @pl.when(pl.program_id(1) == 0)
    def _():
        h_ref[...] = _norm_mod(x_ref[...], g_ref[...], sh_ref[...], sc_ref[...]).astype(BF16)

    o_ref[...] = jnp.dot(h_ref[...], w_ref[...],
                         preferred_element_type=F32).astype(o_ref.dtype)


def norm_mod_matmul(x, g_all, mod, w_all, layer, wi, sh_col, sc_col, tm=1024, tn=512):
    t = x.shape[0]
    n = w_all.shape[-1]
    return pl.pallas_call(
        _nm_mm_kernel,
        grid=(t // tm, n // tn),
        in_specs=[pl.BlockSpec((tm, D_MODEL), lambda i, j: (i, 0)),
                  pl.BlockSpec((None, 1, D_MODEL), lambda i, j: (layer, 0, 0)),
                  _mod_spec(layer, sh_col, tm),
                  _mod_spec(layer, sc_col, tm),
                  pl.BlockSpec((None, D_MODEL, tn), lambda i, j: (wi, 0, j))],
        out_specs=pl.BlockSpec((tm, tn), lambda i, j: (i, j)),
        out_shape=jax.ShapeDtypeStruct((t, n), F32),
        scratch_shapes=[pltpu.VMEM((tm, D_MODEL), BF16)],
        compiler_params=_cp("arbitrary", "arbitrary"),
        name="norm_mod_matmul",
    )(x, g_all.reshape(DEPTH, 1, D_MODEL), mod, mod, w_all)


def _proj_res_kernel(a_ref, b_ref, wa_ref, wb_ref, x_ref, gate_ref, o_ref):
    y = jnp.dot(a_ref[...], wa_ref[...], preferred_element_type=F32)
    y = y + jnp.dot(b_ref[...], wb_ref[...], preferred_element_type=F32)
    o_ref[...] = x_ref[...] + gate_ref[...] * y


def proj_residual(oa, ob, w_all, wi, x, mod, layer, gate_col, tm=1024, tn=512):
    t, ka = oa.shape
    kb = ob.shape[1]
    assert ka == kb
    return pl.pallas_call(
        _proj_res_kernel,
        grid=(t // tm, D_MODEL // tn),
        in_specs=[pl.BlockSpec((tm, ka), lambda i, j: (i, 0)),
                  pl.BlockSpec((tm, kb), lambda i, j: (i, 0)),
                  pl.BlockSpec((None, ka, tn), lambda i, j: (wi, 0, j)),
                  pl.BlockSpec((None, kb, tn), lambda i, j: (wi, 1, j)),
                  pl.BlockSpec((tm, tn), lambda i, j: (i, j)),
                  _gate_spec(layer, gate_col, tm, tn)],
        out_specs=pl.BlockSpec((tm, tn), lambda i, j: (i, j)),
        out_shape=jax.ShapeDtypeStruct((t, D_MODEL), F32),
        compiler_params=_cp("arbitrary", "arbitrary"),
        name="proj_residual",
    )(oa, ob, w_all, w_all, x, mod)


def _gqa_prep_kernel(p_ref, gain_ref, cos_ref, sin_ref, o_ref):
    nqk = GQA_Q_HEADS + GQA_KV_HEADS
    nh = nqk + GQA_KV_HEADS
    cos = cos_ref[...]
    sin = sin_ref[...]
    lane = lax.broadcasted_iota(jnp.int32, cos.shape, 1)
    half = ROPE_AXIS_DIM // 2
    first_half = (lane % ROPE_AXIS_DIM) < half
    for h in range(nh):
        cols = slice(h * HEAD_DIM, (h + 1) * HEAD_DIM)
        x = p_ref[:, cols]
        if h < nqk:
            is_q = h < GQA_Q_HEADS
            gain = gain_ref[0 if is_q else 1] * (ATT_SCALE * LOG2E if is_q else 1.0)
            y = x * lax.rsqrt(jnp.mean(x * x, axis=-1, keepdims=True) + RMS_EPS) * gain
            partner = jnp.where(first_half, pltpu.roll(y, HEAD_DIM - half, 1), pltpu.roll(y, half, 1))
            x = y * cos + partner * sin
        o_ref[:, cols] = x.astype(BF16)


def gqa_prep(p, gains, cos_t, sin_t, tm=512):
    t = p.shape[0]
    nh = GQA_Q_HEADS + 2 * GQA_KV_HEADS
    w = nh * HEAD_DIM
    assert (3 * NA_HEADS * HEAD_DIM) % w == 0
    col_blk = 3 * NA_HEADS * HEAD_DIM // w
    lat_tiles = T_LAT // tm
    per_seq = SEQ // tm

    def tab_idx(i):
        return (jnp.where(i < lat_tiles, i % per_seq, per_seq), 0)

    return pl.pallas_call(
        _gqa_prep_kernel,
        grid=(t // tm,),
        in_specs=[pl.BlockSpec((tm, w), lambda i: (i, col_blk)),
                  pl.BlockSpec((2, 1, HEAD_DIM), lambda i: (0, 0, 0)),
                  pl.BlockSpec((tm, HEAD_DIM), tab_idx),
                  pl.BlockSpec((tm, HEAD_DIM), tab_idx)],
        out_specs=pl.BlockSpec((tm, w), lambda i: (i, 0)),
        out_shape=jax.ShapeDtypeStruct((t, w), BF16),
        compiler_params=_cp("arbitrary"),
        name="gqa_prep",
    )(p, gains, cos_t, sin_t)


def rope_tables(tm=512):
    t = jnp.arange(SEQ)
    row = (t // GRID_W).astype(F32)
    col = (t % GRID_W).astype(F32)
    inv = ROPE_THETA ** (-jnp.arange(0, ROPE_AXIS_DIM, 2, dtype=F32) / ROPE_AXIS_DIM)
    ar = row[:, None] * inv[None]
    ac = col[:, None] * inv[None]
    cos_t = jnp.concatenate([jnp.cos(ar), jnp.cos(ar), jnp.cos(ac), jnp.cos(ac)], axis=-1)
    sin_t = jnp.concatenate([-jnp.sin(ar), jnp.sin(ar), -jnp.sin(ac), jnp.sin(ac)], axis=-1)
    cos_t = jnp.concatenate([cos_t, jnp.ones((tm, HEAD_DIM), F32)], axis=0)
    sin_t = jnp.concatenate([sin_t, jnp.zeros((tm, HEAD_DIM), F32)], axis=0)
    return cos_t, sin_t


LOG2E = 1.4426950408889634
ATT_KCHUNK = 512


def _softmax_chunk(q, k, v, bias, m, l, acc):
    s = lax.dot_general(q, k.astype(BF16), (((1,), (1,)), ((), ())), preferred_element_type=F32)
    if bias is not None:
        s = s + bias
    m_new = jnp.maximum(m, jnp.max(s, axis=-1, keepdims=True))
    alpha = jnp.exp2(m - m_new)
    p = jnp.exp2(s - m_new)
    l = alpha * l + jnp.sum(p, axis=-1, keepdims=True)
    acc = alpha * acc + jnp.dot(p.astype(BF16), v.astype(BF16), preferred_element_type=F32)
    return m_new, l, acc


def _attn_kernel(*refs, nseg, group, scale):
    q_ref = refs[0]
    k_refs = refs[1:1 + nseg]
    v_refs = refs[1 + nseg:1 + 2 * nseg]
    o_ref = refs[1 + 2 * nseg]
    tq = q_ref.shape[0]
    q = jnp.concatenate([q_ref[:, g * HEAD_DIM:(g + 1) * HEAD_DIM] for g in range(group)], axis=0)
    if scale != 1.0:
        q = q.astype(F32) * scale
    q = q.astype(BF16)
    m = jnp.full((q.shape[0], 1), -jnp.inf, F32)
    l = jnp.zeros((q.shape[0], 1), F32)
    acc = jnp.zeros((q.shape[0], HEAD_DIM), F32)
    for k_ref, v_ref in zip(k_refs, v_refs):
        nk = k_ref.shape[0]
        for c0 in range(0, nk, ATT_KCHUNK):
            cs = min(ATT_KCHUNK, nk - c0)
            m, l, acc = _softmax_chunk(q, k_ref[c0:c0 + cs, :], v_ref[c0:c0 + cs, :], None, m, l, acc)
    o = acc / l
    for g in range(group):
        o_ref[:, g * HEAD_DIM:(g + 1) * HEAD_DIM] = o[g * tq:(g + 1) * tq].astype(o_ref.dtype)


def attention(q_arr, q_idx, kv_arrs, k_idx, v_idx, kv_rows, grid, tq, group, scale, out_rows, out_cols, o_idx, name):
    nseg = len(kv_rows)
    in_specs = [pl.BlockSpec((tq, group * HEAD_DIM), q_idx)]
    in_specs += [pl.BlockSpec((kv_rows[i], HEAD_DIM), k_idx[i]) for i in range(nseg)]
    in_specs += [pl.BlockSpec((kv_rows[i], HEAD_DIM), v_idx[i]) for i in range(nseg)]
    return pl.pallas_call(
        functools.partial(_attn_kernel, nseg=nseg, group=group, scale=scale),
        grid=grid,
        in_specs=in_specs,
        out_specs=pl.BlockSpec((tq, group * HEAD_DIM), o_idx),
        out_shape=jax.ShapeDtypeStruct((out_rows, out_cols), BF16),
        compiler_params=_cp(*(["arbitrary"] * len(grid))),
        name=name,
    )(q_arr, *kv_arrs, *kv_arrs)


NA_QROWS = 8
NA_KROWS = 16
NA_TQ = NA_QROWS * GRID_W
NA_TK = NA_KROWS * GRID_W
GRID_H = SEQ // GRID_W


def _na_kernel(q_ref, k_ref, v_ref, kc_ref, vc_ref, bias_ref, o_ref):
    rb = pl.program_id(2)
    w0 = jnp.clip(NA_QROWS * rb - (NA_KROWS - NA_QROWS) // 2, 0, GRID_H - NA_KROWS) * GRID_W
    w0 = pl.multiple_of(w0, 4 * GRID_W)
    q = (q_ref[...] * (ATT_SCALE * LOG2E)).astype(BF16)
    m = jnp.full((NA_TQ, 1), -jnp.inf, F32)
    l = jnp.zeros((NA_TQ, 1), F32)
    acc = jnp.zeros((NA_TQ, HEAD_DIM), F32)
    m, l, acc = _softmax_chunk(q, kc_ref[...], vc_ref[...], None, m, l, acc)
    for c0 in range(0, NA_TK, ATT_KCHUNK):
        kw = k_ref[pl.ds(w0 + c0, ATT_KCHUNK), :]
        vw = v_ref[pl.ds(w0 + c0, ATT_KCHUNK), :]
        m, l, acc = _softmax_chunk(q, kw, vw, bias_ref[:, c0:c0 + ATT_KCHUNK], m, l, acc)
    o_ref[...] = (acc / l).astype(o_ref.dtype)


def na_bias_table(rpb):
    aw = jnp.arange(GRID_W)
    col_start = jnp.clip(aw - NA_COLS // 2, 0, GRID_W - NA_COLS)
    col_ok = (aw[None, :] >= col_start[:, None]) & (aw[None, :] < col_start[:, None] + NA_COLS)
    off_c = jnp.clip(aw[None, :] - aw[:, None] + (NA_COLS - 1), 0, 2 * NA_COLS - 2)
    nh, nr, nc = rpb.shape
    pick = (off_c.reshape(1, -1) == jnp.arange(nc)[:, None]).astype(F32)
    tiles = jnp.dot(rpb.astype(F32).reshape(nh * nr, nc), pick, precision=lax.Precision.HIGHEST)
    tiles = jnp.where(col_ok.reshape(1, -1), tiles * LOG2E, NEG_INF).reshape(nh, nr, GRID_W, GRID_W)
    blocked = jnp.full((nh, 1, GRID_W, GRID_W), NEG_INF, F32)
    tiles = jnp.concatenate([tiles, blocked], axis=1)
    nblk = GRID_H // NA_QROWS
    out = []
    for rb in (0, nblk // 2, nblk - 1):
        w0 = min(max(NA_QROWS * rb - (NA_KROWS - NA_QROWS) // 2, 0), GRID_H - NA_KROWS)
        picks = []
        for qr in range(NA_QROWS * rb, NA_QROWS * (rb + 1)):
            rs = min(max(qr - NA_ROWS // 2, 0), GRID_H - NA_ROWS)
            for kr in range(w0, w0 + NA_KROWS):
                picks.append(kr - qr + NA_ROWS - 1 if rs <= kr < rs + NA_ROWS else nr)
        b = jnp.stack([tiles[:, k] for k in picks], axis=1)
        b = b.reshape(nh, NA_QROWS, NA_KROWS, GRID_W, GRID_W).transpose(0, 1, 3, 2, 4)
        out.append(b.reshape(nh, NA_TQ, NA_TK))
    return jnp.stack(out, axis=1)


def na_attention(p, bias):
    nblk = GRID_H // NA_QROWS
    ctx_blk0 = T_LAT // CTX_LEN

    def pat(rb):
        return jnp.where(rb == 0, 0, jnp.where(rb == nblk - 1, 2, 1))

    return pl.pallas_call(
        _na_kernel,
        grid=(NA_HEADS, BATCH, nblk),
        in_specs=[pl.BlockSpec((NA_TQ, HEAD_DIM), lambda h, b, r: (b * nblk + r, h)),
                  pl.BlockSpec((SEQ, HEAD_DIM), lambda h, b, r: (b, NA_HEADS + h)),
                  pl.BlockSpec((SEQ, HEAD_DIM), lambda h, b, r: (b, 2 * NA_HEADS + h)),
                  pl.BlockSpec((CTX_LEN, HEAD_DIM), lambda h, b, r: (ctx_blk0 + b, NA_HEADS + h)),
                  pl.BlockSpec((CTX_LEN, HEAD_DIM), lambda h, b, r: (ctx_blk0 + b, 2 * NA_HEADS + h)),
                  pl.BlockSpec((None, None, NA_TQ, NA_TK), lambda h, b, r: (h, pat(r), 0, 0))],
        out_specs=pl.BlockSpec((NA_TQ, HEAD_DIM), lambda h, b, r: (b * nblk + r, h)),
        out_shape=jax.ShapeDtypeStruct((T_LAT, NA_HEADS * HEAD_DIM), BF16),
        compiler_params=_cp("arbitrary", "arbitrary", "arbitrary"),
        name="na_attention",
    )(p, p, p, p, p, bias)


def s5_weights(lam_re, lam_im, log_dt, b_re, b_im, c_re, c_im):
    hi = lax.Precision.HIGHEST
    L = S5_CHUNK
    lr = jnp.minimum(lam_re.astype(F32), -1e-4)
    li = lam_im.astype(F32)
    dt = jnp.exp(log_dt.astype(F32))[..., None]
    mag = jnp.exp(lr * dt)
    ar = mag * jnp.cos(li * dt)
    ai = mag * jnp.sin(li * dt)
    den = lr * lr + li * li
    xr = ar - 1.0
    fr = (xr * lr + ai * li) / den
    fi = (ai * lr - xr * li) / den
    br = b_re.astype(F32)
    bi = b_im.astype(F32)
    bbr = fr[..., None] * br - fi[..., None] * bi
    bbi = fr[..., None] * bi + fi[..., None] * br
    k = jnp.arange(L + 1, dtype=F32)[:, None, None, None]
    pm = jnp.exp(lr * dt * k)
    pr = pm * jnp.cos(li * dt * k)
    pi = pm * jnp.sin(li * dt * k)
    e_r = pr[..., None] * bbr - pi[..., None] * bbi
    e_i = pr[..., None] * bbi + pi[..., None] * bbr
    cr = c_re.astype(F32)
    ci = c_im.astype(F32)
    kern = (jnp.einsum('dgop,kdgpi->kdgoi', cr, e_r[:L], precision=hi)
            - jnp.einsum('dgop,kdgpi->kdgoi', ci, e_i[:L], precision=hi))
    s_i = jnp.arange(L)
    ng = lr.shape[1]
    lag = s_i[None, :, None] - s_i[:, None, None]
    place_f = (lag == s_i[None, None, :]).astype(F32)
    place = jnp.stack([place_f, place_f[::-1, ::-1]], axis=-1)
    toep = jnp.einsum('stkd,kdgoi->gsito', place, kern, precision=hi)
    w_toep_sum = toep.reshape(ng, S5_CW, S5_CW)
    st_r = e_r[L - 1 - s_i].transpose(1, 2, 0, 4, 3).reshape(2, ng, S5_CW, S5_STATE)
    st_i = e_i[L - 1 - s_i].transpose(1, 2, 0, 4, 3).reshape(2, ng, S5_CW, S5_STATE)
    w_state = jnp.concatenate([st_r, st_i], axis=-1)
    w_state_sw = jnp.concatenate([st_i, st_r], axis=-1)
    qr = pr[1:, :, :, None, :]
    qi = pi[1:, :, :, None, :]
    d_r = cr[None] * qr - ci[None] * qi
    d_i = cr[None] * qi + ci[None] * qr
    wo_r = d_r.transpose(1, 2, 4, 0, 3).reshape(2, ng, S5_STATE, S5_CW)
    wo_i = (-d_i).transpose(1, 2, 4, 0, 3).reshape(2, ng, S5_STATE, S5_CW)
    w_out = jnp.concatenate([wo_r, wo_i], axis=2)
    flip_rows = lambda w: w.reshape(ng, L, S5_GROUP, w.shape[-1])[:, ::-1].reshape(w.shape)
    flip_cols = lambda w: w.reshape(ng, w.shape[1], L, S5_GROUP)[:, :, ::-1].reshape(w.shape)
    wst = jnp.concatenate([w_state[0], flip_rows(w_state[1]),
                           w_state_sw[0], flip_rows(w_state_sw[1])], axis=-1).astype(BF16)
    wy = jnp.concatenate([w_toep_sum, w_out[0], flip_cols(w_out[1])], axis=1).astype(BF16)
    a1 = jnp.concatenate([pr[L], pr[L]], axis=-1)
    a2 = jnp.concatenate([-pi[L], pi[L]], axis=-1)
    per_row = lambda a: jnp.repeat(a.transpose(1, 0, 2), BATCH, axis=1)
    return wst, wy, per_row(a1), per_row(a2)


S5_CTX_CHUNKS = CTX_LEN // S5_CHUNK
S5_PACK = HEAD_DIM // S5_GROUP
S5_PACKS = S5_GROUPS // S5_PACK
S5_BROWS = S5_NCHUNK


def s5_lane_permutation():
    tl, g, hh = jnp.meshgrid(jnp.arange(S5_CHUNK), jnp.arange(S5_PACK), jnp.arange(S5_GROUP), indexing="ij")
    dst = (g * S5_CW + tl * S5_GROUP + hh).reshape(-1)
    return (dst[:, None] == jnp.arange(S5_PACK * S5_CW)[None, :]).astype(BF16)


S5_SB = 2


def _s5_pack_kernel(lat_ref, ctx_ref, p_ref, o_ref):
    cols = []
    for tl in range(S5_CHUNK):
        rows = []
        for b in range(S5_SB):
            rows.append(ctx_ref[pl.ds(b * CTX_LEN + tl, S5_CTX_CHUNKS, stride=S5_CHUNK), :])
            rows.append(lat_ref[pl.ds(b * SEQ + tl, SEQ // S5_CHUNK, stride=S5_CHUNK), :])
        cols.append(jnp.concatenate(rows, axis=0))
    xcat = jnp.concatenate(cols, axis=1).astype(BF16)
    o_ref[...] = jnp.dot(xcat, p_ref[...], preferred_element_type=F32).astype(BF16)


def s5_pack(u, perm):
    ctx0 = T_LAT // (S5_SB * CTX_LEN)
    w = S5_PACK * S5_CW
    return pl.pallas_call(
        _s5_pack_kernel,
        grid=(S5_PACKS, BATCH // S5_SB),
        in_specs=[pl.BlockSpec((S5_SB * SEQ, HEAD_DIM), lambda k, b: (b, k)),
                  pl.BlockSpec((S5_SB * CTX_LEN, HEAD_DIM), lambda k, b: (ctx0 + b, k)),
                  pl.BlockSpec((w, w), lambda k, b: (0, 0))],
        out_specs=pl.BlockSpec((None, S5_SB * S5_BROWS, w), lambda k, b: (k, b, 0)),
        out_shape=jax.ShapeDtypeStruct((S5_PACKS, BATCH * S5_BROWS, w), BF16),
        compiler_params=_cp("arbitrary", "arbitrary"),
        name="s5_pack",
    )(u, u, perm)


def _s5_kernel(u_ref, wst_ref, wy_ref, a1_ref, a2_ref, y_ref, s_ref, ssw_ref, hf_ref, hr_ref):
    gb = wst_ref.shape[0]
    nc = S5_NCHUNK
    sw = 2 * S5_STATE
    for j in range(gb):
        ug = u_ref[:, j * S5_CW:(j + 1) * S5_CW]
        s4 = jnp.dot(ug, wst_ref[j], preferred_element_type=F32)
        for b in range(BATCH):
            rows = s4[b * nc:(b + 1) * nc]
            s_ref[:, j, b, :] = rows[:, 0:sw]
            s_ref[:, j, BATCH + b, :] = rows[:, sw:2 * sw]
            ssw_ref[:, j, b, :] = rows[:, 2 * sw:3 * sw]
            ssw_ref[:, j, BATCH + b, :] = rows[:, 3 * sw:4 * sw]
    a1 = a1_ref[...]
    a2 = a2_ref[...]
    fwd = lax.broadcasted_iota(jnp.int32, (gb, 2 * BATCH, sw), 1) < BATCH

    def step(i, carry):
        h, hs = carry
        ri = jnp.where(i < S5_CTX_CHUNKS, S5_CTX_CHUNKS - 1 - i, nc - 1 + S5_CTX_CHUNKS - i)
        hf_ref[i] = h
        hr_ref[ri] = h
        s = jnp.where(fwd, s_ref[i], s_ref[ri])
        ssw = jnp.where(fwd, ssw_ref[i], ssw_ref[ri])
        return a1 * h + a2 * hs + s, a1 * hs - a2 * h + ssw

    z = jnp.zeros((gb, 2 * BATCH, sw), F32)
    lax.fori_loop(0, nc, step, (z, z), unroll=4)
    for j in range(gb):
        ug = u_ref[:, j * S5_CW:(j + 1) * S5_CW]
        hf = jnp.concatenate([hf_ref[:, j, b, :] for b in range(BATCH)], axis=0)
        hr = jnp.concatenate([hr_ref[:, j, BATCH + b, :] for b in range(BATCH)], axis=0)
        lhs = jnp.concatenate([ug, hf.astype(BF16), hr.astype(BF16)], axis=1)
        y_ref[:, j * S5_CW:(j + 1) * S5_CW] = jnp.dot(lhs, wy_ref[j], preferred_element_type=F32)


def s5_core(u_packed, wst, wy, a1, a2):
    gb = S5_GB
    per_pack = S5_PACK // gb
    rows = u_packed.shape[1]
    sw = 2 * S5_STATE
    blk = lambda *shape: pl.BlockSpec((gb,) + shape, lambda i: (i,) + (0,) * len(shape))
    io = pl.BlockSpec((None, rows, gb * S5_CW), lambda i: (i // per_pack, 0, i % per_pack))
    state = pltpu.VMEM((S5_NCHUNK, gb, 2 * BATCH, sw), F32)
    return pl.pallas_call(
        _s5_kernel,
        grid=(S5_GROUPS // gb,),
        in_specs=[io, blk(S5_CW, 4 * sw), blk(S5_CW + 2 * sw, S5_CW), blk(2 * BATCH, sw), blk(2 * BATCH, sw)],
        out_specs=io,
        out_shape=jax.ShapeDtypeStruct(u_packed.shape, F32),
        scratch_shapes=[state, state, state, state],
        compiler_params=_cp("arbitrary"),
        name="s5_core",
    )(u_packed, wst, wy, a1, a2)


def _s5_unpack_kernel(y_ref, pt_ref, lat_ref, ctx_ref):
    y = y_ref[...]
    hi = y.astype(BF16)
    lo = (y - hi.astype(F32)).astype(BF16)
    yn = (jnp.dot(hi, pt_ref[...], preferred_element_type=F32)
          + jnp.dot(lo, pt_ref[...], preferred_element_type=F32))
    for tl in range(S5_CHUNK):
        piece = yn[:, tl * HEAD_DIM:(tl + 1) * HEAD_DIM]
        for b in range(S5_SB):
            r0 = b * S5_BROWS
            ctx_ref[pl.ds(b * CTX_LEN + tl, S5_CTX_CHUNKS, stride=S5_CHUNK), :] = piece[r0:r0 + S5_CTX_CHUNKS]
            lat_ref[pl.ds(b * SEQ + tl, SEQ // S5_CHUNK, stride=S5_CHUNK), :] = (
                piece[r0 + S5_CTX_CHUNKS:r0 + S5_BROWS])


def s5_unpack(y_packed, perm_t):
    w = S5_PACK * S5_CW
    return pl.pallas_call(
        _s5_unpack_kernel,
        grid=(S5_PACKS, BATCH // S5_SB),
        in_specs=[pl.BlockSpec((None, S5_SB * S5_BROWS, w), lambda k, b: (k, b, 0)),
                  pl.BlockSpec((w, w), lambda k, b: (0, 0))],
        out_specs=[pl.BlockSpec((S5_SB * SEQ, HEAD_DIM), lambda k, b: (b, k)),
                   pl.BlockSpec((S5_SB * CTX_LEN, HEAD_DIM), lambda k, b: (b, k))],
        out_shape=[jax.ShapeDtypeStruct((T_LAT, D_MODEL), F32), jax.ShapeDtypeStruct((T_CTX, D_MODEL), F32)],
        compiler_params=_cp("arbitrary", "arbitrary"),
        name="s5_unpack",
    )(y_packed, perm_t)


def _glu_kernel(u_ref, y_ref, d_ref, wa_ref, wg_ref, x_ref, gate_ref, o_ref, h_ref):
    @pl.when(pl.program_id(1) == 0)
    def _():
        y = u_ref[...] * d_ref[...] + y_ref[...]
        h_ref[...] = jax.nn.gelu(y).astype(BF16)

    h = h_ref[...]
    za = jnp.dot(h, wa_ref[...], preferred_element_type=F32)
    zg = jnp.dot(h, wg_ref[...], preferred_element_type=F32)
    o_ref[...] = x_ref[...] + gate_ref[...] * (za * jax.nn.sigmoid(zg))


def glu_residual(u, y, d_all, w_all, wi, x, mod, layer, gate_col, tm=1024, tn=256):
    t = u.shape[0]
    nj = D_MODEL // tn
    row = pl.BlockSpec((tm, D_MODEL), lambda i, j: (i, 0))
    return pl.pallas_call(
        _glu_kernel,
        grid=(t // tm, nj),
        in_specs=[row, row,
                  pl.BlockSpec((None, 1, D_MODEL), lambda i, j: (wi, 0, 0)),
                  pl.BlockSpec((None, D_MODEL, tn), lambda i, j: (wi, 0, j)),
                  pl.BlockSpec((None, D_MODEL, tn), lambda i, j: (wi, 0, nj + j)),
                  pl.BlockSpec((tm, tn), lambda i, j: (i, j)),
                  _gate_spec(layer, gate_col, tm, tn)],
        out_specs=pl.BlockSpec((tm, tn), lambda i, j: (i, j)),
        out_shape=jax.ShapeDtypeStruct((t, D_MODEL), F32),
        scratch_shapes=[pltpu.VMEM((tm, D_MODEL), BF16)],
        compiler_params=_cp("arbitrary", "arbitrary"),
        name="glu_residual",
    )(u, y, d_all.reshape(-1, 1, D_MODEL), w_all, w_all, x, mod)


ROUTE_LANES = 128
MOE_ROW_TILES = D_MODEL // 2 // HEAD_DIM


def _route_kernel(x_ref, g_ref, sh_ref, sc_ref, wr_ref, br_ref, h_ref, r_ref, cnt_ref, carry_ref):
    @pl.when(pl.program_id(0) == 0)
    def _():
        carry_ref[...] = jnp.zeros_like(carry_ref)

    h = _norm_mod(x_ref[...], g_ref[...], sh_ref[...], sc_ref[...])
    bits = lax.bitcast_convert_type(h.astype(BF16).astype(F32), jnp.uint32)
    half = h.shape[1] // 2
    packed = (bits[:, :half] >> 16) | (bits[:, half:] & jnp.uint32(0xFFFF0000))
    for j in range(half // HEAD_DIM):
        h_ref[:, j, :] = packed[:, j * HEAD_DIM:(j + 1) * HEAD_DIM]
    lg = jnp.dot(h, wr_ref[...], precision=lax.Precision.HIGHEST, preferred_element_type=F32) + br_ref[...]
    lane = lax.broadcasted_iota(jnp.int32, lg.shape, 1)
    ninf = -jnp.inf
    coarse = lane < MOE_GROUPS
    lc = jnp.where(coarse, lg, ninf)
    mc = jnp.max(lc, axis=-1, keepdims=True)
    g_sel = jnp.min(jnp.where(lc == mc, lane, ROUTE_LANES), axis=-1, keepdims=True)
    p_sel = 1.0 / jnp.sum(jnp.where(coarse, jnp.exp(lc - mc), 0.0), axis=-1, keepdims=True)
    lo = MOE_GROUPS + MOE_EXPERTS_PER_GROUP * g_sel
    lf = jnp.where((lane >= lo) & (lane < lo + MOE_EXPERTS_PER_GROUP), lg, ninf)
    v0 = jnp.max(lf, axis=-1, keepdims=True)
    i0 = jnp.min(jnp.where(lf == v0, lane, ROUTE_LANES), axis=-1, keepdims=True)
    lf2 = jnp.where(lane == i0, ninf, lf)
    v1 = jnp.max(lf2, axis=-1, keepdims=True)
    i1 = jnp.min(jnp.where(lf2 == v1, lane, ROUTE_LANES), axis=-1, keepdims=True)
    e1 = jnp.exp(v1 - v0)
    w0 = p_sel / (1.0 + e1)
    w1 = w0 * e1
    tm = lg.shape[0]
    lower = (lax.broadcasted_iota(jnp.int32, (tm, tm), 1)
             < lax.broadcasted_iota(jnp.int32, (tm, tm), 0)).astype(BF16)
    carry = carry_ref[...]
    hot0 = (lane == i0).astype(F32)
    before0 = carry + jnp.dot(lower, hot0.astype(BF16), preferred_element_type=F32)
    rank0 = jnp.sum(hot0 * before0, axis=-1, keepdims=True)
    carry = carry + jnp.sum(hot0, axis=0, keepdims=True)
    hot1 = (lane == i1).astype(F32)
    before1 = carry + jnp.dot(lower, hot1.astype(BF16), preferred_element_type=F32)
    rank1 = jnp.sum(hot1 * before1, axis=-1, keepdims=True)
    carry = carry + jnp.sum(hot1, axis=0, keepdims=True)
    carry_ref[...] = carry
    cnt_ref[...] = carry
    cols = [(i0 - MOE_GROUPS).astype(F32), (i1 - MOE_GROUPS).astype(F32), w0, w1, rank0, rank1]
    r = jnp.zeros_like(lg)
    for c, val in enumerate(cols):
        r = jnp.where(lane == c, val, r)
    r_ref[...] = r


ROUTE_E0, ROUTE_E1, ROUTE_W0, ROUTE_W1, ROUTE_R0, ROUTE_R1 = range(6)


def moe_route(x, g_all, mod, layer, sh_col, sc_col, w_route, b_route, tm=512):
    t = x.shape[0]
    mspec = lambda col: pl.BlockSpec((None, 1, D_MODEL),
                                     lambda i: (layer * MOD_ROWS + _mod_row(i, tm), 0, col))
    return pl.pallas_call(
        _route_kernel,
        grid=(t // tm,),
        in_specs=[pl.BlockSpec((tm, D_MODEL), lambda i: (i, 0)),
                  pl.BlockSpec((None, 1, D_MODEL), lambda i: (layer, 0, 0)),
                  mspec(sh_col), mspec(sc_col),
                  pl.BlockSpec((D_MODEL, ROUTE_LANES), lambda i: (0, 0)),
                  pl.BlockSpec((1, ROUTE_LANES), lambda i: (0, 0))],
        out_specs=[pl.BlockSpec((tm, MOE_ROW_TILES, HEAD_DIM), lambda i: (i, 0, 0)),
                   pl.BlockSpec((tm, ROUTE_LANES), lambda i: (i, 0)),
                   pl.BlockSpec((1, ROUTE_LANES), lambda i: (0, 0))],
        out_shape=[jax.ShapeDtypeStruct((t, MOE_ROW_TILES, HEAD_DIM), jnp.uint32),
                   jax.ShapeDtypeStruct((t, ROUTE_LANES), F32),
                   jax.ShapeDtypeStruct((1, ROUTE_LANES), F32)],
        scratch_shapes=[pltpu.VMEM((1, ROUTE_LANES), F32)],
        compiler_params=_cp("arbitrary"),
        name="moe_route",
    )(x, g_all.reshape(DEPTH, 1, D_MODEL), mod, mod, w_route, b_route)


def moe_plan(route, counts):
    cnt = counts[0, MOE_GROUPS:MOE_GROUPS + MOE_EXPERTS].astype(jnp.int32)
    padded = ((cnt + MOE_TILE - 1) // MOE_TILE) * MOE_TILE
    pad_end = jnp.cumsum(padded)
    pad_off = pad_end - padded
    e = route[:, ROUTE_E0:ROUTE_E1 + 1].astype(jnp.int32)
    rank = route[:, ROUTE_R0:ROUTE_R1 + 1].astype(jnp.int32)
    hot = e[:, :, None] == jnp.arange(MOE_EXPERTS, dtype=jnp.int32)[None, None, :]
    dest = jnp.sum(jnp.where(hot, pad_off[None, None, :], 0), axis=-1) + rank
    n_used = (pad_end[-1] // MOE_TILE).astype(jnp.int32).reshape(1)
    tile_start = jnp.arange(MOE_NT, dtype=jnp.int32) * MOE_TILE
    tile_expert = jnp.sum((pad_end[None, :] <= tile_start[:, None]).astype(jnp.int32), axis=1)
    tile_expert = jnp.minimum(tile_expert, MOE_EXPERTS - 1).astype(jnp.int32)
    return dest, tile_expert, n_used


def _dispatch_kernel(dest_ref, h_ref, zero_hbm, hs_hbm, sem):
    del zero_hbm
    tm = h_ref.shape[0]

    def row_copy(r, d):
        return pltpu.make_async_copy(h_ref.at[pl.ds(r, 1)], hs_hbm.at[pl.ds(d, 1)], sem)

    def issue(r, c):
        row_copy(r, dest_ref[0, 0, 2 * r]).start(priority=0)
        row_copy(r, dest_ref[0, 0, 2 * r + 1]).start(priority=1)
        return c
    lax.fori_loop(0, tm, issue, 0, unroll=8)

    def drain(r, c):
        row_copy(r, 0).wait()
        row_copy(r, 0).wait()
        return c
    lax.fori_loop(0, tm, drain, 0, unroll=8)


def moe_dispatch(h, dest, tm=512):
    t = h.shape[0]
    zeros = jnp.zeros((MOE_NT * MOE_TILE,) + h.shape[1:], h.dtype)
    return pl.pallas_call(
        _dispatch_kernel,
        grid=(t // tm,),
        in_specs=[pl.BlockSpec((1, 1, 2 * tm), lambda i: (i, 0, 0), memory_space=pltpu.SMEM),
                  pl.BlockSpec((tm,) + h.shape[1:], lambda i: (i, 0, 0)),
                  pl.BlockSpec(memory_space=pl.ANY)],
        out_specs=pl.BlockSpec(memory_space=pl.ANY),
        out_shape=jax.ShapeDtypeStruct(zeros.shape, h.dtype),
        scratch_shapes=[pltpu.SemaphoreType.DMA],
        input_output_aliases={2: 0},
        compiler_params=_cp("arbitrary"),
        name="moe_dispatch",
    )(dest.reshape(t // tm, 1, 2 * tm), h, zeros)


def _expert_kernel(te_ref, nu_ref, hs_ref, wg_ref, wu_ref, wd_ref, o_ref, wgb, wub, wdb):
    i = pl.program_id(0)

    @pl.when(i < nu_ref[0])
    def _():
        @pl.when((i == 0) | (te_ref[i] != te_ref[jnp.maximum(i - 1, 0)]))
        def _():
            wgb[...] = wg_ref[...].astype(BF16)
            wub[...] = wu_ref[...].astype(BF16)
            wdb[...] = wd_ref[...].astype(BF16)

        w = jnp.concatenate([hs_ref[:, j, :] for j in range(MOE_ROW_TILES)], axis=1)
        lo = lax.bitcast_convert_type(w << 16, F32).astype(BF16)
        hi = lax.bitcast_convert_type(w & jnp.uint32(0xFFFF0000), F32).astype(BF16)
        h = jnp.concatenate([lo, hi], axis=1)
        g = jnp.dot(h, wgb[...], preferred_element_type=F32)
        u = jnp.dot(h, wub[...], preferred_element_type=F32)
        hid = (jax.nn.silu(g) * u).astype(BF16)
        o_ref[...] = jnp.dot(hid, wdb[...], preferred_element_type=F32)

    @pl.when(i >= nu_ref[0])
    def _():
        o_ref[...] = jnp.zeros_like(o_ref)


def moe_experts(hs, tile_expert, n_used, w_gate, w_up, w_down, layer):
    def wspec(shape):
        return pl.BlockSpec((None, None, None) + shape,
                            lambda i, te, nu: (layer, te[i] // MOE_EXPERTS_PER_GROUP,
                                               te[i] % MOE_EXPERTS_PER_GROUP, 0, 0))

    grid_spec = pltpu.PrefetchScalarGridSpec(
        num_scalar_prefetch=2,
        grid=(MOE_NT,),
        in_specs=[pl.BlockSpec((MOE_TILE, MOE_ROW_TILES, HEAD_DIM),
                               lambda i, te, nu: (jnp.minimum(i, nu[0] - 1), 0, 0)),
                  wspec((D_MODEL, MOE_FFN)), wspec((D_MODEL, MOE_FFN)), wspec((MOE_FFN, D_MODEL))],
        out_specs=pl.BlockSpec((MOE_TILE, D_MODEL), lambda i, te, nu: (i, 0)),
        scratch_shapes=[pltpu.VMEM((D_MODEL, MOE_FFN), BF16),
                        pltpu.VMEM((D_MODEL, MOE_FFN), BF16),
                        pltpu.VMEM((MOE_FFN, D_MODEL), BF16)])
    return pl.pallas_call(
        _expert_kernel,
        grid_spec=grid_spec,
        out_shape=jax.ShapeDtypeStruct((MOE_NT * MOE_TILE, D_MODEL), F32),
        compiler_params=_cp("arbitrary"),
        name="moe_experts",
    )(tile_expert, n_used, hs, w_gate, w_up, w_down)


def _combine_kernel(dest_ref, destn_ref, ys_hbm, r_ref, x_ref, gate_ref, o_ref, buf, sem):
    i = pl.program_id(0)
    n = pl.num_programs(0)
    tm = x_ref.shape[0]

    def row_copy(d, slot, k, r):
        return pltpu.make_async_copy(ys_hbm.at[pl.ds(d, 1), :], buf.at[slot, k, pl.ds(r, 1), :], sem.at[slot])

    def issue(dref, slot):
        def body(r, c):
            row_copy(dref[0, 0, 2 * r], slot, 0, r).start(priority=0)
            row_copy(dref[0, 0, 2 * r + 1], slot, 1, r).start(priority=1)
            return c
        lax.fori_loop(0, tm, body, 0, unroll=8)

    @pl.when(i == 0)
    def _():
        issue(dest_ref, 0)

    @pl.when(i + 1 < n)
    def _():
        issue(destn_ref, (i + 1) % 2)

    slot = i % 2

    def drain(r, c):
        row_copy(0, slot, 0, r).wait()
        row_copy(0, slot, 1, r).wait()
        return c
    lax.fori_loop(0, tm, drain, 0, unroll=8)
    r = r_ref[...]
    y = r[:, ROUTE_W0:ROUTE_W0 + 1] * buf[slot, 0] + r[:, ROUTE_W1:ROUTE_W1 + 1] * buf[slot, 1]
    o_ref[...] = x_ref[...] + gate_ref[...] * y


def moe_combine(ys, dest, route, x, mod, layer, gate_col, tm=256):
    t = x.shape[0]
    row = pl.BlockSpec((tm, D_MODEL), lambda i: (i, 0))
    nblk = t // tm
    smem = lambda off: pl.BlockSpec((1, 1, 2 * tm), lambda i: (jnp.minimum(i + off, nblk - 1), 0, 0),
                                    memory_space=pltpu.SMEM)
    dest3 = dest.reshape(nblk, 1, 2 * tm)
    return pl.pallas_call(
        _combine_kernel,
        grid=(nblk,),
        in_specs=[smem(0), smem(1),
                  pl.BlockSpec(memory_space=pl.ANY),
                  pl.BlockSpec((tm, ROUTE_LANES), lambda i: (i, 0)),
                  row,
                  pl.BlockSpec((None, 1, D_MODEL),
                               lambda i: (layer * MOD_ROWS + _mod_row(i, tm), 0, gate_col))],
        out_specs=row,
        out_shape=jax.ShapeDtypeStruct((t, D_MODEL), F32),
        scratch_shapes=[pltpu.VMEM((2, 2, tm, D_MODEL), F32), pltpu.SemaphoreType.DMA((2,))],
        compiler_params=_cp("arbitrary"),
        name="moe_combine",
    )(dest3, dest3, ys, route, x, mod)


def _final_norm_kernel(x_ref, g_ref, o_ref):
    x = x_ref[...]
    o_ref[...] = x * lax.rsqrt(jnp.mean(x * x, axis=-1, keepdims=True) + RMS_EPS) * g_ref[...]


def final_norm(x, g, rows, tm=512):
    row = pl.BlockSpec((tm, D_MODEL), lambda i: (i, 0))
    return pl.pallas_call(
        _final_norm_kernel,
        grid=(rows // tm,),
        in_specs=[row, pl.BlockSpec((1, D_MODEL), lambda i: (0, 0))],
        out_specs=row,
        out_shape=jax.ShapeDtypeStruct((rows, D_MODEL), F32),
        compiler_params=_cp("arbitrary"),
        name="final_norm",
    )(x, g.reshape(1, D_MODEL))


def attention_layer(xs, mod, layer, norm1_g, w_in, w_out, na_bias, q_gain, k_gain, cos_t, sin_t):
    li = layer // 2
    p = norm_mod_matmul(xs, norm1_g, mod, w_in, layer, li, 0, 1)
    gains = jnp.stack([q_gain[li], k_gain[li]]).reshape(2, 1, HEAD_DIM)
    qkv = gqa_prep(p, gains, cos_t, sin_t)
    kcol, vcol = GQA_Q_HEADS, GQA_Q_HEADS + GQA_KV_HEADS
    grp = GQA_Q_HEADS // GQA_KV_HEADS
    ctx0 = T_LAT // CTX_LEN

    oa = na_attention(p, na_bias[li])
    tq = 256
    nq = SEQ // tq
    ob = attention(
        qkv, lambda b, k, q: (b * nq + q, k),
        [qkv, qkv],
        [lambda b, k, q: (b, kcol + k), lambda b, k, q: (ctx0 + b, kcol + k)],
        [lambda b, k, q: (b, vcol + k), lambda b, k, q: (ctx0 + b, vcol + k)],
        [SEQ, CTX_LEN], (BATCH, GQA_KV_HEADS, nq), tq, grp, 1.0,
        T_LAT, GQA_Q_HEADS * HEAD_DIM, lambda b, k, q: (b * nq + q, k), "gqa_latent")
    oac = attention(
        p, lambda b, h: (ctx0 + b, h),
        [p],
        [lambda b, h: (ctx0 + b, NA_HEADS + h)],
        [lambda b, h: (ctx0 + b, 2 * NA_HEADS + h)],
        [CTX_LEN], (BATCH, NA_HEADS), CTX_LEN, 1, ATT_SCALE * LOG2E,
        T_CTX, NA_HEADS * HEAD_DIM, lambda b, h: (b, h), "na_context")
    obc = attention(
        qkv, lambda b, k: (ctx0 + b, k),
        [qkv],
        [lambda b, k: (ctx0 + b, kcol + k)],
        [lambda b, k: (ctx0 + b, vcol + k)],
        [CTX_LEN], (BATCH, GQA_KV_HEADS), CTX_LEN, grp, 1.0,
        T_CTX, GQA_Q_HEADS * HEAD_DIM, lambda b, k: (b, k), "gqa_context")
    oa_all = jnp.concatenate([oa, oac], axis=0)
    ob_all = jnp.concatenate([ob, obc], axis=0)
    return proj_residual(oa_all, ob_all, w_out, li, xs, mod, layer, 2)


def s5_layer(xs, mod, layer, norm1_g, w_in, chunk_ops, d_skip, w_glu):
    li = layer // 2
    u = norm_mod_matmul(xs, norm1_g, mod, w_in, layer, li, 0, 1)
    wst, wy, a1, a2 = (w[li] for w in chunk_ops)
    perm = s5_lane_permutation()
    y_lat, y_ctx = s5_unpack(s5_core(s5_pack(u, perm), wst, wy, a1, a2), perm.T)
    y = jnp.concatenate([y_lat, y_ctx], axis=0)
    return glu_residual(u, y, d_skip, w_glu, li, xs, mod, layer, 2)


def moe_layer(xs, mod, layer, norm2_g, w_coarse, b_coarse, w_fine, b_fine, w_gate, w_up, w_down):
    wf = w_fine[layer].transpose(1, 0, 2).reshape(D_MODEL, MOE_EXPERTS)
    w_route = jnp.concatenate([w_coarse[layer], wf], axis=1).astype(F32)
    w_route = jnp.pad(w_route, ((0, 0), (0, ROUTE_LANES - w_route.shape[1])))
    b_route = jnp.concatenate([b_coarse[layer], b_fine[layer].reshape(-1)]).astype(F32)
    b_route = jnp.pad(b_route, (0, ROUTE_LANES - b_route.shape[0])).reshape(1, ROUTE_LANES)
    h, route, counts = moe_route(xs, norm2_g, mod, layer, 3, 4, w_route, b_route)
    dest, tile_expert, n_used = moe_plan(route, counts)
    hs = moe_dispatch(h, dest)
    ys = moe_experts(hs, tile_expert, n_used, w_gate, w_up, w_down, layer)
    return moe_combine(ys, dest, route, xs, mod, layer, 5)


def kernel(x, c, ctx, c_ctx, ada_w, ada_b, norm1_g, norm2_g, final_g, attn_w_in, attn_w_out, na_rpb, q_gain, k_gain, s5_w_in, s5_lam_re, s5_lam_im, s5_log_dt, s5_b_re, s5_b_im, s5_c_re, s5_c_im, s5_d, s5_w_glu, moe_w_coarse, moe_b_coarse, moe_w_fine, moe_b_fine, moe_w_gate, moe_w_up, moe_w_down):
    xs = jnp.concatenate([x.reshape(T_LAT, D_MODEL), ctx.reshape(T_CTX, D_MODEL)], axis=0)
    c8 = jnp.concatenate([c, c_ctx[None, :], jnp.zeros((MOD_ROWS - BATCH - 1, D_MODEL), F32)], axis=0)
    mod = ada_mod(c8, ada_w, ada_b).reshape(DEPTH * MOD_ROWS, 1, 6 * D_MODEL)
    cos_t, sin_t = rope_tables()
    attn_w_in, attn_w_out, s5_w_in, s5_w_glu = (w.astype(BF16) for w in (attn_w_in, attn_w_out, s5_w_in, s5_w_glu))
    chunk_ops = jax.vmap(s5_weights)(s5_lam_re, s5_lam_im, s5_log_dt, s5_b_re, s5_b_im, s5_c_re, s5_c_im)
    na_bias = jax.vmap(na_bias_table)(na_rpb)
    for layer in range(DEPTH):
        if layer % 2 == 0:
            xs = attention_layer(xs, mod, layer, norm1_g, attn_w_in, attn_w_out, na_bias, q_gain, k_gain,
                                 cos_t, sin_t)
        else:
            xs = s5_layer(xs, mod, layer, norm1_g, s5_w_in, chunk_ops, s5_d, s5_w_glu)
        xs = moe_layer(xs, mod, layer, norm2_g, moe_w_coarse, moe_b_coarse, moe_w_fine, moe_b_fine,
                       moe_w_gate, moe_w_up, moe_w_down)
    return final_norm(xs, final_g, T_LAT).reshape(BATCH, SEQ, D_MODEL)
```

```python
import functools
import math

import jax
import jax.numpy as jnp
from jax import lax
from jax.experimental import pallas as pl
from jax.experimental.pallas import tpu as pltpu

F32 = jnp.float32
BF16 = jnp.bfloat16

D_MODEL = 2048
BATCH = 4
SEQ = 4096
DEPTH = 4
GRID_W = 64
CTX_LEN = 256
HEAD_DIM = 128
NA_HEADS = 8
NA_ROWS = 8
NA_COLS = 16
GQA_Q_HEADS = 8
GQA_KV_HEADS = 2
ROPE_THETA = 10000.0
ROPE_AXIS_DIM = HEAD_DIM // 2
ATTN_IN_COLS = (3 * NA_HEADS + GQA_Q_HEADS + 2 * GQA_KV_HEADS) * HEAD_DIM
S5_GROUP = 16
S5_GROUPS = D_MODEL // S5_GROUP
S5_STATE = 64
MOE_GROUPS = 4
MOE_EXPERTS_PER_GROUP = 8
MOE_EXPERTS = MOE_GROUPS * MOE_EXPERTS_PER_GROUP
MOE_FFN = D_MODEL // 4
RMS_EPS = 1e-6
NEG_INF = -1e30

T_LAT = BATCH * SEQ
T_CTX = BATCH * CTX_LEN
T_ALL = T_LAT + T_CTX
MOD_ROWS = 8
ATT_SCALE = HEAD_DIM ** -0.5

S5_CHUNK = 16
S5_SEQ = CTX_LEN + SEQ
S5_NCHUNK = S5_SEQ // S5_CHUNK
S5_CW = S5_CHUNK * S5_GROUP
S5_GB = 4

MOE_TILE = 512
MOE_SLOTS = T_ALL * 2
MOE_NT = MOE_SLOTS // MOE_TILE + MOE_EXPERTS

VMEM_LIMIT = 56 * 1024 * 1024


def _cp(*sem):
    return pltpu.CompilerParams(dimension_semantics=sem, vmem_limit_bytes=VMEM_LIMIT)


def _mod_row(i, tm):
    return jnp.minimum((i * tm) // SEQ, BATCH)


def _ada_kernel(c_ref, w_ref, b_ref, o_ref):
    s = jax.nn.silu(c_ref[...])
    o_ref[...] = jnp.dot(s, w_ref[...], precision=lax.Precision.HIGHEST,
                         preferred_element_type=F32) + b_ref[...]


def ada_mod(c8, ada_w, ada_b):
    tn = 1024
    n = ada_w.shape[-1]
    return pl.pallas_call(
        _ada_kernel,
        grid=(DEPTH, n // tn),
        in_specs=[pl.BlockSpec((MOD_ROWS, D_MODEL), lambda l, j: (0, 0)),
                  pl.BlockSpec((None, D_MODEL, tn), lambda l, j: (l, 0, j)),
                  pl.BlockSpec((None, 1, tn), lambda l, j: (l, 0, j))],
        out_specs=pl.BlockSpec((None, MOD_ROWS, tn), lambda l, j: (l, 0, j)),
        out_shape=jax.ShapeDtypeStruct((DEPTH, MOD_ROWS, n), F32),
        compiler_params=_cp("arbitrary", "arbitrary"),
        name="ada_mod",
    )(c8, ada_w, ada_b.reshape(DEPTH, 1, n))


def _norm_mod(x, g, sh, sc):
    y = x * lax.rsqrt(jnp.mean(x * x, axis=-1, keepdims=True) + RMS_EPS) * g
    return y * (1.0 + sc) + sh


def _mod_spec(layer, col, tm):
    return pl.BlockSpec((None, 1, D_MODEL),
                        lambda i, j: (layer * MOD_ROWS + _mod_row(i, tm), 0, col))


def _gate_spec(layer, col, tm, tn):
    return pl.BlockSpec((None, 1, tn),
                        lambda i, j: (layer * MOD_ROWS + _mod_row(i, tm), 0, col * (D_MODEL // tn) + j))


def _nm_mm_kernel(x_ref, g_ref, sh_ref, sc_ref, w_ref, o_ref, h_ref):
    @pl.when(pl.program_id(1) == 0)
    def _():
        h_ref[...] = _norm_mod(x_ref[...], g_ref[...], sh_ref[...], sc_ref[...]).astype(BF16)

    o_ref[...] = jnp.dot(h_ref[...], w_ref[...],
                         preferred_element_type=F32).astype(o_ref.dtype)


def norm_mod_matmul(x, g_all, mod, w_all, layer, wi, sh_col, sc_col, tm=1024, tn=512):
    t = x.shape[0]
    n = w_all.shape[-1]
    return pl.pallas_call(
        _nm_mm_kernel,
        grid=(t // tm, n // tn),
        in_specs=[pl.BlockSpec((tm, D_MODEL), lambda i, j: (i, 0)),
                  pl.BlockSpec((None, 1, D_MODEL), lambda i, j: (layer, 0, 0)),
                  _mod_spec(layer, sh_col, tm),
                  _mod_spec(layer, sc_col, tm),
                  pl.BlockSpec((None, D_MODEL, tn), lambda i, j: (wi, 0, j))],
        out_specs=pl.BlockSpec((tm, tn), lambda i, j: (i, j)),
        out_shape=jax.ShapeDtypeStruct((t, n), F32),
        scratch_shapes=[pltpu.VMEM((tm, D_MODEL), BF16)],
        compiler_params=_cp("arbitrary", "arbitrary"),
        name="norm_mod_matmul",
    )(x, g_all.reshape(DEPTH, 1, D_MODEL), mod, mod, w_all)


def _proj_res_kernel(a_ref, b_ref, wa_ref, wb_ref, x_ref, gate_ref, o_ref):
    y = jnp.dot(a_ref[...], wa_ref[...], preferred_element_type=F32)
    y = y + jnp.dot(b_ref[...], wb_ref[...], preferred_element_type=F32)
    o_ref[...] = x_ref[...] + gate_ref[...] * y


def proj_residual(oa, ob, w_all, wi, x, mod, layer, gate_col, tm=1024, tn=512):
    t, ka = oa.shape
    kb = ob.shape[1]
    assert ka == kb
    return pl.pallas_call(
        _proj_res_kernel,
        grid=(t // tm, D_MODEL // tn),
        in_specs=[pl.BlockSpec((tm, ka), lambda i, j: (i, 0)),
                  pl.BlockSpec((tm, kb), lambda i, j: (i, 0)),
                  pl.BlockSpec((None, ka, tn), lambda i, j: (wi, 0, j)),
                  pl.BlockSpec((None, kb, tn), lambda i, j: (wi, 1, j)),
                  pl.BlockSpec((tm, tn), lambda i, j: (i, j)),
                  _gate_spec(layer, gate_col, tm, tn)],
        out_specs=pl.BlockSpec((tm, tn), lambda i, j: (i, j)),
        out_shape=jax.ShapeDtypeStruct((t, D_MODEL), F32),
        compiler_params=_cp("arbitrary", "arbitrary"),
        name="proj_residual",
    )(oa, ob, w_all, w_all, x, mod)


def _gqa_prep_kernel(p_ref, gain_ref, cos_ref, sin_ref, o_ref):
    nqk = GQA_Q_HEADS + GQA_KV_HEADS
    nh = nqk + GQA_KV_HEADS
    cos = cos_ref[...]
    sin = sin_ref[...]
    lane = lax.broadcasted_iota(jnp.int32, cos.shape, 1)
    half = ROPE_AXIS_DIM // 2
    first_half = (lane % ROPE_AXIS_DIM) < half
    for h in range(nh):
        cols = slice(h * HEAD_DIM, (h + 1) * HEAD_DIM)
        x = p_ref[:, cols]
        if h < nqk:
            is_q = h < GQA_Q_HEADS
            gain = gain_ref[0 if is_q else 1] * (ATT_SCALE * LOG2E if is_q else 1.0)
            y = x * lax.rsqrt(jnp.mean(x * x, axis=-1, keepdims=True) + RMS_EPS) * gain
            partner = jnp.where(first_half, pltpu.roll(y, HEAD_DIM - half, 1), pltpu.roll(y, half, 1))
            x = y * cos + partner * sin
        o_ref[:, cols] = x.astype(BF16)


def gqa_prep(p, gains, cos_t, sin_t, tm=512):
    t = p.shape[0]
    nh = GQA_Q_HEADS + 2 * GQA_KV_HEADS
    w = nh * HEAD_DIM
    assert (3 * NA_HEADS * HEAD_DIM) % w == 0
    col_blk = 3 * NA_HEADS * HEAD_DIM // w
    lat_tiles = T_LAT // tm
    per_seq = SEQ // tm

    def tab_idx(i):
        return (jnp.where(i < lat_tiles, i % per_seq, per_seq), 0)

    return pl.pallas_call(
        _gqa_prep_kernel,
        grid=(t // tm,),
        in_specs=[pl.BlockSpec((tm, w), lambda i: (i, col_blk)),
                  pl.BlockSpec((2, 1, HEAD_DIM), lambda i: (0, 0, 0)),
                  pl.BlockSpec((tm, HEAD_DIM), tab_idx),
                  pl.BlockSpec((tm, HEAD_DIM), tab_idx)],
        out_specs=pl.BlockSpec((tm, w), lambda i: (i, 0)),
        out_shape=jax.ShapeDtypeStruct((t, w), BF16),
        compiler_params=_cp("arbitrary"),
        name="gqa_prep",
    )(p, gains, cos_t, sin_t)


def rope_tables(tm=512):
    t = jnp.arange(SEQ)
    row = (t // GRID_W).astype(F32)
    col = (t % GRID_W).astype(F32)
    inv = ROPE_THETA ** (-jnp.arange(0, ROPE_AXIS_DIM, 2, dtype=F32) / ROPE_AXIS_DIM)
    ar = row[:, None] * inv[None]
    ac = col[:, None] * inv[None]
    cos_t = jnp.concatenate([jnp.cos(ar), jnp.cos(ar), jnp.cos(ac), jnp.cos(ac)], axis=-1)
    sin_t = jnp.concatenate([-jnp.sin(ar), jnp.sin(ar), -jnp.sin(ac), jnp.sin(ac)], axis=-1)
    cos_t = jnp.concatenate([cos_t, jnp.ones((tm, HEAD_DIM), F32)], axis=0)
    sin_t = jnp.concatenate([sin_t, jnp.zeros((tm, HEAD_DIM), F32)], axis=0)
    return cos_t, sin_t


LOG2E = 1.4426950408889634
ATT_KCHUNK = 512


def _softmax_chunk(q, k, v, bias, m, l, acc):
    s = lax.dot_general(q, k.astype(BF16), (((1,), (1,)), ((), ())), preferred_element_type=F32)
    if bias is not None:
        s = s + bias
    m_new = jnp.maximum(m, jnp.max(s, axis=-1, keepdims=True))
    alpha = jnp.exp2(m - m_new)
    p = jnp.exp2(s - m_new)
    l = alpha * l + jnp.sum(p, axis=-1, keepdims=True)
    acc = alpha * acc + jnp.dot(p.astype(BF16), v.astype(BF16), preferred_element_type=F32)
    return m_new, l, acc


def _attn_kernel(*refs, nseg, group, scale):
    q_ref = refs[0]
    k_refs = refs[1:1 + nseg]
    v_refs = refs[1 + nseg:1 + 2 * nseg]
    o_ref = refs[1 + 2 * nseg]
    tq = q_ref.shape[0]
    q = jnp.concatenate([q_ref[:, g * HEAD_DIM:(g + 1) * HEAD_DIM] for g in range(group)], axis=0)
    if scale != 1.0:
        q = q.astype(F32) * scale
    q = q.astype(BF16)
    m = jnp.full((q.shape[0], 1), -jnp.inf, F32)
    l = jnp.zeros((q.shape[0], 1), F32)
    acc = jnp.zeros((q.shape[0], HEAD_DIM), F32)
    for k_ref, v_ref in zip(k_refs, v_refs):
        nk = k_ref.shape[0]
        for c0 in range(0, nk, ATT_KCHUNK):
            cs = min(ATT_KCHUNK, nk - c0)
            m, l, acc = _softmax_chunk(q, k_ref[c0:c0 + cs, :], v_ref[c0:c0 + cs, :], None, m, l, acc)
    o = acc / l
    for g in range(group):
        o_ref[:, g * HEAD_DIM:(g + 1) * HEAD_DIM] = o[g * tq:(g + 1) * tq].astype(o_ref.dtype)


def attention(q_arr, q_idx, kv_arrs, k_idx, v_idx, kv_rows, grid, tq, group, scale, out_rows, out_cols, o_idx, name):
    nseg = len(kv_rows)
    in_specs = [pl.BlockSpec((tq, group * HEAD_DIM), q_idx)]
    in_specs += [pl.BlockSpec((kv_rows[i], HEAD_DIM), k_idx[i]) for i in range(nseg)]
    in_specs += [pl.BlockSpec((kv_rows[i], HEAD_DIM), v_idx[i]) for i in range(nseg)]
    return pl.pallas_call(
        functools.partial(_attn_kernel, nseg=nseg, group=group, scale=scale),
        grid=grid,
        in_specs=in_specs,
        out_specs=pl.BlockSpec((tq, group * HEAD_DIM), o_idx),
        out_shape=jax.ShapeDtypeStruct((out_rows, out_cols), BF16),
        compiler_params=_cp(*(["arbitrary"] * len(grid))),
        name=name,
    )(q_arr, *kv_arrs, *kv_arrs)


NA_QROWS = 8
NA_KROWS = 16
NA_TQ = NA_QROWS * GRID_W
NA_TK = NA_KROWS * GRID_W
GRID_H = SEQ // GRID_W


def _na_kernel(q_ref, k_ref, v_ref, kc_ref, vc_ref, bias_ref, o_ref):
    rb = pl.program_id(2)
    w0 = jnp.clip(NA_QROWS * rb - (NA_KROWS - NA_QROWS) // 2, 0, GRID_H - NA_KROWS) * GRID_W
    w0 = pl.multiple_of(w0, 4 * GRID_W)
    q = (q_ref[...] * (ATT_SCALE * LOG2E)).astype(BF16)
    m = jnp.full((NA_TQ, 1), -jnp.inf, F32)
    l = jnp.zeros((NA_TQ, 1), F32)
    acc = jnp.zeros((NA_TQ, HEAD_DIM), F32)
    m, l, acc = _softmax_chunk(q, kc_ref[...], vc_ref[...], None, m, l, acc)
    for c0 in range(0, NA_TK, ATT_KCHUNK):
        kw = k_ref[pl.ds(w0 + c0, ATT_KCHUNK), :]
        vw = v_ref[pl.ds(w0 + c0, ATT_KCHUNK), :]
        m, l, acc = _softmax_chunk(q, kw, vw, bias_ref[:, c0:c0 + ATT_KCHUNK], m, l, acc)
    o_ref[...] = (acc / l).astype(o_ref.dtype)


def na_bias_table(rpb):
    aw = jnp.arange(GRID_W)
    col_start = jnp.clip(aw - NA_COLS // 2, 0, GRID_W - NA_COLS)
    col_ok = (aw[None, :] >= col_start[:, None]) & (aw[None, :] < col_start[:, None] + NA_COLS)
    off_c = jnp.clip(aw[None, :] - aw[:, None] + (NA_COLS - 1), 0, 2 * NA_COLS - 2)
    nh, nr, nc = rpb.shape
    pick = (off_c.reshape(1, -1) == jnp.arange(nc)[:, None]).astype(F32)
    tiles = jnp.dot(rpb.astype(F32).reshape(nh * nr, nc), pick, precision=lax.Precision.HIGHEST)
    tiles = jnp.where(col_ok.reshape(1, -1), tiles * LOG2E, NEG_INF).reshape(nh, nr, GRID_W, GRID_W)
    blocked = jnp.full((nh, 1, GRID_W, GRID_W), NEG_INF, F32)
    tiles = jnp.concatenate([tiles, blocked], axis=1)
    nblk = GRID_H // NA_QROWS
    out = []
    for rb in (0, nblk // 2, nblk - 1):
        w0 = min(max(NA_QROWS * rb - (NA_KROWS - NA_QROWS) // 2, 0), GRID_H - NA_KROWS)
        picks = []
        for qr in range(NA_QROWS * rb, NA_QROWS * (rb + 1)):
            rs = min(max(qr - NA_ROWS // 2, 0), GRID_H - NA_ROWS)
            for kr in range(w0, w0 + NA_KROWS):
                picks.append(kr - qr + NA_ROWS - 1 if rs <= kr < rs + NA_ROWS else nr)
        b = jnp.stack([tiles[:, k] for k in picks], axis=1)
        b = b.reshape(nh, NA_QROWS, NA_KROWS, GRID_W, GRID_W).transpose(0, 1, 3, 2, 4)
        out.append(b.reshape(nh, NA_TQ, NA_TK))
    return jnp.stack(out, axis=1)


def na_attention(p, bias):
    nblk = GRID_H // NA_QROWS
    ctx_blk0 = T_LAT // CTX_LEN

    def pat(rb):
        return jnp.where(rb == 0, 0, jnp.where(rb == nblk - 1, 2, 1))

    return pl.pallas_call(
        _na_kernel,
        grid=(NA_HEADS, BATCH, nblk),
        in_specs=[pl.BlockSpec((NA_TQ, HEAD_DIM), lambda h, b, r: (b * nblk + r, h)),
                  pl.BlockSpec((SEQ, HEAD_DIM), lambda h, b, r: (b, NA_HEADS + h)),
                  pl.BlockSpec((SEQ, HEAD_DIM), lambda h, b, r: (b, 2 * NA_HEADS + h)),
                  pl.BlockSpec((CTX_LEN, HEAD_DIM), lambda h, b, r: (ctx_blk0 + b, NA_HEADS + h)),
                  pl.BlockSpec((CTX_LEN, HEAD_DIM), lambda h, b, r: (ctx_blk0 + b, 2 * NA_HEADS + h)),
                  pl.BlockSpec((None, None, NA_TQ, NA_TK), lambda h, b, r: (h, pat(r), 0, 0))],
        out_specs=pl.BlockSpec((NA_TQ, HEAD_DIM), lambda h, b, r: (b * nblk + r, h)),
        out_shape=jax.ShapeDtypeStruct((T_LAT, NA_HEADS * HEAD_DIM), BF16),
        compiler_params=_cp("arbitrary", "arbitrary", "arbitrary"),
        name="na_attention",
    )(p, p, p, p, p, bias)


def s5_weights(lam_re, lam_im, log_dt, b_re, b_im, c_re, c_im):
    hi = lax.Precision.HIGHEST
    L = S5_CHUNK
    lr = jnp.minimum(lam_re.astype(F32), -1e-4)
    li = lam_im.astype(F32)
    dt = jnp.exp(log_dt.astype(F32))[..., None]
    mag = jnp.exp(lr * dt)
    ar = mag * jnp.cos(li * dt)
    ai = mag * jnp.sin(li * dt)
    den = lr * lr + li * li
    xr = ar - 1.0
    fr = (xr * lr + ai * li) / den
    fi = (ai * lr - xr * li) / den
    br = b_re.astype(F32)
    bi = b_im.astype(F32)
    bbr = fr[..., None] * br - fi[..., None] * bi
    bbi = fr[..., None] * bi + fi[..., None] * br
    k = jnp.arange(L + 1, dtype=F32)[:, None, None, None]
    pm = jnp.exp(lr * dt * k)
    pr = pm * jnp.cos(li * dt * k)
    pi = pm * jnp.sin(li * dt * k)
    e_r = pr[..., None] * bbr - pi[..., None] * bbi
    e_i = pr[..., None] * bbi + pi[..., None] * bbr
    cr = c_re.astype(F32)
    ci = c_im.astype(F32)
    kern = (jnp.einsum('dgop,kdgpi->kdgoi', cr, e_r[:L], precision=hi)
            - jnp.einsum('dgop,kdgpi->kdgoi', ci, e_i[:L], precision=hi))
    s_i = jnp.arange(L)
    ng = lr.shape[1]
    lag = s_i[None, :, None] - s_i[:, None, None]
    place_f = (lag == s_i[None, None, :]).astype(F32)
    place = jnp.stack([place_f, place_f[::-1, ::-1]], axis=-1)
    toep = jnp.einsum('stkd,kdgoi->gsito', place, kern, precision=hi)
    w_toep_sum = toep.reshape(ng, S5_CW, S5_CW)
    st_r = e_r[L - 1 - s_i].transpose(1, 2, 0, 4, 3).reshape(2, ng, S5_CW, S5_STATE)
    st_i = e_i[L - 1 - s_i].transpose(1, 2, 0, 4, 3).reshape(2, ng, S5_CW, S5_STATE)
    w_state = jnp.concatenate([st_r, st_i], axis=-1)
    w_state_sw = jnp.concatenate([st_i, st_r], axis=-1)
    qr = pr[1:, :, :, None, :]
    qi = pi[1:, :, :, None, :]
    d_r = cr[None] * qr - ci[None] * qi
    d_i = cr[None] * qi + ci[None] * qr
    wo_r = d_r.transpose(1, 2, 4, 0, 3).reshape(2, ng, S5_STATE, S5_CW)
    wo_i = (-d_i).transpose(1, 2, 4, 0, 3).reshape(2, ng, S5_STATE, S5_CW)
    w_out = jnp.concatenate([wo_r, wo_i], axis=2)
    flip_rows = lambda w: w.reshape(ng, L, S5_GROUP, w.shape[-1])[:, ::-1].reshape(w.shape)
    flip_cols = lambda w: w.reshape(ng, w.shape[1], L, S5_GROUP)[:, :, ::-1].reshape(w.shape)
    wst = jnp.concatenate([w_state[0], flip_rows(w_state[1]),
                           w_state_sw[0], flip_rows(w_state_sw[1])], axis=-1).astype(BF16)
    wy = jnp.concatenate([w_toep_sum, w_out[0], flip_cols(w_out[1])], axis=1).astype(BF16)
    a1 = jnp.concatenate([pr[L], pr[L]], axis=-1)
    a2 = jnp.concatenate([-pi[L], pi[L]], axis=-1)
    per_row = lambda a: jnp.repeat(a.transpose(1, 0, 2), BATCH, axis=1)
    return wst, wy, per_row(a1), per_row(a2)


S5_CTX_CHUNKS = CTX_LEN // S5_CHUNK
S5_PACK = HEAD_DIM // S5_GROUP
S5_PACKS = S5_GROUPS // S5_PACK
S5_BROWS = S5_NCHUNK


def s5_lane_permutation():
    tl, g, hh = jnp.meshgrid(jnp.arange(S5_CHUNK), jnp.arange(S5_PACK), jnp.arange(S5_GROUP), indexing="ij")
    dst = (g * S5_CW + tl * S5_GROUP + hh).reshape(-1)
    return (dst[:, None] == jnp.arange(S5_PACK * S5_CW)[None, :]).astype(BF16)


S5_SB = 2


def _s5_pack_kernel(lat_ref, ctx_ref, p_ref, o_ref):
    cols = []
    for tl in range(S5_CHUNK):
        rows = []
        for b in range(S5_SB):
            rows.append(ctx_ref[pl.ds(b * CTX_LEN + tl, S5_CTX_CHUNKS, stride=S5_CHUNK), :])
            rows.append(lat_ref[pl.ds(b * SEQ + tl, SEQ // S5_CHUNK, stride=S5_CHUNK), :])
        cols.append(jnp.concatenate(rows, axis=0))
    xcat = jnp.concatenate(cols, axis=1).astype(BF16)
    o_ref[...] = jnp.dot(xcat, p_ref[...], preferred_element_type=F32).astype(BF16)


def s5_pack(u, perm):
    ctx0 = T_LAT // (S5_SB * CTX_LEN)
    w = S5_PACK * S5_CW
    return pl.pallas_call(
        _s5_pack_kernel,
        grid=(S5_PACKS, BATCH // S5_SB),
        in_specs=[pl.BlockSpec((S5_SB * SEQ, HEAD_DIM), lambda k, b: (b, k)),
                  pl.BlockSpec((S5_SB * CTX_LEN, HEAD_DIM), lambda k, b: (ctx0 + b, k)),
                  pl.BlockSpec((w, w), lambda k, b: (0, 0))],
        out_specs=pl.BlockSpec((None, S5_SB * S5_BROWS, w), lambda k, b: (k, b, 0)),
        out_shape=jax.ShapeDtypeStruct((S5_PACKS, BATCH * S5_BROWS, w), BF16),
        compiler_params=_cp("arbitrary", "arbitrary"),
        name="s5_pack",
    )(u, u, perm)


def _s5_kernel(u_ref, wst_ref, wy_ref, a1_ref, a2_ref, y_ref, s_ref, ssw_ref, hf_ref, hr_ref):
    gb = wst_ref.shape[0]
    nc = S5_NCHUNK
    sw = 2 * S5_STATE
    for j in range(gb):
        ug = u_ref[:, j * S5_CW:(j + 1) * S5_CW]
        s4 = jnp.dot(ug, wst_ref[j], preferred_element_type=F32)
        for b in range(BATCH):
            rows = s4[b * nc:(b + 1) * nc]
            s_ref[:, j, b, :] = rows[:, 0:sw]
            s_ref[:, j, BATCH + b, :] = rows[:, sw:2 * sw]
            ssw_ref[:, j, b, :] = rows[:, 2 * sw:3 * sw]
            ssw_ref[:, j, BATCH + b, :] = rows[:, 3 * sw:4 * sw]
    a1 = a1_ref[...]
    a2 = a2_ref[...]
    fwd = lax.broadcasted_iota(jnp.int32, (gb, 2 * BATCH, sw), 1) < BATCH

    def step(i, carry):
        h, hs = carry
        ri = jnp.where(i < S5_CTX_CHUNKS, S5_CTX_CHUNKS - 1 - i, nc - 1 + S5_CTX_CHUNKS - i)
        hf_ref[i] = h
        hr_ref[ri] = h
        s = jnp.where(fwd, s_ref[i], s_ref[ri])
        ssw = jnp.where(fwd, ssw_ref[i], ssw_ref[ri])
        return a1 * h + a2 * hs + s, a1 * hs - a2 * h + ssw

    z = jnp.zeros((gb, 2 * BATCH, sw), F32)
    lax.fori_loop(0, nc, step, (z, z), unroll=4)
    for j in range(gb):
        ug = u_ref[:, j * S5_CW:(j + 1) * S5_CW]
        hf = jnp.concatenate([hf_ref[:, j, b, :] for b in range(BATCH)], axis=0)
        hr = jnp.concatenate([hr_ref[:, j, BATCH + b, :] for b in range(BATCH)], axis=0)
        lhs = jnp.concatenate([ug, hf.astype(BF16), hr.astype(BF16)], axis=1)
        y_ref[:, j * S5_CW:(j + 1) * S5_CW] = jnp.dot(lhs, wy_ref[j], preferred_element_type=F32)


def s5_core(u_packed, wst, wy, a1, a2):
    gb = S5_GB
    per_pack = S5_PACK // gb
    rows = u_packed.shape[1]
    sw = 2 * S5_STATE
    blk = lambda *shape: pl.BlockSpec((gb,) + shape, lambda i: (i,) + (0,) * len(shape))
    io = pl.BlockSpec((None, rows, gb * S5_CW), lambda i: (i // per_pack, 0, i % per_pack))
    state = pltpu.VMEM((S5_NCHUNK, gb, 2 * BATCH, sw), F32)
    return pl.pallas_call(
        _s5_kernel,
        grid=(S5_GROUPS // gb,),
        in_specs=[io, blk(S5_CW, 4 * sw), blk(S5_CW + 2 * sw, S5_CW), blk(2 * BATCH, sw), blk(2 * BATCH, sw)],
        out_specs=io,
        out_shape=jax.ShapeDtypeStruct(u_packed.shape, F32),
        scratch_shapes=[state, state, state, state],
        compiler_params=_cp("arbitrary"),
        name="s5_core",
    )(u_packed, wst, wy, a1, a2)


def _s5_unpack_kernel(y_ref, pt_ref, lat_ref, ctx_ref):
    y = y_ref[...]
    hi = y.astype(BF16)
    lo = (y - hi.astype(F32)).astype(BF16)
    yn = (jnp.dot(hi, pt_ref[...], preferred_element_type=F32)
          + jnp.dot(lo, pt_ref[...], preferred_element_type=F32))
    for tl in range(S5_CHUNK):
        piece = yn[:, tl * HEAD_DIM:(tl + 1) * HEAD_DIM]
        for b in range(S5_SB):
            r0 = b * S5_BROWS
            ctx_ref[pl.ds(b * CTX_LEN + tl, S5_CTX_CHUNKS, stride=S5_CHUNK), :] = piece[r0:r0 + S5_CTX_CHUNKS]
            lat_ref[pl.ds(b * SEQ + tl, SEQ // S5_CHUNK, stride=S5_CHUNK), :] = (
                piece[r0 + S5_CTX_CHUNKS:r0 + S5_BROWS])


def s5_unpack(y_packed, perm_t):
    w = S5_PACK * S5_CW
    return pl.pallas_call(
        _s5_unpack_kernel,
        grid=(S5_PACKS, BATCH // S5_SB),
        in_specs=[pl.BlockSpec((None, S5_SB * S5_BROWS, w), lambda k, b: (k, b, 0)),
                  pl.BlockSpec((w, w), lambda k, b: (0, 0))],
        out_specs=[pl.BlockSpec((S5_SB * SEQ, HEAD_DIM), lambda k, b: (b, k)),
                   pl.BlockSpec((S5_SB * CTX_LEN, HEAD_DIM), lambda k, b: (b, k))],
        out_shape=[jax.ShapeDtypeStruct((T_LAT, D_MODEL), F32), jax.ShapeDtypeStruct((T_CTX, D_MODEL), F32)],
        compiler_params=_cp("arbitrary", "arbitrary"),
        name="s5_unpack",
    )(y_packed, perm_t)


def _glu_kernel(u_ref, y_ref, d_ref, wa_ref, wg_ref, x_ref, gate_ref, o_ref, h_ref):
    @pl.when(pl.program_id(1) == 0)
    def _():
        y = u_ref[...] * d_ref[...] + y_ref[...]
        h_ref[...] = jax.nn.gelu(y).astype(BF16)

    h = h_ref[...]
    za = jnp.dot(h, wa_ref[...], preferred_element_type=F32)
    zg = jnp.dot(h, wg_ref[...], preferred_element_type=F32)
    o_ref[...] = x_ref[...] + gate_ref[...] * (za * jax.nn.sigmoid(zg))


def glu_residual(u, y, d_all, w_all, wi, x, mod, layer, gate_col, tm=1024, tn=256):
    t = u.shape[0]
    nj = D_MODEL // tn
    row = pl.BlockSpec((tm, D_MODEL), lambda i, j: (i, 0))
    return pl.pallas_call(
        _glu_kernel,
        grid=(t // tm, nj),
        in_specs=[row, row,
                  pl.BlockSpec((None, 1, D_MODEL), lambda i, j: (wi, 0, 0)),
                  pl.BlockSpec((None, D_MODEL, tn), lambda i, j: (wi, 0, j)),
                  pl.BlockSpec((None, D_MODEL, tn), lambda i, j: (wi, 0, nj + j)),
                  pl.BlockSpec((tm, tn), lambda i, j: (i, j)),
                  _gate_spec(layer, gate_col, tm, tn)],
        out_specs=pl.BlockSpec((tm, tn), lambda i, j: (i, j)),
        out_shape=jax.ShapeDtypeStruct((t, D_MODEL), F32),
        scratch_shapes=[pltpu.VMEM((tm, D_MODEL), BF16)],
        compiler_params=_cp("arbitrary", "arbitrary"),
        name="glu_residual",
    )(u, y, d_all.reshape(-1, 1, D_MODEL), w_all, w_all, x, mod)


ROUTE_LANES = 128
MOE_ROW_TILES = D_MODEL // 2 // HEAD_DIM


def _route_kernel(x_ref, g_ref, sh_ref, sc_ref, wr_ref, br_ref, h_ref, r_ref, cnt_ref, carry_ref):
    @pl.when(pl.program_id(0) == 0)
    def _():
        carry_ref[...] = jnp.zeros_like(carry_ref)

    h = _norm_mod(x_ref[...], g_ref[...], sh_ref[...], sc_ref[...])
    bits = lax.bitcast_convert_type(h.astype(BF16).astype(F32), jnp.uint32)
    half = h.shape[1] // 2
    packed = (bits[:, :half] >> 16) | (bits[:, half:] & jnp.uint32(0xFFFF0000))
    for j in range(half // HEAD_DIM):
        h_ref[:, j, :] = packed[:, j * HEAD_DIM:(j + 1) * HEAD_DIM]
    lg = jnp.dot(h, wr_ref[...], precision=lax.Precision.HIGHEST, preferred_element_type=F32) + br_ref[...]
    lane = lax.broadcasted_iota(jnp.int32, lg.shape, 1)
    ninf = -jnp.inf
    coarse = lane < MOE_GROUPS
    lc = jnp.where(coarse, lg, ninf)
    mc = jnp.max(lc, axis=-1, keepdims=True)
    g_sel = jnp.min(jnp.where(lc == mc, lane, ROUTE_LANES), axis=-1, keepdims=True)
    p_sel = 1.0 / jnp.sum(jnp.where(coarse, jnp.exp(lc - mc), 0.0), axis=-1, keepdims=True)
    lo = MOE_GROUPS + MOE_EXPERTS_PER_GROUP * g_sel
    lf = jnp.where((lane >= lo) & (lane < lo + MOE_EXPERTS_PER_GROUP), lg, ninf)
    v0 = jnp.max(lf, axis=-1, keepdims=True)
    i0 = jnp.min(jnp.where(lf == v0, lane, ROUTE_LANES), axis=-1, keepdims=True)
    lf2 = jnp.where(lane == i0, ninf, lf)
    v1 = jnp.max(lf2, axis=-1, keepdims=True)
    i1 = jnp.min(jnp.where(lf2 == v1, lane, ROUTE_LANES), axis=-1, keepdims=True)
    e1 = jnp.exp(v1 - v0)
    w0 = p_sel / (1.0 + e1)
    w1 = w0 * e1
    tm = lg.shape[0]
    lower = (lax.broadcasted_iota(jnp.int32, (tm, tm), 1)
             < lax.broadcasted_iota(jnp.int32, (tm, tm), 0)).astype(BF16)
    carry = carry_ref[...]
    hot0 = (lane == i0).astype(F32)
    before0 = carry + jnp.dot(lower, hot0.astype(BF16), preferred_element_type=F32)
    rank0 = jnp.sum(hot0 * before0, axis=-1, keepdims=True)
    carry = carry + jnp.sum(hot0, axis=0, keepdims=True)
    hot1 = (lane == i1).astype(F32)
    before1 = carry + jnp.dot(lower, hot1.astype(BF16), preferred_element_type=F32)
    rank1 = jnp.sum(hot1 * before1, axis=-1, keepdims=True)
    carry = carry + jnp.sum(hot1, axis=0, keepdims=True)
    carry_ref[...] = carry
    cnt_ref[...] = carry
    cols = [(i0 - MOE_GROUPS).astype(F32), (i1 - MOE_GROUPS).astype(F32), w0, w1, rank0, rank1]
    r = jnp.zeros_like(lg)
    for c, val in enumerate(cols):
        r = jnp.where(lane == c, val, r)
    r_ref[...] = r


ROUTE_E0, ROUTE_E1, ROUTE_W0, ROUTE_W1, ROUTE_R0, ROUTE_R1 = range(6)


def moe_route(x, g_all, mod, layer, sh_col, sc_col, w_route, b_route, tm=512):
    t = x.shape[0]
    mspec = lambda col: pl.BlockSpec((None, 1, D_MODEL),
                                     lambda i: (layer * MOD_ROWS + _mod_row(i, tm), 0, col))
    return pl.pallas_call(
        _route_kernel,
        grid=(t // tm,),
        in_specs=[pl.BlockSpec((tm, D_MODEL), lambda i: (i, 0)),
                  pl.BlockSpec((None, 1, D_MODEL), lambda i: (layer, 0, 0)),
                  mspec(sh_col), mspec(sc_col),
                  pl.BlockSpec((D_MODEL, ROUTE_LANES), lambda i: (0, 0)),
                  pl.BlockSpec((1, ROUTE_LANES), lambda i: (0, 0))],
        out_specs=[pl.BlockSpec((tm, MOE_ROW_TILES, HEAD_DIM), lambda i: (i, 0, 0)),
                   pl.BlockSpec((tm, ROUTE_LANES), lambda i: (i, 0)),
                   pl.BlockSpec((1, ROUTE_LANES), lambda i: (0, 0))],
        out_shape=[jax.ShapeDtypeStruct((t, MOE_ROW_TILES, HEAD_DIM), jnp.uint32),
                   jax.ShapeDtypeStruct((t, ROUTE_LANES), F32),
                   jax.ShapeDtypeStruct((1, ROUTE_LANES), F32)],
        scratch_shapes=[pltpu.VMEM((1, ROUTE_LANES), F32)],
        compiler_params=_cp("arbitrary"),
        name="moe_route",
    )(x, g_all.reshape(DEPTH, 1, D_MODEL), mod, mod, w_route, b_route)


def moe_plan(route, counts):
    cnt = counts[0, MOE_GROUPS:MOE_GROUPS + MOE_EXPERTS].astype(jnp.int32)
    padded = ((cnt + MOE_TILE - 1) // MOE_TILE) * MOE_TILE
    pad_end = jnp.cumsum(padded)
    pad_off = pad_end - padded
    e = route[:, ROUTE_E0:ROUTE_E1 + 1].astype(jnp.int32)
    rank = route[:, ROUTE_R0:ROUTE_R1 + 1].astype(jnp.int32)
    hot = e[:, :, None] == jnp.arange(MOE_EXPERTS, dtype=jnp.int32)[None, None, :]
    dest = jnp.sum(jnp.where(hot, pad_off[None, None, :], 0), axis=-1) + rank
    n_used = (pad_end[-1] // MOE_TILE).astype(jnp.int32).reshape(1)
    tile_start = jnp.arange(MOE_NT, dtype=jnp.int32) * MOE_TILE
    tile_expert = jnp.sum((pad_end[None, :] <= tile_start[:, None]).astype(jnp.int32), axis=1)
    tile_expert = jnp.minimum(tile_expert, MOE_EXPERTS - 1).astype(jnp.int32)
    return dest, tile_expert, n_used


def _dispatch_kernel(dest_ref, h_ref, slots_hbm, hs_hbm, sem):
    del slots_hbm
    tm = h_ref.shape[0]

    def row_copy(r, d):
        return pltpu.make_async_copy(h_ref.at[pl.ds(r, 1)], hs_hbm.at[pl.ds(d, 1)], sem)

    def issue(r, c):
        row_copy(r, dest_ref[0, 0, 2 * r]).start(priority=0)
        row_copy(r, dest_ref[0, 0, 2 * r + 1]).start(priority=1)
        return c
    lax.fori_loop(0, tm, issue, 0, unroll=8)

    def drain(r, c):
        row_copy(r, 0).wait()
        row_copy(r, 0).wait()
        return c
    lax.fori_loop(0, tm, drain, 0, unroll=8)


def moe_slot_buffer():
    return jnp.zeros((MOE_NT * MOE_TILE, MOE_ROW_TILES, HEAD_DIM), jnp.uint32)


def moe_dispatch(h, dest, slots, tm=512):
    t = h.shape[0]
    return pl.pallas_call(
        _dispatch_kernel,
        grid=(t // tm,),
        in_specs=[pl.BlockSpec((1, 1, 2 * tm), lambda i: (i, 0, 0), memory_space=pltpu.SMEM),
                  pl.BlockSpec((tm,) + h.shape[1:], lambda i: (i, 0, 0)),
                  pl.BlockSpec(memory_space=pl.ANY)],
        out_specs=pl.BlockSpec(memory_space=pl.ANY),
        out_shape=jax.ShapeDtypeStruct(slots.shape, h.dtype),
        scratch_shapes=[pltpu.SemaphoreType.DMA],
        input_output_aliases={2: 0},
        compiler_params=_cp("arbitrary"),
        name="moe_dispatch",
    )(dest.reshape(t // tm, 1, 2 * tm), h, slots)


def _expert_kernel(te_ref, nu_ref, hs_ref, wg_ref, wu_ref, wd_ref, o_ref, wgb, wub, wdb):
    i = pl.program_id(0)

    @pl.when(i < nu_ref[0])
    def _():
        @pl.when((i == 0) | (te_ref[i] != te_ref[jnp.maximum(i - 1, 0)]))
        def _():
            wgb[...] = wg_ref[...].astype(BF16)
            wub[...] = wu_ref[...].astype(BF16)
            wdb[...] = wd_ref[...].astype(BF16)

        w = jnp.concatenate([hs_ref[:, j, :] for j in range(MOE_ROW_TILES)], axis=1)
        lo = lax.bitcast_convert_type(w << 16, F32).astype(BF16)
        hi = lax.bitcast_convert_type(w & jnp.uint32(0xFFFF0000), F32).astype(BF16)
        h = jnp.concatenate([lo, hi], axis=1)
        g = jnp.dot(h, wgb[...], preferred_element_type=F32)
        u = jnp.dot(h, wub[...], preferred_element_type=F32)
        hid = (jax.nn.silu(g) * u).astype(BF16)
        o_ref[...] = jnp.dot(hid, wdb[...], preferred_element_type=F32)

    @pl.when(i >= nu_ref[0])
    def _():
        o_ref[...] = jnp.zeros_like(o_ref)


def moe_experts(hs, tile_expert, n_used, w_gate, w_up, w_down, layer):
    def wspec(shape):
        return pl.BlockSpec((None, None, None) + shape,
                            lambda i, te, nu: (layer, te[i] // MOE_EXPERTS_PER_GROUP,
                                               te[i] % MOE_EXPERTS_PER_GROUP, 0, 0))

    grid_spec = pltpu.PrefetchScalarGridSpec(
        num_scalar_prefetch=2,
        grid=(MOE_NT,),
        in_specs=[pl.BlockSpec((MOE_TILE, MOE_ROW_TILES, HEAD_DIM),
                               lambda i, te, nu: (jnp.minimum(i, nu[0] - 1), 0, 0)),
                  wspec((D_MODEL, MOE_FFN)), wspec((D_MODEL, MOE_FFN)), wspec((MOE_FFN, D_MODEL))],
        out_specs=pl.BlockSpec((MOE_TILE, D_MODEL), lambda i, te, nu: (i, 0)),
        scratch_shapes=[pltpu.VMEM((D_MODEL, MOE_FFN), BF16),
                        pltpu.VMEM((D_MODEL, MOE_FFN), BF16),
                        pltpu.VMEM((MOE_FFN, D_MODEL), BF16)])
    return pl.pallas_call(
        _expert_kernel,
        grid_spec=grid_spec,
        out_shape=jax.ShapeDtypeStruct((MOE_NT * MOE_TILE, D_MODEL), F32),
        compiler_params=_cp("arbitrary"),
        name="moe_experts",
    )(tile_expert, n_used, hs, w_gate, w_up, w_down)


def _combine_kernel(dest_ref, destn_ref, ys_hbm, r_ref, x_ref, gate_ref, o_ref, buf, sem):
    i = pl.program_id(0)
    n = pl.num_programs(0)
    tm = x_ref.shape[0]

    def row_copy(d, slot, k, r):
        return pltpu.make_async_copy(ys_hbm.at[pl.ds(d, 1), :], buf.at[slot, k, pl.ds(r, 1), :], sem.at[slot])

    def issue(dref, slot):
        def body(r, c):
            row_copy(dref[0, 0, 2 * r], slot, 0, r).start(priority=0)
            row_copy(dref[0, 0, 2 * r + 1], slot, 1, r).start(priority=1)
            return c
        lax.fori_loop(0, tm, body, 0, unroll=8)

    @pl.when(i == 0)
    def _():
        issue(dest_ref, 0)

    @pl.when(i + 1 < n)
    def _():
        issue(destn_ref, (i + 1) % 2)

    slot = i % 2

    def drain(r, c):
        row_copy(0, slot, 0, r).wait()
        row_copy(0, slot, 1, r).wait()
        return c
    lax.fori_loop(0, tm, drain, 0, unroll=8)
    r = r_ref[...]
    y = r[:, ROUTE_W0:ROUTE_W0 + 1] * buf[slot, 0] + r[:, ROUTE_W1:ROUTE_W1 + 1] * buf[slot, 1]
    o_ref[...] = x_ref[...] + gate_ref[...] * y


def moe_combine(ys, dest, route, x, mod, layer, gate_col, tm=256):
    t = x.shape[0]
    row = pl.BlockSpec((tm, D_MODEL), lambda i: (i, 0))
    nblk = t // tm
    smem = lambda off: pl.BlockSpec((1, 1, 2 * tm), lambda i: (jnp.minimum(i + off, nblk - 1), 0, 0),
                                    memory_space=pltpu.SMEM)
    dest3 = dest.reshape(nblk, 1, 2 * tm)
    return pl.pallas_call(
        _combine_kernel,
        grid=(nblk,),
        in_specs=[smem(0), smem(1),
                  pl.BlockSpec(memory_space=pl.ANY),
                  pl.BlockSpec((tm, ROUTE_LANES), lambda i: (i, 0)),
                  row,
                  pl.BlockSpec((None, 1, D_MODEL),
                               lambda i: (layer * MOD_ROWS + _mod_row(i, tm), 0, gate_col))],
        out_specs=row,
        out_shape=jax.ShapeDtypeStruct((t, D_MODEL), F32),
        scratch_shapes=[pltpu.VMEM((2, 2, tm, D_MODEL), F32), pltpu.SemaphoreType.DMA((2,))],
        compiler_params=_cp("arbitrary"),
        name="moe_combine",
    )(dest3, dest3, ys, route, x, mod)


def _final_norm_kernel(x_ref, g_ref, o_ref):
    x = x_ref[...]
    o_ref[...] = x * lax.rsqrt(jnp.mean(x * x, axis=-1, keepdims=True) + RMS_EPS) * g_ref[...]


def final_norm(x, g, rows, tm=512):
    row = pl.BlockSpec((tm, D_MODEL), lambda i: (i, 0))
    return pl.pallas_call(
        _final_norm_kernel,
        grid=(rows // tm,),
        in_specs=[row, pl.BlockSpec((1, D_MODEL), lambda i: (0, 0))],
        out_specs=row,
        out_shape=jax.ShapeDtypeStruct((rows, D_MODEL), F32),
        compiler_params=_cp("arbitrary"),
        name="final_norm",
    )(x, g.reshape(1, D_MODEL))


def attention_layer(xs, mod, layer, norm1_g, w_in, w_out, na_bias, q_gain, k_gain, cos_t, sin_t):
    li = layer // 2
    p = norm_mod_matmul(xs, norm1_g, mod, w_in, layer, li, 0, 1)
    gains = jnp.stack([q_gain[li], k_gain[li]]).reshape(2, 1, HEAD_DIM)
    qkv = gqa_prep(p, gains, cos_t, sin_t)
    kcol, vcol = GQA_Q_HEADS, GQA_Q_HEADS + GQA_KV_HEADS
    grp = GQA_Q_HEADS // GQA_KV_HEADS
    ctx0 = T_LAT // CTX_LEN

    oa = na_attention(p, na_bias[li])
    tq = 256
    nq = SEQ // tq
    ob = attention(
        qkv, lambda b, k, q: (b * nq + q, k),
        [qkv, qkv],
        [lambda b, k, q: (b, kcol + k), lambda b, k, q: (ctx0 + b, kcol + k)],
        [lambda b, k, q: (b, vcol + k), lambda b, k, q: (ctx0 + b, vcol + k)],
        [SEQ, CTX_LEN], (BATCH, GQA_KV_HEADS, nq), tq, grp, 1.0,
        T_LAT, GQA_Q_HEADS * HEAD_DIM, lambda b, k, q: (b * nq + q, k), "gqa_latent")
    oac = attention(
        p, lambda b, h: (ctx0 + b, h),
        [p],
        [lambda b, h: (ctx0 + b, NA_HEADS + h)],
        [lambda b, h: (ctx0 + b, 2 * NA_HEADS + h)],
        [CTX_LEN], (BATCH, NA_HEADS), CTX_LEN, 1, ATT_SCALE * LOG2E,
        T_CTX, NA_HEADS * HEAD_DIM, lambda b, h: (b, h), "na_context")
    obc = attention(
        qkv, lambda b, k: (ctx0 + b, k),
        [qkv],
        [lambda b, k: (ctx0 + b, kcol + k)],
        [lambda b, k: (ctx0 + b, vcol + k)],
        [CTX_LEN], (BATCH, GQA_KV_HEADS), CTX_LEN, grp, 1.0,
        T_CTX, GQA_Q_HEADS * HEAD_DIM, lambda b, k: (b, k), "gqa_context")
    oa_all = jnp.concatenate([oa, oac], axis=0)
    ob_all = jnp.concatenate([ob, obc], axis=0)
    return proj_residual(oa_all, ob_all, w_out, li, xs, mod, layer, 2)


def s5_layer(xs, mod, layer, norm1_g, w_in, chunk_ops, d_skip, w_glu):
    li = layer // 2
    u = norm_mod_matmul(xs, norm1_g, mod, w_in, layer, li, 0, 1)
    wst, wy, a1, a2 = (w[li] for w in chunk_ops)
    perm = s5_lane_permutation()
    y_lat, y_ctx = s5_unpack(s5_core(s5_pack(u, perm), wst, wy, a1, a2), perm.T)
    y = jnp.concatenate([y_lat, y_ctx], axis=0)
    return glu_residual(u, y, d_skip, w_glu, li, xs, mod, layer, 2)


def moe_layer(xs, slots, mod, layer, norm2_g, w_coarse, b_coarse, w_fine, b_fine, w_gate, w_up, w_down):
    wf = w_fine[layer].transpose(1, 0, 2).reshape(D_MODEL, MOE_EXPERTS)
    w_route = jnp.concatenate([w_coarse[layer], wf], axis=1).astype(F32)
    w_route = jnp.pad(w_route, ((0, 0), (0, ROUTE_LANES - w_route.shape[1])))
    b_route = jnp.concatenate([b_coarse[layer], b_fine[layer].reshape(-1)]).astype(F32)
    b_route = jnp.pad(b_route, (0, ROUTE_LANES - b_route.shape[0])).reshape(1, ROUTE_LANES)
    h, route, counts = moe_route(xs, norm2_g, mod, layer, 3, 4, w_route, b_route)
    dest, tile_expert, n_used = moe_plan(route, counts)
    hs = moe_dispatch(h, dest, slots)
    ys = moe_experts(hs, tile_expert, n_used, w_gate, w_up, w_down, layer)
    return moe_combine(ys, dest, route, xs, mod, layer, 5), hs


def kernel(x, c, ctx, c_ctx, ada_w, ada_b, norm1_g, norm2_g, final_g, attn_w_in, attn_w_out, na_rpb, q_gain, k_gain, s5_w_in, s5_lam_re, s5_lam_im, s5_log_dt, s5_b_re, s5_b_im, s5_c_re, s5_c_im, s5_d, s5_w_glu, moe_w_coarse, moe_b_coarse, moe_w_fine, moe_b_fine, moe_w_gate, moe_w_up, moe_w_down):
    xs = jnp.concatenate([x.reshape(T_LAT, D_MODEL), ctx.reshape(T_CTX, D_MODEL)], axis=0)
    c8 = jnp.concatenate([c, c_ctx[None, :], jnp.zeros((MOD_ROWS - BATCH - 1, D_MODEL), F32)], axis=0)
    mod = ada_mod(c8, ada_w, ada_b).reshape(DEPTH * MOD_ROWS, 1, 6 * D_MODEL)
    cos_t, sin_t = rope_tables()
    attn_w_in, attn_w_out, s5_w_in, s5_w_glu = (w.astype(BF16) for w in (attn_w_in, attn_w_out, s5_w_in, s5_w_glu))
    chunk_ops = jax.vmap(s5_weights)(s5_lam_re, s5_lam_im, s5_log_dt, s5_b_re, s5_b_im, s5_c_re, s5_c_im)
    na_bias = jax.vmap(na_bias_table)(na_rpb)
    slots = moe_slot_buffer()
    for layer in range(DEPTH):
        if layer % 2 == 0:
            xs = attention_layer(xs, mod, layer, norm1_g, attn_w_in, attn_w_out, na_bias, q_gain, k_gain,
                                 cos_t, sin_t)
        else:
            xs = s5_layer(xs, mod, layer, norm1_g, s5_w_in, chunk_ops, s5_d, s5_w_glu)
        xs, slots = moe_layer(xs, slots, mod, layer, norm2_g, moe_w_coarse, moe_b_coarse, moe_w_fine,
                              moe_b_fine, moe_w_gate, moe_w_up, moe_w_down)
    return final_norm(xs, final_g, T_LAT).reshape(BATCH, SEQ, D_MODEL)
```
